```python
import math
import jax
import jax.numpy as jnp
from jax import lax
import numpy as np

D_MODEL = 1024
BATCH = 32
SEQ = 2048
DEPTH = 4

GRID_W = 64
CTX_LEN = 256
N_EVEN = (DEPTH + 1) // 2
N_ODD = DEPTH // 2
EPS = 1e-6
CONV_W = 3

GLA_HEADS = 4
GLA_DK = 64
GLA_DV = 128
GLA_RANK = 16
GLA_TAU = 16.0
GLA_CHUNK = 64
GLA_QK = GLA_HEADS * GLA_DK
GLA_V = GLA_HEADS * GLA_DV

HY_WIDTH = 512
HY_EMB = 33
HY_BANDS = (HY_EMB - 1) // 2
HY_HIDDEN = 64
HY_FAST_DECAY = 0.3
HY_SLOW_DECAY = 1.5
HY_TARGET = 1e-2

EV_SPLIT = (GLA_QK, 2 * GLA_QK, 2 * GLA_QK + GLA_V, 2 * GLA_QK + 2 * GLA_V,
            2 * GLA_QK + 2 * GLA_V + GLA_RANK, 2 * GLA_QK + 2 * GLA_V + 2 * GLA_RANK)
EV_IN = 2 * GLA_QK + 2 * GLA_V + 2 * GLA_RANK + 3 * HY_WIDTH
EV_MIX = GLA_V + HY_WIDTH

SC_WIDTH = D_MODEL

N_GROUPS = 4
EXP_PER_GROUP = 8
N_EXPERTS = N_GROUPS * EXP_PER_GROUP
TOP_K = 2
D_EXPERT = 512
MOE_BLOCK = 128

kernel_name = 'hybrid_gla_hyena_shortconv_hmoe_dit'


def rmsnorm(x, g):
    xf = x.astype(jnp.float32)
    y = xf * lax.rsqrt(jnp.mean(xf * xf, axis=-1, keepdims=True) + EPS)
    return (y * g.astype(jnp.float32)).astype(x.dtype)


def modulate(x, shift, scale):
    return x * (1 + scale) + shift


def dwconv3(x, w, b=None):
    xp = jnp.pad(x, ((0, 0), (1, 1), (0, 0)))
    y = w[0] * xp[:, :-2] + w[1] * xp[:, 1:-1] + w[2] * xp[:, 2:]
    return y if b is None else y + b


def flip(a):
    return jnp.flip(a, axis=1)


def to_col_major(h):
    B, S, D = h.shape
    rows = S // GRID_W
    return h.reshape(B, rows, GRID_W, D).transpose(0, 2, 1, 3).reshape(B, S, D)


def to_row_major(h):
    B, S, D = h.shape
    rows = S // GRID_W
    return h.reshape(B, GRID_W, rows, D).transpose(0, 2, 1, 3).reshape(B, S, D)


def gla_chunked(q, k, v, g, s0):
    B, L, H, dk = q.shape
    dv = v.shape[-1]
    C = GLA_CHUNK
    n = L // C
    f32 = jnp.float32
    q = q.astype(f32).reshape(B, n, C, H, dk)
    k = k.astype(f32).reshape(B, n, C, H, dk)
    v = v.astype(f32).reshape(B, n, C, H, dv)
    b = jnp.cumsum(g.astype(f32).reshape(B, n, C, H, dk), axis=2)
    b_last = b[:, :, -1]
    qd = q * jnp.exp(b)
    kd = k * jnp.exp(-b)
    kr = k * jnp.exp(b_last[:, :, None] - b)
    mask = jnp.tril(jnp.ones((C, C), dtype=bool))
    a = jnp.where(mask, jnp.einsum('bnthd,bnshd->bnhts', qd, kd), 0.0)
    o_intra = jnp.einsum('bnhts,bnshe->bnthe', a, v)
    ds = jnp.einsum('bnshd,bnshe->bnhde', kr, v)
    decay = jnp.exp(b_last)

    def step(s, inp):
        d, dsn = inp
        return d[..., None] * s + dsn, s

    s_final, s_before = lax.scan(step, s0, (jnp.moveaxis(decay, 1, 0), jnp.moveaxis(ds, 1, 0)))
    s_before = jnp.moveaxis(s_before, 0, 1)
    o_inter = jnp.einsum('bnthd,bnhde->bnthe', qd, s_before)
    return (o_intra + o_inter).reshape(B, L, H, dv), s_final


def gla_split(p, wa_f, ba_f, wa_b, ba_b):
    B, L, _ = p.shape
    q, k, v, g, rf, rb, hy = jnp.split(p, EV_SPLIT, axis=-1)
    q = q.reshape(B, L, GLA_HEADS, GLA_DK) * GLA_DK ** -0.5
    k = k.reshape(B, L, GLA_HEADS, GLA_DK)
    v = v.reshape(B, L, GLA_HEADS, GLA_DV)
    gf = (jax.nn.log_sigmoid((rf @ wa_f + ba_f).astype(jnp.float32)) / GLA_TAU).reshape(B, L, GLA_HEADS, GLA_DK)
    gb = (jax.nn.log_sigmoid((rb @ wa_b + ba_b).astype(jnp.float32)) / GLA_TAU).reshape(B, L, GLA_HEADS, GLA_DK)
    return q, k, v, g, gf, gb, hy


def gla_readout(o, g, gain):
    B, L = o.shape[:2]
    o = o * lax.rsqrt(jnp.mean(o * o, axis=-1, keepdims=True) + EPS) * gain.astype(jnp.float32)
    return o.reshape(B, L, GLA_V).astype(g.dtype) * jax.nn.silu(g)


def hyena_filters(L, w1, b1, f1, w2, b2, f2, w3):
    f32 = jnp.float32
    t = jnp.linspace(0.0, 1.0, L, dtype=f32)[:, None]
    pos = jnp.arange(L, dtype=f32)[:, None]
    bands = jnp.linspace(1e-4, HY_BANDS - 1, HY_BANDS, dtype=f32)[None]
    ang = (2.0 * math.pi / L) * pos * bands
    z = jnp.concatenate([t, jnp.cos(ang), jnp.sin(ang)], axis=-1)
    hh = jnp.sin(f1 * (z @ w1 + b1))
    hh = jnp.sin(f2 * (hh @ w2 + b2))
    hh = (hh @ w3).astype(f32)
    max_decay = math.log(HY_TARGET) / HY_FAST_DECAY
    min_decay = math.log(HY_TARGET) / HY_SLOW_DECAY
    deltas = jnp.linspace(min_decay, max_decay, HY_WIDTH, dtype=f32)
    window = jnp.exp(-t * jnp.abs(deltas)[None])
    hh = hh.reshape(L, 2, HY_WIDTH) * window[:, None]
    return hh[:, 0], hh[:, 1]


def bidir_fftconv(u, h_fwd, h_bwd, skip):
    L = u.shape[1]
    n = 2 * L
    uf = u.astype(jnp.float32)
    U = jnp.fft.rfft(uf, n=n, axis=1)
    K = jnp.fft.rfft(h_fwd, n=n, axis=0) + jnp.conj(jnp.fft.rfft(h_bwd, n=n, axis=0))
    y = jnp.fft.irfft(U * K[None], n=n, axis=1)[:, :L]
    return (y + uf * skip.astype(jnp.float32)).astype(u.dtype)


def hyena_mix(p, filt, conv_w, conv_b, skip):
    pc = dwconv3(p, conv_w, conv_b)
    x0, x1, v = jnp.split(pc, 3, axis=-1)
    return x0 * bidir_fftconv(v * x1, filt[0], filt[1], skip)


def even_mixer(u, uc, w_in, w_out, wa_f, ba_f, wa_b, ba_b, gla_gain,
               conv_w, conv_b, w1, b1, f1, w2, b2, f2, w3, skip, ctx_out):
    ql, kl, vl, gl, gfl, gbl, hl = gla_split(u @ w_in, wa_f, ba_f, wa_b, ba_b)
    qc, kc, vc, gc, gfc, gbc, hc = gla_split(uc @ w_in, wa_f, ba_f, wa_b, ba_b)
    s0 = jnp.zeros((u.shape[0], GLA_HEADS, GLA_DK, GLA_DV), jnp.float32)
    oc_f, sc_f = gla_chunked(qc, kc, vc, gfc, s0)
    oc_b, sc_b = gla_chunked(flip(qc), flip(kc), flip(vc), flip(gbc), s0)
    ol_f, _ = gla_chunked(ql, kl, vl, gfl, sc_f)
    ol_b, _ = gla_chunked(flip(ql), flip(kl), flip(vl), flip(gbl), sc_b)
    hw = (w1, b1, f1, w2, b2, f2, w3)
    yl = jnp.concatenate([gla_readout(ol_f + flip(ol_b), gl, gla_gain),
                          hyena_mix(hl, hyena_filters(u.shape[1], *hw), conv_w, conv_b, skip)], axis=-1) @ w_out
    yc = None
    if ctx_out:
        yc = jnp.concatenate([gla_readout(oc_f + flip(oc_b), gc, gla_gain),
                              hyena_mix(hc, hyena_filters(uc.shape[1], *hw), conv_w, conv_b, skip)], axis=-1) @ w_out
    return yl, yc


def shortconv_mixer(u, w_in, conv_w, w_out):
    bg, cg, xin = jnp.split(u @ w_in, 3, axis=-1)
    return (bg * dwconv3(cg * xin, conv_w)) @ w_out


def grouped_experts(xt, expert, gate, w1, w3, w2):
    T, D = xt.shape
    M = T * TOP_K
    flat_e = expert.reshape(M)
    flat_t = jnp.repeat(jnp.arange(T, dtype=jnp.int32), TOP_K)
    flat_g = gate.reshape(M)
    order = jnp.argsort(flat_e)
    se = flat_e[order]
    counts = jnp.bincount(flat_e, length=N_EXPERTS)
    padded = (counts + MOE_BLOCK - 1) // MOE_BLOCK * MOE_BLOCK
    start = jnp.cumsum(counts) - counts
    pend = jnp.cumsum(padded)
    pstart = pend - padded
    dest = pstart[se] + jnp.arange(M, dtype=jnp.int32) - start[se]
    n_blocks = -(-(M + N_EXPERTS * (MOE_BLOCK - 1)) // MOE_BLOCK)
    P = n_blocks * MOE_BLOCK
    slot_t = jnp.full((P,), T, jnp.int32).at[dest].set(flat_t[order])
    slot_g = jnp.zeros((P,), gate.dtype).at[dest].set(flat_g[order])
    blk_e = jnp.minimum(jnp.searchsorted(pend, jnp.arange(n_blocks, dtype=jnp.int32) * MOE_BLOCK, side='right'),
                        N_EXPERTS - 1)
    x_pad = jnp.concatenate([xt, jnp.zeros((1, D), xt.dtype)], axis=0)

    def run_block(args):
        tok, e = args
        xb = x_pad[tok]
        return (jax.nn.silu(xb @ w1[e]) * (xb @ w3[e])) @ w2[e]

    yb = lax.map(run_block, (slot_t.reshape(n_blocks, MOE_BLOCK), blk_e))
    y = yb.reshape(P, D) * slot_g[:, None]
    return jax.ops.segment_sum(y, slot_t, num_segments=T + 1)[:T]


def hier_moe(xt, w_grp, b_grp, w_exp, b_exp, w1, w3, w2):
    T = xt.shape[0]
    lg = (xt @ w_grp + b_grp).astype(jnp.float32)
    p_grp = jax.nn.softmax(lg, axis=-1)
    p_g, grp = lax.top_k(p_grp, 1)
    le = (xt @ w_exp + b_exp).astype(jnp.float32).reshape(T, N_GROUPS, EXP_PER_GROUP)
    le = jnp.take_along_axis(le, grp[:, :, None], axis=1)[:, 0]
    top_p, top_i = lax.top_k(jax.nn.softmax(le, axis=-1), TOP_K)
    gate = p_g * top_p / jnp.sum(top_p, axis=-1, keepdims=True)
    expert = grp * EXP_PER_GROUP + top_i
    return grouped_experts(xt, expert, gate.astype(xt.dtype), w1, w3, w2)


def setup_inputs(seed: int = 0) -> dict:
    key = jax.random.key(seed)
    ks = iter(jax.random.split(key, 48))
    D = D_MODEL

    def nrm(shape, scale):
        return scale * jax.random.normal(next(ks), shape, jnp.float32)

    return {
        'x': nrm((BATCH, SEQ, D), 1.0),
        'c': nrm((BATCH, D), 1.0),
        'ctx': nrm((BATCH, CTX_LEN, D), 1.0),
        'c_ctx': nrm((D,), 1.0),
        'mod_w': nrm((DEPTH, D, 6 * D), 0.5 * D ** -0.5),
        'mod_b': nrm((DEPTH, 6 * D), 0.02),
        'norm_mix': 1.0 + nrm((DEPTH, D), 0.02),
        'norm_ffn': 1.0 + nrm((DEPTH, D), 0.02),
        'norm_final': 1.0 + nrm((D,), 0.02),
        'ev_w_in': nrm((N_EVEN, D, EV_IN), D ** -0.5),
        'ev_w_out': nrm((N_EVEN, EV_MIX, D), EV_MIX ** -0.5),
        'gla_wa_f': nrm((N_EVEN, GLA_RANK, GLA_QK), GLA_RANK ** -0.5),
        'gla_ba_f': nrm((N_EVEN, GLA_QK), 0.1),
        'gla_wa_b': nrm((N_EVEN, GLA_RANK, GLA_QK), GLA_RANK ** -0.5),
        'gla_ba_b': nrm((N_EVEN, GLA_QK), 0.1),
        'gla_norm': 1.0 + nrm((N_EVEN, GLA_DV), 0.02),
        'hy_conv_w': nrm((N_EVEN, CONV_W, 3 * HY_WIDTH), CONV_W ** -0.5),
        'hy_conv_b': nrm((N_EVEN, 3 * HY_WIDTH), 0.02),
        'hy_w1': nrm((N_EVEN, HY_EMB, HY_HIDDEN), HY_EMB ** -0.5),
        'hy_b1': nrm((N_EVEN, HY_HIDDEN), 0.1),
        'hy_f1': 1.0 + nrm((N_EVEN, HY_HIDDEN), 0.02),
        'hy_w2': nrm((N_EVEN, HY_HIDDEN, HY_HIDDEN), HY_HIDDEN ** -0.5),
        'hy_b2': nrm((N_EVEN, HY_HIDDEN), 0.1),
        'hy_f2': 1.0 + nrm((N_EVEN, HY_HIDDEN), 0.02),
        'hy_w3': nrm((N_EVEN, HY_HIDDEN, 2 * HY_WIDTH), 0.02),
        'hy_skip': nrm((N_EVEN, HY_WIDTH), 0.5),
        'od_w_in': nrm((N_ODD, D, 3 * SC_WIDTH), D ** -0.5),
        'od_conv_w': nrm((N_ODD, CONV_W, SC_WIDTH), CONV_W ** -0.5),
        'od_w_out': nrm((N_ODD, SC_WIDTH, D), SC_WIDTH ** -0.5),
        'rt_w_grp': nrm((DEPTH, D, N_GROUPS), D ** -0.5),
        'rt_b_grp': nrm((DEPTH, N_GROUPS), 0.01),
        'rt_w_exp': nrm((DEPTH, D, N_EXPERTS), D ** -0.5),
        'rt_b_exp': nrm((DEPTH, N_EXPERTS), 0.01),
        'ex_w1': nrm((DEPTH, N_EXPERTS, D, D_EXPERT), D ** -0.5),
        'ex_w3': nrm((DEPTH, N_EXPERTS, D, D_EXPERT), D ** -0.5),
        'ex_w2': nrm((DEPTH, N_EXPERTS, D_EXPERT, D), D_EXPERT ** -0.5),
    }


def reference(x, c, ctx, c_ctx, mod_w, mod_b, norm_mix, norm_ffn, norm_final,
              ev_w_in, ev_w_out, gla_wa_f, gla_ba_f, gla_wa_b, gla_ba_b, gla_norm,
              hy_conv_w, hy_conv_b, hy_w1, hy_b1, hy_f1, hy_w2, hy_b2, hy_f2, hy_w3, hy_skip,
              od_w_in, od_conv_w, od_w_out,
              rt_w_grp, rt_b_grp, rt_w_exp, rt_b_exp, ex_w1, ex_w3, ex_w2):
    B, S, D = x.shape
    Lc = ctx.shape[1]
    h, hc = x, ctx
    s_lat = jax.nn.silu(c)
    s_ctx = jax.nn.silu(c_ctx)
    for l in range(DEPTH):
        i = l // 2
        even = l % 2 == 0
        ctx_out = l < DEPTH - 1
        col_major = i % 2 == 1
        m_lat = jnp.split((s_lat @ mod_w[l] + mod_b[l])[:, None, :], 6, axis=-1)
        m_ctx = jnp.split(s_ctx @ mod_w[l] + mod_b[l], 6, axis=-1)
        u = modulate(rmsnorm(h, norm_mix[l]), m_lat[0], m_lat[1])
        if col_major:
            u = to_col_major(u)
        uc = modulate(rmsnorm(hc, norm_mix[l]), m_ctx[0], m_ctx[1]) if (even or ctx_out) else None
        if even:
            y, yc = even_mixer(u, uc, ev_w_in[i], ev_w_out[i], gla_wa_f[i], gla_ba_f[i], gla_wa_b[i], gla_ba_b[i],
                               gla_norm[i], hy_conv_w[i], hy_conv_b[i], hy_w1[i], hy_b1[i], hy_f1[i],
                               hy_w2[i], hy_b2[i], hy_f2[i], hy_w3[i], hy_skip[i], ctx_out)
        else:
            y = shortconv_mixer(u, od_w_in[i], od_conv_w[i], od_w_out[i])
            yc = shortconv_mixer(uc, od_w_in[i], od_conv_w[i], od_w_out[i]) if ctx_out else None
        if col_major:
            y = to_row_major(y)
        h = h + m_lat[2] * y
        v = modulate(rmsnorm(h, norm_ffn[l]), m_lat[3], m_lat[4]).reshape(B * S, D)
        if ctx_out:
            hc = hc + m_ctx[2] * yc
            vc = modulate(rmsnorm(hc, norm_ffn[l]), m_ctx[3], m_ctx[4]).reshape(B * Lc, D)
            f = hier_moe(jnp.concatenate([vc, v], axis=0), rt_w_grp[l], rt_b_grp[l], rt_w_exp[l], rt_b_exp[l],
                         ex_w1[l], ex_w3[l], ex_w2[l])
            hc = hc + m_ctx[5] * f[:B * Lc].reshape(B, Lc, D)
            f = f[B * Lc:]
        else:
            f = hier_moe(v, rt_w_grp[l], rt_b_grp[l], rt_w_exp[l], rt_b_exp[l], ex_w1[l], ex_w3[l], ex_w2[l])
        h = h + m_lat[5] * f.reshape(B, S, D)
    return rmsnorm(h, norm_final)
```

```python
import functools
import math

import numpy as np
import jax
import jax.numpy as jnp
from jax import lax
from jax.experimental import pallas as pl
from jax.experimental.pallas import tpu as pltpu

F32 = jnp.float32
BF16 = jnp.bfloat16

EPS = 1e-6
GRID_W = 64

GLA_HEADS = 4
GLA_DK = 64
GLA_DV = 128
GLA_RANK = 16
GLA_TAU = 16.0
GLA_CHUNK = 64
GLA_QK = GLA_HEADS * GLA_DK
GLA_V = GLA_HEADS * GLA_DV

HY_WIDTH = 512
HY_EMB = 33
HY_BANDS = (HY_EMB - 1) // 2
HY_HIDDEN = 64
HY_FAST_DECAY = 0.3
HY_SLOW_DECAY = 1.5
HY_TARGET = 1e-2

N_GROUPS = 4
EXP_PER_GROUP = 8
N_EXPERTS = N_GROUPS * EXP_PER_GROUP
TOP_K = 2

LANE = 128
ROW_TILE = 256
MOE_ROWS = 256
R_COLS = LANE
VMEM_LIMIT = 56 * 1024 * 1024


def _cparams(*sem):
    return pltpu.CompilerParams(dimension_semantics=sem, vmem_limit_bytes=VMEM_LIMIT)


def _split_bf16(a):
    hi = a.astype(BF16)
    lo = (a - hi.astype(F32)).astype(BF16)
    return hi, lo


def _dot(a, b):
    return jnp.dot(a, b, preferred_element_type=F32)


def _dot_nt(a, b):
    return lax.dot_general(a, b, (((1,), (1,)), ((), ())), preferred_element_type=F32)


def _dot_tn(a, b):
    return lax.dot_general(a, b, (((0,), (0,)), ((), ())), preferred_element_type=F32)


def _dot3(a, b):
    ah, al = _split_bf16(a)
    bh, bl = _split_bf16(b)
    return _dot(ah, bh) + _dot(ah, bl) + _dot(al, bh)


def _silu(x):
    return x / (1.0 + jnp.exp(-x))


def _log_sigmoid(x):
    return jnp.minimum(x, 0.0) - jnp.log1p(jnp.exp(-jnp.abs(x)))


def _norm_mod(x, g, shift, scale):
    y = x * lax.rsqrt(jnp.mean(x * x, axis=-1, keepdims=True) + EPS)
    return (y * g) * (1.0 + scale) + shift


def _mods_kernel(s_ref, w_ref, b_ref, o_ref):
    s = s_ref[...]
    s = _silu(s)
    o_ref[...] = _dot3(s, w_ref[...]) + b_ref[...]


def _mods(cond, mod_w, mod_b):
    depth, d, n = mod_w.shape
    r = cond.shape[0]
    tn = 1024
    return pl.pallas_call(
        _mods_kernel,
        grid=(depth, n // tn),
        in_specs=[pl.BlockSpec((r, d), lambda l, j: (0, 0)),
                  pl.BlockSpec((None, d, tn), lambda l, j: (l, 0, j)),
                  pl.BlockSpec((None, 1, tn), lambda l, j: (l, 0, j))],
        out_specs=pl.BlockSpec((None, r, tn), lambda l, j: (l, 0, j)),
        out_shape=jax.ShapeDtypeStruct((depth, r, n), F32),
        compiler_params=_cparams("parallel", "parallel"),
        name="mods",
    )(cond, mod_w, mod_b.reshape(depth, 1, n))


class _Rows:
    def __init__(self, batch, lc, s, lat_only):
        self.batch = batch
        self.nct = 0 if lat_only else batch * lc // ROW_TILE
        self.tps = s // ROW_TILE
        self.tpc = lc // ROW_TILE
        self.n_tiles = self.nct + batch * self.tps

    def mod_row(self, i):
        return jnp.where(i < self.nct, self.batch, (i - self.nct) // self.tps)

    def seq_edges(self, i):
        pos_c = i % self.tpc
        pos_l = (i - self.nct) % self.tps
        is_c = i < self.nct
        first = jnp.where(is_c, pos_c == 0, pos_l == 0)
        last = jnp.where(is_c, pos_c == self.tpc - 1, pos_l == self.tps - 1)
        return first, last


def _normmod_mm_kernel(h_ref, g_ref, mod_ref, w_ref, o_ref, *, shift_i, scale_i):
    u = _norm_mod(h_ref[...], g_ref[...], mod_ref[shift_i], mod_ref[scale_i])
    o_ref[...] = _dot(u.astype(BF16), w_ref[...])


def _normmod_mm(h, g, mods_l, w, rows, shift_i, scale_i, name):
    d = h.shape[1]
    n = w.shape[1]
    return pl.pallas_call(
        functools.partial(_normmod_mm_kernel, shift_i=shift_i, scale_i=scale_i),
        grid=(rows.n_tiles,),
        in_specs=[pl.BlockSpec((ROW_TILE, d), lambda i: (i, 0)),
                  pl.BlockSpec((1, d), lambda i: (0, 0)),
                  pl.BlockSpec((None, 6, 1, d), lambda i: (rows.mod_row(i), 0, 0, 0)),
                  pl.BlockSpec((d, n), lambda i: (0, 0))],
        out_specs=pl.BlockSpec((ROW_TILE, n), lambda i: (i, 0)),
        out_shape=jax.ShapeDtypeStruct((rows.n_tiles * ROW_TILE, n), F32),
        compiler_params=_cparams("parallel"),
        name=name,
    )(h, g.reshape(1, d), mods_l, w)


def _gla_kernel(q_ref, k_ref, v_ref, g_ref, r_ref, waf_ref, wab_ref, baf_ref, bab_ref, gain_ref,
                s0f_ref, s0b_ref, o_ref, sf_ref, sb_ref, glf_scr, glb_scr, of_scr, st_scr, *, seq_len):
    C = GLA_CHUNK
    n_chunks = seq_len // C
    lane_k = lax.broadcasted_iota(jnp.int32, (C, 2 * GLA_DK), 1)
    head_of_lane = lane_k // GLA_DK
    row = lax.broadcasted_iota(jnp.int32, (C, C), 0)
    col = lax.broadcasted_iota(jnp.int32, (C, C), 1)
    tril = row >= col
    triu = row <= col
    tri_lo = jnp.where(tril, 1.0, 0.0).astype(BF16)
    tri_up = jnp.where(triu, 1.0, 0.0).astype(BF16)
    srow = lax.broadcasted_iota(jnp.int32, (2 * GLA_DV, 2 * GLA_DK), 0) // GLA_DV
    scol = lax.broadcasted_iota(jnp.int32, (2 * GLA_DV, 2 * GLA_DK), 1) // GLA_DK
    same_head = srow == scol

    r = r_ref[...]
    glf_scr[...] = _log_sigmoid(_dot3(r, waf_ref[...]) + baf_ref[...]) * (1.0 / GLA_TAU)
    glb_scr[...] = _log_sigmoid(_dot3(r, wab_ref[...]) + bab_ref[...]) * (1.0 / GLA_TAU)

    def chunk(c, gl_scr, tri, mask):
        rows = pl.ds(pl.multiple_of(c * C, C), C)
        gl = gl_scr[rows, :]
        g_hi = gl.astype(BF16)
        g_r1 = gl - g_hi.astype(F32)
        g_mid = g_r1.astype(BF16)
        g_lo = (g_r1 - g_mid.astype(F32)).astype(BF16)
        b = _dot(tri, g_hi) + _dot(tri, g_mid) + _dot(tri, g_lo)
        b_tot = jnp.sum(gl, axis=0, keepdims=True)
        q = q_ref[rows, :] * (GLA_DK ** -0.5)
        k = k_ref[rows, :]
        v = v_ref[rows, :].astype(BF16)
        qd = q * jnp.exp(b)
        kd = (k * jnp.exp(-b)).astype(BF16)
        kr = (k * jnp.exp(b_tot - b)).astype(BF16)
        st = st_scr[...]
        o_parts = []
        for h in range(2):
            qh = jnp.where(head_of_lane == h, qd, 0.0).astype(BF16)
            a = jnp.where(mask, _dot_nt(qh, kd), 0.0).astype(BF16)
            o_parts.append(_dot(a, v[:, h * GLA_DV:(h + 1) * GLA_DV]))
        o = jnp.concatenate(o_parts, axis=1) + _dot_nt(qd.astype(BF16), st.astype(BF16))
        ds = jnp.where(same_head, _dot_tn(v, kr), 0.0)
        st_scr[...] = st * jnp.exp(b_tot) + ds
        return rows, o

    st_scr[...] = s0f_ref[...]

    def fwd_body(c, carry):
        rows, o = chunk(c, glf_scr, tri_lo, tril)
        of_scr[rows, :] = o
        return carry

    lax.fori_loop(0, n_chunks, fwd_body, 0)
    sf_ref[...] = st_scr[...]
    st_scr[...] = s0b_ref[...]
    gain = gain_ref[...]

    def bwd_body(i, carry):
        c = n_chunks - 1 - i
        rows, o = chunk(c, glb_scr, tri_up, triu)
        o = o + of_scr[rows, :]
        outs = []
        for h in range(2):
            oh = o[:, h * GLA_DV:(h + 1) * GLA_DV]
            oh = oh * lax.rsqrt(jnp.mean(oh * oh, axis=-1, keepdims=True) + EPS) * gain
            outs.append(oh)
        o_ref[rows, :] = jnp.concatenate(outs, axis=1) * _silu(g_ref[rows, :])
        return carry

    lax.fori_loop(0, n_chunks, bwd_body, 0)
    sb_ref[...] = st_scr[...]


def _gla(p, waf, wab, baf, bab, gain, s0f, s0b, batch, seq_len, blk0):
    dk2, dv2 = 2 * GLA_DK, 2 * GLA_DV
    seq = lambda width, cb: pl.BlockSpec((seq_len, width), lambda b, hp: (blk0 + b, cb(hp)))
    state_spec = pl.BlockSpec((None, None, dv2, dk2), lambda b, hp: (b, hp, 0, 0))
    in_specs = [seq(dk2, lambda hp: hp),
                seq(dk2, lambda hp: GLA_QK // dk2 + hp),
                seq(dv2, lambda hp: 2 * GLA_QK // dv2 + hp),
                seq(dv2, lambda hp: (2 * GLA_QK + GLA_V) // dv2 + hp),
                seq(R_COLS, lambda hp: (2 * GLA_QK + 2 * GLA_V + 3 * HY_WIDTH) // R_COLS),
                pl.BlockSpec((R_COLS, dk2), lambda b, hp: (0, hp)),
                pl.BlockSpec((R_COLS, dk2), lambda b, hp: (0, hp)),
                pl.BlockSpec((1, dk2), lambda b, hp: (0, hp)),
                pl.BlockSpec((1, dk2), lambda b, hp: (0, hp)),
                pl.BlockSpec((1, GLA_DV), lambda b, hp: (0, 0)),
                state_spec, state_spec]
    state_shape = jax.ShapeDtypeStruct((batch, 2, dv2, dk2), F32)
    return pl.pallas_call(
        functools.partial(_gla_kernel, seq_len=seq_len),
        grid=(batch, 2),
        in_specs=in_specs,
        out_specs=[pl.BlockSpec((seq_len, dv2), lambda b, hp: (b, hp)), state_spec, state_spec],
        out_shape=[jax.ShapeDtypeStruct((batch * seq_len, GLA_V), F32), state_shape, state_shape],
        scratch_shapes=[pltpu.VMEM((seq_len, dk2), F32), pltpu.VMEM((seq_len, dk2), F32),
                        pltpu.VMEM((seq_len, dv2), F32), pltpu.VMEM((dv2, dk2), F32)],
        compiler_params=_cparams("parallel", "parallel"),
        name="gla",
    )(p, p, p, p, p, waf, wab, baf, bab, gain, s0f, s0b)


def _conv3(m, prev_row, next_row, w_ref):
    n = m.shape[0]
    ridx = lax.broadcasted_iota(jnp.int32, m.shape, 0)
    m_prev = jnp.where(ridx == 0, prev_row, pltpu.roll(m, 1, 0))
    m_next = jnp.where(ridx == n - 1, next_row, pltpu.roll(m, n - 1, 0))
    return w_ref[0:1, :] * m_prev + w_ref[1:2, :] * m + w_ref[2:3, :] * m_next


def _halo_specs(width, col_block, t_rows):
    g = ROW_TILE // 8
    last = t_rows // 8 - 1
    prev = pl.BlockSpec((8, width), lambda i: (jnp.maximum(i * g - 1, 0), col_block))
    nxt = pl.BlockSpec((8, width), lambda i: (jnp.minimum((i + 1) * g, last), col_block))
    return prev, nxt


def _hyena_pre_kernel(x0_ref, x1_ref, v_ref, x0p_ref, x0n_ref, x1p_ref, x1n_ref, vp_ref, vn_ref,
                      w_ref, b_ref, x0c_ref, z_ref, *, rows):
    first, last = rows.seq_edges(pl.program_id(0))
    keep_p = jnp.where(first, 0.0, 1.0)
    keep_n = jnp.where(last, 0.0, 1.0)
    hw = HY_WIDTH

    def conv(ref, p_ref, n_ref, j):
        w = w_ref.at[:, j * hw:(j + 1) * hw]
        y = _conv3(ref[...], p_ref[7:8, :] * keep_p, n_ref[0:1, :] * keep_n, w)
        return y + b_ref[:, j * hw:(j + 1) * hw]

    x0c_ref[...] = conv(x0_ref, x0p_ref, x0n_ref, 0)
    z_ref[...] = conv(v_ref, vp_ref, vn_ref, 2) * conv(x1_ref, x1p_ref, x1n_ref, 1)


def _hyena_pre(p, conv_w, conv_b, rows):
    t_rows = p.shape[0]
    hw = HY_WIDTH
    cb0 = (2 * GLA_QK + 2 * GLA_V) // hw
    in_specs = [pl.BlockSpec((ROW_TILE, hw), lambda i, j=j: (i, cb0 + j)) for j in range(3)]
    for j in range(3):
        in_specs.extend(_halo_specs(hw, cb0 + j, t_rows))
    in_specs += [pl.BlockSpec((3, 3 * hw), lambda i: (0, 0)), pl.BlockSpec((1, 3 * hw), lambda i: (0, 0))]
    out_spec = pl.BlockSpec((ROW_TILE, hw), lambda i: (i, 0))
    shape = jax.ShapeDtypeStruct((rows.n_tiles * ROW_TILE, hw), F32)
    return pl.pallas_call(
        functools.partial(_hyena_pre_kernel, rows=rows),
        grid=(rows.n_tiles,),
        in_specs=in_specs,
        out_specs=[out_spec, out_spec],
        out_shape=[shape, shape],
        compiler_params=_cparams("parallel"),
        name="hyena_pre",
    )(p, p, p, p, p, p, p, p, p, conv_w, conv_b.reshape(1, 3 * hw))


def _filter_kernel(z_ref, w1_ref, b1_ref, f1_ref, w2_ref, b2_ref, f2_ref, w3_ref, win_ref, o_ref):
    hh = jnp.sin(f1_ref[...] * (_dot3(z_ref[...], w1_ref[...]) + b1_ref[...]))
    hh = jnp.sin(f2_ref[...] * (_dot3(hh, w2_ref[...]) + b2_ref[...]))
    win = win_ref[...]
    o_ref[...] = _dot3(hh, w3_ref[...]) * jnp.concatenate([win, win], axis=1)


@functools.lru_cache(maxsize=None)
def _filter_features(L):
    t = np.linspace(0.0, 1.0, L, dtype=np.float32)[:, None]
    pos = np.arange(L, dtype=np.float32)[:, None]
    bands = np.linspace(1e-4, HY_BANDS - 1, HY_BANDS, dtype=np.float32)[None]
    ang = (np.float32(2.0 * math.pi / L) * pos * bands).astype(np.float32)
    z = np.concatenate([t, np.cos(ang), np.sin(ang)], axis=-1).astype(np.float32)
    z = np.pad(z, ((0, 0), (0, LANE - HY_EMB)))
    max_decay = math.log(HY_TARGET) / HY_FAST_DECAY
    min_decay = math.log(HY_TARGET) / HY_SLOW_DECAY
    deltas = np.linspace(min_decay, max_decay, HY_WIDTH, dtype=np.float32)
    window = np.exp(-t * np.abs(deltas)[None]).astype(np.float32)
    return z, window


def _hyena_filters(L, w1, b1, f1, w2, b2, f2, w3):
    z, window = _filter_features(L)
    w1p = jnp.pad(w1, ((0, LANE - HY_EMB), (0, 0)))
    tl = min(L, 512)
    full = lambda a: pl.BlockSpec(a.shape, lambda i: (0,) * a.ndim)
    row = lambda a: a.reshape(1, -1)
    ops = [w1p, row(b1), row(f1), w2, row(b2), row(f2), w3]
    return pl.pallas_call(
        _filter_kernel,
        grid=(L // tl,),
        in_specs=[pl.BlockSpec((tl, LANE), lambda i: (i, 0))] + [full(a) for a in ops]
                 + [pl.BlockSpec((tl, HY_WIDTH), lambda i: (i, 0))],
        out_specs=pl.BlockSpec((tl, 2 * HY_WIDTH), lambda i: (i, 0)),
        out_shape=jax.ShapeDtypeStruct((L, 2 * HY_WIDTH), F32),
        compiler_params=_cparams("parallel"),
        name="hyena_filters",
    )(jnp.asarray(z), *ops, jnp.asarray(window))


def _freq_tile(L):
    return min(2 * L, 512)


@functools.lru_cache(maxsize=None)
def _dft_tables(L):
    n = 2 * L
    tf = _freq_tile(L)
    half = tf // 2
    t = np.arange(L, dtype=np.int64)[None, :]
    fm = np.zeros((n, L), np.float64)
    scale = np.zeros((n, 1), np.float64)
    sign = np.zeros((n, 1), np.float64)
    for j in range(n // tf):
        k = (np.arange(half, dtype=np.int64) + j * half)[:, None]
        ang = 2.0 * np.pi * ((k * t) % n).astype(np.float64) / n
        fm[j * tf:j * tf + half] = np.cos(ang)
        fm[j * tf + half:(j + 1) * tf] = -np.sin(ang)
        scale[j * tf:(j + 1) * tf] = 2.0 / n
        sign[j * tf:j * tf + half] = 1.0
        sign[j * tf + half:(j + 1) * tf] = -1.0
    fm[half] = np.cos(np.pi * t[0])
    scale[0] = 1.0 / n
    scale[half] = 1.0 / n
    sign[half] = 1.0
    return (fm.astype(np.float32), np.ascontiguousarray(fm.T).astype(np.float32),
            scale.astype(np.float32), sign.astype(np.float32))


def _ktab_kernel(f_ref, h_ref, scale_ref, sign_ref, o_ref):
    hw = HY_WIDTH
    hh, hl = _split_bf16(h_ref[...])
    f = f_ref[...]
    kk = _dot(f, hh) + _dot(f, hl)
    o_ref[...] = scale_ref[...] * (kk[:, :hw] + sign_ref[...] * kk[:, hw:])


def _hyena_ktab(filt, L):
    fm, _, scale, sign = _dft_tables(L)
    n = 2 * L
    tf = _freq_tile(L)
    return pl.pallas_call(
        _ktab_kernel,
        grid=(n // tf,),
        in_specs=[pl.BlockSpec((tf, L), lambda j: (j, 0)),
                  pl.BlockSpec((L, 2 * HY_WIDTH), lambda j: (0, 0)),
                  pl.BlockSpec((tf, 1), lambda j: (j, 0)),
                  pl.BlockSpec((tf, 1), lambda j: (j, 0))],
        out_specs=pl.BlockSpec((tf, HY_WIDTH), lambda j: (j, 0)),
        out_shape=jax.ShapeDtypeStruct((n, HY_WIDTH), F32),
        compiler_params=_cparams("parallel"),
        name="hyena_ktab",
    )(jnp.asarray(fm, dtype=BF16), filt, jnp.asarray(scale), jnp.asarray(sign))


def _hyena_conv_kernel(z_ref, x0_ref, f_ref, ft_ref, k_ref, skip_ref, o_ref, zb_scr, acc_scr, *, tf):
    j = pl.program_id(1)
    half = tf // 2

    @pl.when(j == 0)
    def _():
        zb_scr[...] = z_ref[...].astype(BF16)
        acc_scr[...] = jnp.zeros_like(acc_scr)

    zf = _dot(f_ref[...], zb_scr[...])
    re, im = zf[:half], zf[half:]
    kre, kim = k_ref[:half, :], k_ref[half:, :]
    ridx = lax.broadcasted_iota(jnp.int32, re.shape, 0)
    mix = jnp.where(jnp.logical_and(j == 0, ridx == 0), 0.0, 1.0)
    yre = re * kre - mix * (im * kim)
    yim = mix * (re * kim) + im * jnp.where(mix == 0.0, kim, kre)
    y = jnp.concatenate([yre, yim], axis=0).astype(BF16)
    acc_scr[...] += _dot(ft_ref[...], y)

    @pl.when(j == pl.num_programs(1) - 1)
    def _():
        o_ref[...] = x0_ref[...] * (acc_scr[...] + z_ref[...] * skip_ref[...])


def _hyena_conv(z, x0c, ktab, skip, batch, L, blk0):
    fm, fmt, _, _ = _dft_tables(L)
    n = 2 * L
    tf = _freq_tile(L)
    hw = HY_WIDTH
    return pl.pallas_call(
        functools.partial(_hyena_conv_kernel, tf=tf),
        grid=(batch, n // tf),
        in_specs=[pl.BlockSpec((L, hw), lambda b, j: (blk0 + b, 0)),
                  pl.BlockSpec((L, hw), lambda b, j: (blk0 + b, 0)),
                  pl.BlockSpec((tf, L), lambda b, j: (j, 0)),
                  pl.BlockSpec((L, tf), lambda b, j: (0, j)),
                  pl.BlockSpec((tf, hw), lambda b, j: (j, 0)),
                  pl.BlockSpec((1, hw), lambda b, j: (0, 0))],
        out_specs=pl.BlockSpec((L, hw), lambda b, j: (b, 0)),
        out_shape=jax.ShapeDtypeStruct((batch * L, hw), F32),
        scratch_shapes=[pltpu.VMEM((L, hw), BF16), pltpu.VMEM((L, hw), F32)],
        compiler_params=_cparams("parallel", "arbitrary"),
        name="hyena_conv",
    )(z, x0c, jnp.asarray(fm, dtype=BF16), jnp.asarray(fmt, dtype=BF16), ktab, skip.reshape(1, hw))


def _mix_out_kernel(ac_ref, al_ref, bc_ref, bl_ref, wa_ref, wb_ref, h_ref, mod_ref, o_ref, *, nct):
    is_ctx = pl.program_id(0) < nct
    a = jnp.where(is_ctx, ac_ref[...], al_ref[...])
    b = jnp.where(is_ctx, bc_ref[...], bl_ref[...])
    y = _dot(a.astype(BF16), wa_ref[...]) + _dot(b.astype(BF16), wb_ref[...])
    o_ref[...] = h_ref[...] + mod_ref[2] * y


def _mix_out(a_ctx, a_lat, b_ctx, b_lat, w_out, h, mods_l, rows):
    d = h.shape[1]
    ka, kb = a_ctx.shape[1], b_ctx.shape[1]
    nct = rows.nct
    ctx_spec = lambda k: pl.BlockSpec((ROW_TILE, k), lambda i: (jnp.minimum(i, nct - 1), 0))
    lat_spec = lambda k: pl.BlockSpec((ROW_TILE, k), lambda i: (jnp.maximum(i - nct, 0), 0))
    return pl.pallas_call(
        functools.partial(_mix_out_kernel, nct=nct),
        grid=(rows.n_tiles,),
        in_specs=[ctx_spec(ka), lat_spec(ka), ctx_spec(kb), lat_spec(kb),
                  pl.BlockSpec((ka, d), lambda i: (0, 0)),
                  pl.BlockSpec((kb, d), lambda i: (ka // kb, 0)),
                  pl.BlockSpec((ROW_TILE, d), lambda i: (i, 0)),
                  pl.BlockSpec((None, 6, 1, d), lambda i: (rows.mod_row(i), 0, 0, 0))],
        out_specs=pl.BlockSpec((ROW_TILE, d), lambda i: (i, 0)),
        out_shape=jax.ShapeDtypeStruct((rows.n_tiles * ROW_TILE, d), F32),
        compiler_params=_cparams("parallel"),
        name="mix_out",
    )(a_ctx, a_lat, b_ctx, b_lat, w_out, w_out, h, mods_l)


def _shortconv_out_kernel(bg_ref, cg_ref, xi_ref, cgp_ref, cgn_ref, xip_ref, xin_ref, cw_ref, w_ref,
                          h_ref, mod_ref, o_ref, *, rows):
    first, last = rows.seq_edges(pl.program_id(0))
    keep_p = jnp.where(first, 0.0, 1.0)
    keep_n = jnp.where(last, 0.0, 1.0)
    m = cg_ref[...] * xi_ref[...]
    m_prev = cgp_ref[7:8, :] * xip_ref[7:8, :] * keep_p
    m_next = cgn_ref[0:1, :] * xin_ref[0:1, :] * keep_n
    y = bg_ref[...] * _conv3(m, m_prev, m_next, cw_ref)
    o_ref[...] = h_ref[...] + mod_ref[2] * _dot(y.astype(BF16), w_ref[...])


def _shortconv_out(p, conv_w, w_out, h, mods_l, rows):
    d = h.shape[1]
    t_rows = p.shape[0]
    in_specs = [pl.BlockSpec((ROW_TILE, d), lambda i, j=j: (i, j)) for j in range(3)]
    in_specs += [*_halo_specs(d, 1, t_rows), *_halo_specs(d, 2, t_rows),
                 pl.BlockSpec((3, d), lambda i: (0, 0)),
                 pl.BlockSpec((d, d), lambda i: (0, 0)),
                 pl.BlockSpec((ROW_TILE, d), lambda i: (i, 0)),
                 pl.BlockSpec((None, 6, 1, d), lambda i: (rows.mod_row(i), 0, 0, 0))]
    return pl.pallas_call(
        functools.partial(_shortconv_out_kernel, rows=rows),
        grid=(rows.n_tiles,),
        in_specs=in_specs,
        out_specs=pl.BlockSpec((ROW_TILE, d), lambda i: (i, 0)),
        out_shape=jax.ShapeDtypeStruct((rows.n_tiles * ROW_TILE, d), F32),
        compiler_params=_cparams("parallel"),
        name="shortconv_out",
    )(p, p, p, p, p, p, p, conv_w, w_out, h, mods_l)


def _router_kernel(h_ref, g_ref, mod_ref, w_ref, b_ref, v_ref, lg_ref):
    v = _norm_mod(h_ref[...], g_ref[...], mod_ref[3], mod_ref[4])
    v_ref[...] = v
    lg_ref[...] = _dot3(v, w_ref[...]) + b_ref[...]


def _router(h, g, mods_l, w_rt, b_rt, rows):
    d = h.shape[1]
    nt = rows.n_tiles * ROW_TILE
    return pl.pallas_call(
        _router_kernel,
        grid=(rows.n_tiles,),
        in_specs=[pl.BlockSpec((ROW_TILE, d), lambda i: (i, 0)),
                  pl.BlockSpec((1, d), lambda i: (0, 0)),
                  pl.BlockSpec((None, 6, 1, d), lambda i: (rows.mod_row(i), 0, 0, 0)),
                  pl.BlockSpec((d, LANE), lambda i: (0, 0)),
                  pl.BlockSpec((1, LANE), lambda i: (0, 0))],
        out_specs=[pl.BlockSpec((ROW_TILE, d), lambda i: (i, 0)),
                   pl.BlockSpec((ROW_TILE, LANE), lambda i: (i, 0))],
        out_shape=[jax.ShapeDtypeStruct((nt, d), F32), jax.ShapeDtypeStruct((nt, LANE), F32)],
        compiler_params=_cparams("parallel"),
        name="router",
    )(h, g.reshape(1, d), mods_l, w_rt, b_rt)


def _route(logits):
    t = logits.shape[0]
    p_grp = jax.nn.softmax(logits[:, :N_GROUPS], axis=-1)
    p_g, grp = lax.top_k(p_grp, 1)
    le = logits[:, N_GROUPS:N_GROUPS + N_EXPERTS].reshape(t, N_GROUPS, EXP_PER_GROUP)
    le = jnp.take_along_axis(le, grp[:, :, None], axis=1)[:, 0]
    top_p, top_i = lax.top_k(jax.nn.softmax(le, axis=-1), TOP_K)
    gate = p_g * top_p / jnp.sum(top_p, axis=-1, keepdims=True)
    return grp * EXP_PER_GROUP + top_i, gate


def _dispatch_plan(expert):
    t = expert.shape[0]
    m = t * TOP_K
    bm = MOE_ROWS
    flat_e = expert.reshape(m).astype(jnp.int32)
    order = jnp.argsort(flat_e).astype(jnp.int32)
    counts = jnp.bincount(flat_e, length=N_EXPERTS).astype(jnp.int32)
    padded = (counts + bm - 1) // bm * bm
    start = jnp.cumsum(counts) - counts
    pend = jnp.cumsum(padded)
    pstart = pend - padded
    n_blocks = -(-(m + N_EXPERTS * (bm - 1)) // bm)
    blk_e = jnp.minimum(jnp.searchsorted(pend, jnp.arange(n_blocks, dtype=jnp.int32) * bm, side='right'),
                        N_EXPERTS - 1).astype(jnp.int32)
    slot = jnp.arange(n_blocks * bm, dtype=jnp.int32).reshape(n_blocks, bm)
    off = slot - pstart[blk_e][:, None]
    valid = off < counts[blk_e][:, None]
    asg = order[jnp.clip(start[blk_e][:, None] + off, 0, m - 1)]
    src = jnp.where(valid, asg // TOP_K, 0)
    dst = jnp.where(valid, asg, 0)
    n_valid = jnp.sum(valid, axis=1).astype(jnp.int32)
    return blk_e, n_valid, src.reshape(n_blocks, 1, bm), dst.reshape(n_blocks, 1, bm)


def _expert_kernel(blk_e_ref, nv_ref, src_ref, dst_ref, x_hbm, w1_ref, w3_ref, w2_ref, y_hbm,
                   xbuf, ybuf, gsem, ssem):
    j = pl.program_id(0)
    nv = nv_ref[j]
    bm = xbuf.shape[0]

    def gather(r):
        return pltpu.make_async_copy(x_hbm.at[pl.ds(src_ref[0, r], 1), :], xbuf.at[pl.ds(r, 1), :], gsem)

    def scatter(r):
        return pltpu.make_async_copy(ybuf.at[pl.ds(r, 1), :], y_hbm.at[pl.ds(dst_ref[0, r], 1), :], ssem)

    @pl.when(nv > 0)
    def _():
        def g_start(r, c):
            gather(r).start()
            return c

        def g_wait(r, c):
            gather(r).wait()
            return c

        lax.fori_loop(0, bm, g_start, 0)
        lax.fori_loop(0, bm, g_wait, 0)
        x = xbuf[...].astype(BF16)
        hid = _silu(_dot(x, w1_ref[...])) * _dot(x, w3_ref[...])
        ybuf[...] = _dot(hid.astype(BF16), w2_ref[...])

        def s_start(r, c):
            scatter(r).start()
            return c

        def s_wait(r, c):
            scatter(r).wait()
            return c

        lax.fori_loop(0, nv, s_start, 0)
        lax.fori_loop(0, nv, s_wait, 0)


def _experts(v, plan, w1, w3, w2):
    blk_e, n_valid, src, dst = plan
    t, d = v.shape
    n_blocks = blk_e.shape[0]
    bm = MOE_ROWS
    de = w1.shape[2]
    grid_spec = pltpu.PrefetchScalarGridSpec(
        num_scalar_prefetch=2,
        grid=(n_blocks,),
        in_specs=[pl.BlockSpec((None, 1, bm), lambda j, e, n: (j, 0, 0), memory_space=pltpu.SMEM),
                  pl.BlockSpec((None, 1, bm), lambda j, e, n: (j, 0, 0), memory_space=pltpu.SMEM),
                  pl.BlockSpec(memory_space=pl.ANY),
                  pl.BlockSpec((None, d, de), lambda j, e, n: (e[j], 0, 0)),
                  pl.BlockSpec((None, d, de), lambda j, e, n: (e[j], 0, 0)),
                  pl.BlockSpec((None, de, d), lambda j, e, n: (e[j], 0, 0))],
        out_specs=pl.BlockSpec(memory_space=pl.ANY),
        scratch_shapes=[pltpu.VMEM((bm, d), F32), pltpu.VMEM((bm, d), F32),
                        pltpu.SemaphoreType.DMA, pltpu.SemaphoreType.DMA],
    )
    return pl.pallas_call(
        _expert_kernel,
        grid_spec=grid_spec,
        out_shape=jax.ShapeDtypeStruct((t * TOP_K, d), F32),
        compiler_params=_cparams("arbitrary"),
        name="experts",
    )(blk_e, n_valid, src, dst, v, w1, w3, w2)


def _combine_kernel(h_ref, y_ref, gate_ref, mod_ref, o_ref):
    d = h_ref.shape[1]
    g = gate_ref[...]
    f = g[:, 0:1] * y_ref[:, :d] + g[:, 1:2] * y_ref[:, d:]
    o_ref[...] = h_ref[...] + mod_ref[5] * f


def _combine(h, y2, gate, mods_l, rows):
    d = h.shape[1]
    return pl.pallas_call(
        _combine_kernel,
        grid=(rows.n_tiles,),
        in_specs=[pl.BlockSpec((ROW_TILE, d), lambda i: (i, 0)),
                  pl.BlockSpec((ROW_TILE, TOP_K * d), lambda i: (i, 0)),
                  pl.BlockSpec((ROW_TILE, TOP_K), lambda i: (i, 0)),
                  pl.BlockSpec((None, 6, 1, d), lambda i: (rows.mod_row(i), 0, 0, 0))],
        out_specs=pl.BlockSpec((ROW_TILE, d), lambda i: (i, 0)),
        out_shape=jax.ShapeDtypeStruct(h.shape, F32),
        compiler_params=_cparams("parallel"),
        name="moe_combine",
    )(h, y2.reshape(h.shape[0], TOP_K * d), gate, mods_l)


def _final_norm_kernel(h_ref, g_ref, o_ref):
    x = h_ref[...]
    o_ref[...] = x * lax.rsqrt(jnp.mean(x * x, axis=-1, keepdims=True) + EPS) * g_ref[...]


def _final_norm(h, g):
    t, d = h.shape
    return pl.pallas_call(
        _final_norm_kernel,
        grid=(t // ROW_TILE,),
        in_specs=[pl.BlockSpec((ROW_TILE, d), lambda i: (i, 0)), pl.BlockSpec((1, d), lambda i: (0, 0))],
        out_specs=pl.BlockSpec((ROW_TILE, d), lambda i: (i, 0)),
        out_shape=jax.ShapeDtypeStruct((t, d), F32),
        compiler_params=_cparams("parallel"),
        name="final_norm",
    )(h, g.reshape(1, d))


def _even_w_in(w):
    d = w.shape[0]
    n_main = 2 * GLA_QK + 2 * GLA_V
    ranks = w[:, n_main:n_main + 2 * GLA_RANK]
    hy = w[:, n_main + 2 * GLA_RANK:]
    pad = jnp.zeros((d, R_COLS - 2 * GLA_RANK), w.dtype)
    return jnp.concatenate([w[:, :n_main], hy, ranks, pad], axis=1).astype(BF16)


def _rank_proj(wa, first_row):
    return jnp.zeros((R_COLS, GLA_QK), F32).at[first_row:first_row + GLA_RANK].set(wa)


def _grid_transpose(h_lat, batch, a, b):
    d = h_lat.shape[1]
    return h_lat.reshape(batch, a, b, d).transpose(0, 2, 1, 3).reshape(-1, d)


def kernel(x, c, ctx, c_ctx, mod_w, mod_b, norm_mix, norm_ffn, norm_final, ev_w_in, ev_w_out, gla_wa_f, gla_ba_f, gla_wa_b, gla_ba_b, gla_norm, hy_conv_w, hy_conv_b, hy_w1, hy_b1, hy_f1, hy_w2, hy_b2, hy_f2, hy_w3, hy_skip, od_w_in, od_conv_w, od_w_out, rt_w_grp, rt_b_grp, rt_w_exp, rt_b_exp, ex_w1, ex_w3, ex_w2):
    batch, s, d = x.shape
    lc = ctx.shape[1]
    depth = mod_w.shape[0]
    tc, tl = batch * lc, batch * s
    assert lc % ROW_TILE == 0 and s % ROW_TILE == 0 and tc % s == 0 and s % GRID_W == 0
    assert lc % GLA_CHUNK == 0 and s % GLA_CHUNK == 0
    assert depth % 2 == 0
    grid_rows = s // GRID_W

    n_cond = -(-(batch + 1) // 8) * 8
    cond = jnp.concatenate([c, c_ctx[None], jnp.zeros((n_cond - batch - 1, d), F32)], axis=0)
    mods = _mods(cond, mod_w, mod_b).reshape(depth, n_cond, 6, 1, d)

    h = jnp.concatenate([ctx.reshape(tc, d), x.reshape(tl, d)], axis=0)
    col_major_now = False

    for l in range(depth):
        i = l // 2
        even = l % 2 == 0
        ctx_out = l < depth - 1
        col_major = i % 2 == 1
        if not ctx_out:
            h = h[tc:]
        lat_only = h.shape[0] == tl
        if col_major != col_major_now:
            lat = h[-tl:]
            lat = (_grid_transpose(lat, batch, grid_rows, GRID_W) if col_major
                   else _grid_transpose(lat, batch, GRID_W, grid_rows))
            h = lat if lat_only else jnp.concatenate([h[:tc], lat], axis=0)
            col_major_now = col_major
        arr_rows = _Rows(batch, lc, s, lat_only)
        mods_l = mods[l]

        if even:
            p = _normmod_mm(h, norm_mix[l], mods_l, _even_w_in(ev_w_in[i]), arr_rows, 0, 1, "even_in")
            waf = _rank_proj(gla_wa_f[i], 0)
            wab = _rank_proj(gla_wa_b[i], GLA_RANK)
            baf, bab = gla_ba_f[i].reshape(1, -1), gla_ba_b[i].reshape(1, -1)
            gain = gla_norm[i].reshape(1, -1)
            zeros = jnp.zeros((batch, 2, 2 * GLA_DV, 2 * GLA_DK), F32)
            filt_args = (hy_w1[i], hy_b1[i], hy_f1[i], hy_w2[i], hy_b2[i], hy_f2[i], hy_w3[i])
            x0c, z = _hyena_pre(p, hy_conv_w[i], hy_conv_b[i], arr_rows)
            gla_c, sc_f, sc_b = _gla(p, waf, wab, baf, bab, gain, zeros, zeros, batch, lc, 0)
            gla_l, _, _ = _gla(p, waf, wab, baf, bab, gain, sc_f, sc_b, batch, s, tc // s)
            kt_c = _hyena_ktab(_hyena_filters(lc, *filt_args), lc)
            kt_l = _hyena_ktab(_hyena_filters(s, *filt_args), s)
            hy_c = _hyena_conv(z, x0c, kt_c, hy_skip[i], batch, lc, 0)
            hy_l = _hyena_conv(z, x0c, kt_l, hy_skip[i], batch, s, tc // s)
            h = _mix_out(gla_c, gla_l, hy_c, hy_l, ev_w_out[i].astype(BF16), h, mods_l, arr_rows)
        else:
            p = _normmod_mm(h, norm_mix[l], mods_l, od_w_in[i].astype(BF16), arr_rows, 0, 1, "odd_in")
            h = _shortconv_out(p, od_conv_w[i], od_w_out[i].astype(BF16), h, mods_l, arr_rows)

        w_rt = jnp.concatenate([rt_w_grp[l], rt_w_exp[l],
                                jnp.zeros((d, LANE - N_GROUPS - N_EXPERTS), F32)], axis=1)
        b_rt = jnp.concatenate([rt_b_grp[l], rt_b_exp[l],
                                jnp.zeros((LANE - N_GROUPS - N_EXPERTS,), F32)]).reshape(1, LANE)
        v, logits = _router(h, norm_ffn[l], mods_l, w_rt, b_rt, arr_rows)
        expert, gate = _route(logits)
        y2 = _experts(v, _dispatch_plan(expert), ex_w1[l].astype(BF16), ex_w3[l].astype(BF16),
                      ex_w2[l].astype(BF16))
        h = _combine(h, y2, gate, mods_l, arr_rows)

    lat = h[-tl:]
    out = _final_norm(lat, norm_final)
    if col_major_now:
        out = _grid_transpose(out, batch, GRID_W, grid_rows)
    return out.reshape(batch, s, d)
```

```python
import functools
import math

import numpy as np
import jax
import jax.numpy as jnp
from jax import lax
from jax.experimental import pallas as pl
from jax.experimental.pallas import tpu as pltpu

F32 = jnp.float32
BF16 = jnp.bfloat16

EPS = 1e-6
GRID_W = 64

GLA_HEADS = 4
GLA_DK = 64
GLA_DV = 128
GLA_RANK = 16
GLA_TAU = 16.0
GLA_CHUNK = 64
GLA_SLAB = 256
GLA_STEP_UNROLL = 4
GLA_QK = GLA_HEADS * GLA_DK
GLA_V = GLA_HEADS * GLA_DV

HY_WIDTH = 512
HY_EMB = 33
HY_BANDS = (HY_EMB - 1) // 2
HY_HIDDEN = 64
HY_FAST_DECAY = 0.3
HY_SLOW_DECAY = 1.5
HY_TARGET = 1e-2

N_GROUPS = 4
EXP_PER_GROUP = 8
N_EXPERTS = N_GROUPS * EXP_PER_GROUP
TOP_K = 2

LANE = 128
ROW_TILE = 256
MOE_ROWS = 256
R_COLS = LANE
VMEM_LIMIT = 56 * 1024 * 1024


def _cparams(*sem):
    return pltpu.CompilerParams(dimension_semantics=sem, vmem_limit_bytes=VMEM_LIMIT)


def _split_bf16(a):
    hi = a.astype(BF16)
    lo = (a - hi.astype(F32)).astype(BF16)
    return hi, lo


def _dot(a, b):
    return jnp.dot(a, b, preferred_element_type=F32)


def _dot_nt(a, b):
    return lax.dot_general(a, b, (((1,), (1,)), ((), ())), preferred_element_type=F32)


def _dot_tn(a, b):
    return lax.dot_general(a, b, (((0,), (0,)), ((), ())), preferred_element_type=F32)


def _dot3(a, b):
    ah, al = _split_bf16(a)
    bh, bl = _split_bf16(b)
    return _dot(ah, bh) + _dot(ah, bl) + _dot(al, bh)


def _silu(x):
    return x / (1.0 + jnp.exp(-x))


def _log_sigmoid(x):
    return jnp.minimum(x, 0.0) - jnp.log1p(jnp.exp(-jnp.abs(x)))


def _norm_mod(x, g, shift, scale):
    y = x * lax.rsqrt(jnp.mean(x * x, axis=-1, keepdims=True) + EPS)
    return (y * g) * (1.0 + scale) + shift


def _mods_kernel(s_ref, w_ref, b_ref, o_ref):
    s = s_ref[...]
    s = _silu(s)
    o_ref[...] = _dot3(s, w_ref[...]) + b_ref[...]


def _mods(cond, mod_w, mod_b):
    depth, d, n = mod_w.shape
    r = cond.shape[0]
    tn = 1024
    return pl.pallas_call(
        _mods_kernel,
        grid=(depth, n // tn),
        in_specs=[pl.BlockSpec((r, d), lambda l, j: (0, 0)),
                  pl.BlockSpec((None, d, tn), lambda l, j: (l, 0, j)),
                  pl.BlockSpec((None, 1, tn), lambda l, j: (l, 0, j))],
        out_specs=pl.BlockSpec((None, r, tn), lambda l, j: (l, 0, j)),
        out_shape=jax.ShapeDtypeStruct((depth, r, n), F32),
        compiler_params=_cparams("parallel", "parallel"),
        name="mods",
    )(cond, mod_w, mod_b.reshape(depth, 1, n))


class _Rows:
    def __init__(self, batch, lc, s, lat_only):
        self.batch = batch
        self.nct = 0 if lat_only else batch * lc // ROW_TILE
        self.tps = s // ROW_TILE
        self.tpc = lc // ROW_TILE
        self.n_tiles = self.nct + batch * self.tps

    def mod_row(self, i):
        return jnp.where(i < self.nct, self.batch, (i - self.nct) // self.tps)

    def seq_edges(self, i):
        pos_c = i % self.tpc
        pos_l = (i - self.nct) % self.tps
        is_c = i < self.nct
        first = jnp.where(is_c, pos_c == 0, pos_l == 0)
        last = jnp.where(is_c, pos_c == self.tpc - 1, pos_l == self.tps - 1)
        return first, last


def _normmod_mm_kernel(h_ref, g_ref, mod_ref, w_ref, o_ref, *, shift_i, scale_i):
    u = _norm_mod(h_ref[...], g_ref[...], mod_ref[shift_i], mod_ref[scale_i])
    o_ref[...] = _dot(u.astype(BF16), w_ref[...])


def _normmod_mm(h, g, mods_l, w, rows, shift_i, scale_i, name):
    d = h.shape[1]
    n = w.shape[1]
    return pl.pallas_call(
        functools.partial(_normmod_mm_kernel, shift_i=shift_i, scale_i=scale_i),
        grid=(rows.n_tiles,),
        in_specs=[pl.BlockSpec((ROW_TILE, d), lambda i: (i, 0)),
                  pl.BlockSpec((1, d), lambda i: (0, 0)),
                  pl.BlockSpec((None, 6, 1, d), lambda i: (rows.mod_row(i), 0, 0, 0)),
                  pl.BlockSpec((d, n), lambda i: (0, 0))],
        out_specs=pl.BlockSpec((ROW_TILE, n), lambda i: (i, 0)),
        out_shape=jax.ShapeDtypeStruct((rows.n_tiles * ROW_TILE, n), F32),
        compiler_params=_cparams("parallel"),
        name=name,
    )(h, g.reshape(1, d), mods_l, w)


def _gla_kernel(q_ref, k_ref, v_ref, g_ref, r_ref, waf_ref, wab_ref, baf_ref, bab_ref, gain_ref,
                s0f_ref, s0b_ref, o_ref, sf_ref, sb_ref, gl_scr, tot_scr, qd_scr, ds_scr, o_scr, st_scr,
                *, seq_len):
    C, SL = GLA_CHUNK, GLA_SLAB
    cps = SL // C
    n_chunks, n_slabs = seq_len // C, seq_len // SL
    head_of_lane = lax.broadcasted_iota(jnp.int32, (SL, 2 * GLA_DK), 1) // GLA_DK
    row = lax.broadcasted_iota(jnp.int32, (SL, SL), 0)
    col = lax.broadcasted_iota(jnp.int32, (SL, SL), 1)
    same_chunk = (row // C) == (col // C)
    srow = lax.broadcasted_iota(jnp.int32, (2 * GLA_DV, 2 * GLA_DK), 0) // GLA_DV
    scol = lax.broadcasted_iota(jnp.int32, (2 * GLA_DV, 2 * GLA_DK), 1) // GLA_DK
    same_head = srow == scol
    r = r_ref[...]

    def direction(wa_ref, ba_ref, s0_ref, s_out_ref, forward):
        mask = jnp.logical_and(same_chunk, (row >= col) if forward else (row <= col))
        tri = jnp.where(mask, 1.0, 0.0).astype(BF16)
        gl_scr[...] = _log_sigmoid(_dot3(r, wa_ref[...]) + ba_ref[...]) * (1.0 / GLA_TAU)

        def slab(s, carry):
            rows = pl.ds(pl.multiple_of(s * SL, SL), SL)
            gl = gl_scr[rows, :]
            g_hi = gl.astype(BF16)
            g_r1 = gl - g_hi.astype(F32)
            g_mid = g_r1.astype(BF16)
            g_lo = (g_r1 - g_mid.astype(F32)).astype(BF16)
            b = _dot(tri, g_hi) + _dot(tri, g_mid) + _dot(tri, g_lo)
            b3 = b.reshape(cps, C, 2 * GLA_DK)
            last = b3[:, C - 1:C, :] if forward else b3[:, 0:1, :]
            tot = jnp.broadcast_to(last, b3.shape).reshape(SL, 2 * GLA_DK)
            tot_scr[rows, :] = tot
            q = q_ref[rows, :] * (GLA_DK ** -0.5)
            k = k_ref[rows, :]
            v = v_ref[rows, :].astype(BF16)
            qd = q * jnp.exp(b)
            kd = (k * jnp.exp(-b)).astype(BF16)
            kr = (k * jnp.exp(tot - b)).astype(BF16)
            qd_scr[rows, :] = qd.astype(BF16)
            o_parts = []
            for h in range(2):
                qh = jnp.where(head_of_lane == h, qd, 0.0).astype(BF16)
                a = jnp.where(mask, _dot_nt(qh, kd), 0.0).astype(BF16)
                o_parts.append(_dot(a, v[:, h * GLA_DV:(h + 1) * GLA_DV]))
            o = jnp.concatenate(o_parts, axis=1)
            if forward:
                o_scr[rows, :] = o
            else:
                o_scr[rows, :] += o
            for c in range(cps):
                ds = _dot_tn(v[c * C:(c + 1) * C], kr[c * C:(c + 1) * C])
                ds_scr[s * cps + c] = jnp.where(same_head, ds, 0.0)
            return carry

        lax.fori_loop(0, n_slabs, slab, 0)
        st_scr[...] = s0_ref[...]

        def step(i, carry):
            c = i if forward else n_chunks - 1 - i
            first = pl.multiple_of(c * C, C)
            rows = pl.ds(first, C)
            st = st_scr[...]
            o_scr[rows, :] += _dot_nt(qd_scr[rows, :], st.astype(BF16))
            st_scr[...] = st * jnp.exp(tot_scr[pl.ds(first, 1), :]) + ds_scr[c]
            return carry

        lax.fori_loop(0, n_chunks, step, 0, unroll=GLA_STEP_UNROLL)
        s_out_ref[...] = st_scr[...]

    direction(waf_ref, baf_ref, s0f_ref, sf_ref, True)
    direction(wab_ref, bab_ref, s0b_ref, sb_ref, False)
    gain = gain_ref[...]

    def readout(s, carry):
        rows = pl.ds(pl.multiple_of(s * SL, SL), SL)
        o = o_scr[rows, :]
        outs = []
        for h in range(2):
            oh = o[:, h * GLA_DV:(h + 1) * GLA_DV]
            outs.append(oh * lax.rsqrt(jnp.mean(oh * oh, axis=-1, keepdims=True) + EPS) * gain)
        o_ref[rows, :] = jnp.concatenate(outs, axis=1) * _silu(g_ref[rows, :])
        return carry

    lax.fori_loop(0, n_slabs, readout, 0)


def _gla(p, waf, wab, baf, bab, gain, s0f, s0b, batch, seq_len, blk0):
    dk2, dv2 = 2 * GLA_DK, 2 * GLA_DV
    seq = lambda width, cb: pl.BlockSpec((seq_len, width), lambda b, hp: (blk0 + b, cb(hp)))
    state_spec = pl.BlockSpec((None, None, dv2, dk2), lambda b, hp: (b, hp, 0, 0))
    in_specs = [seq(dk2, lambda hp: hp),
                seq(dk2, lambda hp: GLA_QK // dk2 + hp),
                seq(dv2, lambda hp: 2 * GLA_QK // dv2 + hp),
                seq(dv2, lambda hp: (2 * GLA_QK + GLA_V) // dv2 + hp),
                seq(R_COLS, lambda hp: (2 * GLA_QK + 2 * GLA_V + 3 * HY_WIDTH) // R_COLS),
                pl.BlockSpec((R_COLS, dk2), lambda b, hp: (0, hp)),
                pl.BlockSpec((R_COLS, dk2), lambda b, hp: (0, hp)),
                pl.BlockSpec((1, dk2), lambda b, hp: (0, hp)),
                pl.BlockSpec((1, dk2), lambda b, hp: (0, hp)),
                pl.BlockSpec((1, GLA_DV), lambda b, hp: (0, 0)),
                state_spec, state_spec]
    state_shape = jax.ShapeDtypeStruct((batch, 2, dv2, dk2), F32)
    return pl.pallas_call(
        functools.partial(_gla_kernel, seq_len=seq_len),
        grid=(batch, 2),
        in_specs=in_specs,
        out_specs=[pl.BlockSpec((seq_len, dv2), lambda b, hp: (b, hp)), state_spec, state_spec],
        out_shape=[jax.ShapeDtypeStruct((batch * seq_len, GLA_V), F32), state_shape, state_shape],
        scratch_shapes=[pltpu.VMEM((seq_len, dk2), F32),
                        pltpu.VMEM((seq_len, dk2), F32),
                        pltpu.VMEM((seq_len, dk2), BF16),
                        pltpu.VMEM((seq_len // GLA_CHUNK, dv2, dk2), F32),
                        pltpu.VMEM((seq_len, dv2), F32),
                        pltpu.VMEM((dv2, dk2), F32)],
        compiler_params=_cparams("parallel", "parallel"),
        name="gla",
    )(p, p, p, p, p, waf, wab, baf, bab, gain, s0f, s0b)


def _conv3(m, prev_row, next_row, w_ref):
    n = m.shape[0]
    ridx = lax.broadcasted_iota(jnp.int32, m.shape, 0)
    m_prev = jnp.where(ridx == 0, prev_row, pltpu.roll(m, 1, 0))
    m_next = jnp.where(ridx == n - 1, next_row, pltpu.roll(m, n - 1, 0))
    return w_ref[0:1, :] * m_prev + w_ref[1:2, :] * m + w_ref[2:3, :] * m_next


def _halo_specs(width, col_block, t_rows):
    g = ROW_TILE // 8
    last = t_rows // 8 - 1
    prev = pl.BlockSpec((8, width), lambda i: (jnp.maximum(i * g - 1, 0), col_block))
    nxt = pl.BlockSpec((8, width), lambda i: (jnp.minimum((i + 1) * g, last), col_block))
    return prev, nxt


def _hyena_pre_kernel(x0_ref, x1_ref, v_ref, x0p_ref, x0n_ref, x1p_ref, x1n_ref, vp_ref, vn_ref,
                      w_ref, b_ref, x0c_ref, z_ref, *, rows):
    first, last = rows.seq_edges(pl.program_id(0))
    keep_p = jnp.where(first, 0.0, 1.0)
    keep_n = jnp.where(last, 0.0, 1.0)
    hw = HY_WIDTH

    def conv(ref, p_ref, n_ref, j):
        w = w_ref.at[:, j * hw:(j + 1) * hw]
        y = _conv3(ref[...], p_ref[7:8, :] * keep_p, n_ref[0:1, :] * keep_n, w)
        return y + b_ref[:, j * hw:(j + 1) * hw]

    x0c_ref[...] = conv(x0_ref, x0p_ref, x0n_ref, 0)
    z_ref[...] = conv(v_ref, vp_ref, vn_ref, 2) * conv(x1_ref, x1p_ref, x1n_ref, 1)


def _hyena_pre(p, conv_w, conv_b, rows):
    t_rows = p.shape[0]
    hw = HY_WIDTH
    cb0 = (2 * GLA_QK + 2 * GLA_V) // hw
    in_specs = [pl.BlockSpec((ROW_TILE, hw), lambda i, j=j: (i, cb0 + j)) for j in range(3)]
    for j in range(3):
        in_specs.extend(_halo_specs(hw, cb0 + j, t_rows))
    in_specs += [pl.BlockSpec((3, 3 * hw), lambda i: (0, 0)), pl.BlockSpec((1, 3 * hw), lambda i: (0, 0))]
    out_spec = pl.BlockSpec((ROW_TILE, hw), lambda i: (i, 0))
    shape = jax.ShapeDtypeStruct((rows.n_tiles * ROW_TILE, hw), F32)
    return pl.pallas_call(
        functools.partial(_hyena_pre_kernel, rows=rows),
        grid=(rows.n_tiles,),
        in_specs=in_specs,
        out_specs=[out_spec, out_spec],
        out_shape=[shape, shape],
        compiler_params=_cparams("parallel"),
        name="hyena_pre",
    )(p, p, p, p, p, p, p, p, p, conv_w, conv_b.reshape(1, 3 * hw))


def _filter_kernel(z_ref, w1_ref, b1_ref, f1_ref, w2_ref, b2_ref, f2_ref, w3_ref, win_ref, o_ref):
    hh = jnp.sin(f1_ref[...] * (_dot3(z_ref[...], w1_ref[...]) + b1_ref[...]))
    hh = jnp.sin(f2_ref[...] * (_dot3(hh, w2_ref[...]) + b2_ref[...]))
    win = win_ref[...]
    o_ref[...] = _dot3(hh, w3_ref[...]) * jnp.concatenate([win, win], axis=1)


@functools.lru_cache(maxsize=None)
def _filter_features(L):
    t = np.linspace(0.0, 1.0, L, dtype=np.float32)[:, None]
    pos = np.arange(L, dtype=np.float32)[:, None]
    bands = np.linspace(1e-4, HY_BANDS - 1, HY_BANDS, dtype=np.float32)[None]
    ang = (np.float32(2.0 * math.pi / L) * pos * bands).astype(np.float32)
    z = np.concatenate([t, np.cos(ang), np.sin(ang)], axis=-1).astype(np.float32)
    z = np.pad(z, ((0, 0), (0, LANE - HY_EMB)))
    max_decay = math.log(HY_TARGET) / HY_FAST_DECAY
    min_decay = math.log(HY_TARGET) / HY_SLOW_DECAY
    deltas = np.linspace(min_decay, max_decay, HY_WIDTH, dtype=np.float32)
    window = np.exp(-t * np.abs(deltas)[None]).astype(np.float32)
    return z, window


def _hyena_filters(L, w1, b1, f1, w2, b2, f2, w3):
    z, window = _filter_features(L)
    w1p = jnp.pad(w1, ((0, LANE - HY_EMB), (0, 0)))
    tl = min(L, 512)
    full = lambda a: pl.BlockSpec(a.shape, lambda i: (0,) * a.ndim)
    row = lambda a: a.reshape(1, -1)
    ops = [w1p, row(b1), row(f1), w2, row(b2), row(f2), w3]
    return pl.pallas_call(
        _filter_kernel,
        grid=(L // tl,),
        in_specs=[pl.BlockSpec((tl, LANE), lambda i: (i, 0))] + [full(a) for a in ops]
                 + [pl.BlockSpec((tl, HY_WIDTH), lambda i: (i, 0))],
        out_specs=pl.BlockSpec((tl, 2 * HY_WIDTH), lambda i: (i, 0)),
        out_shape=jax.ShapeDtypeStruct((L, 2 * HY_WIDTH), F32),
        compiler_params=_cparams("parallel"),
        name="hyena_filters",
    )(jnp.asarray(z), *ops, jnp.asarray(window))


def _freq_tile(L):
    return min(2 * L, 512)


@functools.lru_cache(maxsize=None)
def _dft_tables(L):
    n = 2 * L
    tf = _freq_tile(L)
    half = tf // 2
    t = np.arange(L, dtype=np.int64)[None, :]
    fm = np.zeros((n, L), np.float64)
    scale = np.zeros((n, 1), np.float64)
    sign = np.zeros((n, 1), np.float64)
    for j in range(n // tf):
        k = (np.arange(half, dtype=np.int64) + j * half)[:, None]
        ang = 2.0 * np.pi * ((k * t) % n).astype(np.float64) / n
        fm[j * tf:j * tf + half] = np.cos(ang)
        fm[j * tf + half:(j + 1) * tf] = -np.sin(ang)
        scale[j * tf:(j + 1) * tf] = 2.0 / n
        sign[j * tf:j * tf + half] = 1.0
        sign[j * tf + half:(j + 1) * tf] = -1.0
    fm[half] = np.cos(np.pi * t[0])
    scale[0] = 1.0 / n
    scale[half] = 1.0 / n
    sign[half] = 1.0
    return (fm.astype(np.float32), np.ascontiguousarray(fm.T).astype(np.float32),
            scale.astype(np.float32), sign.astype(np.float32))


def _ktab_kernel(f_ref, h_ref, scale_ref, sign_ref, o_ref):
    hw = HY_WIDTH
    hh, hl = _split_bf16(h_ref[...])
    f = f_ref[...]
    kk = _dot(f, hh) + _dot(f, hl)
    o_ref[...] = scale_ref[...] * (kk[:, :hw] + sign_ref[...] * kk[:, hw:])


def _hyena_ktab(filt, L):
    fm, _, scale, sign = _dft_tables(L)
    n = 2 * L
    tf = _freq_tile(L)
    return pl.pallas_call(
        _ktab_kernel,
        grid=(n // tf,),
        in_specs=[pl.BlockSpec((tf, L), lambda j: (j, 0)),
                  pl.BlockSpec((L, 2 * HY_WIDTH), lambda j: (0, 0)),
                  pl.BlockSpec((tf, 1), lambda j: (j, 0)),
                  pl.BlockSpec((tf, 1), lambda j: (j, 0))],
        out_specs=pl.BlockSpec((tf, HY_WIDTH), lambda j: (j, 0)),
        out_shape=jax.ShapeDtypeStruct((n, HY_WIDTH), F32),
        compiler_params=_cparams("parallel"),
        name="hyena_ktab",
    )(jnp.asarray(fm, dtype=BF16), filt, jnp.asarray(scale), jnp.asarray(sign))


def _hyena_conv_kernel(z_ref, x0_ref, f_ref, ft_ref, k_ref, skip_ref, o_ref, zb_scr, acc_scr, *, tf):
    j = pl.program_id(1)
    half = tf // 2

    @pl.when(j == 0)
    def _():
        zb_scr[...] = z_ref[...].astype(BF16)
        acc_scr[...] = jnp.zeros_like(acc_scr)

    zf = _dot(f_ref[...], zb_scr[...])
    re, im = zf[:half], zf[half:]
    kre, kim = k_ref[:half, :], k_ref[half:, :]
    ridx = lax.broadcasted_iota(jnp.int32, re.shape, 0)
    mix = jnp.where(jnp.logical_and(j == 0, ridx == 0), 0.0, 1.0)
    yre = re * kre - mix * (im * kim)
    yim = mix * (re * kim) + im * jnp.where(mix == 0.0, kim, kre)
    y = jnp.concatenate([yre, yim], axis=0).astype(BF16)
    acc_scr[...] += _dot(ft_ref[...], y)

    @pl.when(j == pl.num_programs(1) - 1)
    def _():
        o_ref[...] = x0_ref[...] * (acc_scr[...] + z_ref[...] * skip_ref[...])


def _hyena_conv(z, x0c, ktab, skip, batch, L, blk0):
    fm, fmt, _, _ = _dft_tables(L)
    n = 2 * L
    tf = _freq_tile(L)
    hw = HY_WIDTH
    return pl.pallas_call(
        functools.partial(_hyena_conv_kernel, tf=tf),
        grid=(batch, n // tf),
        in_specs=[pl.BlockSpec((L, hw), lambda b, j: (blk0 + b, 0)),
                  pl.BlockSpec((L, hw), lambda b, j: (blk0 + b, 0)),
                  pl.BlockSpec((tf, L), lambda b, j: (j, 0)),
                  pl.BlockSpec((L, tf), lambda b, j: (0, j)),
                  pl.BlockSpec((tf, hw), lambda b, j: (j, 0)),
                  pl.BlockSpec((1, hw), lambda b, j: (0, 0))],
        out_specs=pl.BlockSpec((L, hw), lambda b, j: (b, 0)),
        out_shape=jax.ShapeDtypeStruct((batch * L, hw), F32),
        scratch_shapes=[pltpu.VMEM((L, hw), BF16), pltpu.VMEM((L, hw), F32)],
        compiler_params=_cparams("parallel", "arbitrary"),
        name="hyena_conv",
    )(z, x0c, jnp.asarray(fm, dtype=BF16), jnp.asarray(fmt, dtype=BF16), ktab, skip.reshape(1, hw))


def _mix_out_kernel(ac_ref, al_ref, bc_ref, bl_ref, wa_ref, wb_ref, h_ref, mod_ref, o_ref, *, nct):
    is_ctx = pl.program_id(0) < nct
    a = jnp.where(is_ctx, ac_ref[...], al_ref[...])
    b = jnp.where(is_ctx, bc_ref[...], bl_ref[...])
    y = _dot(a.astype(BF16), wa_ref[...]) + _dot(b.astype(BF16), wb_ref[...])
    o_ref[...] = h_ref[...] + mod_ref[2] * y


def _mix_out(a_ctx, a_lat, b_ctx, b_lat, w_out, h, mods_l, rows):
    d = h.shape[1]
    ka, kb = a_ctx.shape[1], b_ctx.shape[1]
    nct = rows.nct
    ctx_spec = lambda k: pl.BlockSpec((ROW_TILE, k), lambda i: (jnp.minimum(i, nct - 1), 0))
    lat_spec = lambda k: pl.BlockSpec((ROW_TILE, k), lambda i: (jnp.maximum(i - nct, 0), 0))
    return pl.pallas_call(
        functools.partial(_mix_out_kernel, nct=nct),
        grid=(rows.n_tiles,),
        in_specs=[ctx_spec(ka), lat_spec(ka), ctx_spec(kb), lat_spec(kb),
                  pl.BlockSpec((ka, d), lambda i: (0, 0)),
                  pl.BlockSpec((kb, d), lambda i: (ka // kb, 0)),
                  pl.BlockSpec((ROW_TILE, d), lambda i: (i, 0)),
                  pl.BlockSpec((None, 6, 1, d), lambda i: (rows.mod_row(i), 0, 0, 0))],
        out_specs=pl.BlockSpec((ROW_TILE, d), lambda i: (i, 0)),
        out_shape=jax.ShapeDtypeStruct((rows.n_tiles * ROW_TILE, d), F32),
        compiler_params=_cparams("parallel"),
        name="mix_out",
    )(a_ctx, a_lat, b_ctx, b_lat, w_out, w_out, h, mods_l)


def _shortconv_out_kernel(bg_ref, cg_ref, xi_ref, cgp_ref, cgn_ref, xip_ref, xin_ref, cw_ref, w_ref,
                          h_ref, mod_ref, o_ref, *, rows):
    first, last = rows.seq_edges(pl.program_id(0))
    keep_p = jnp.where(first, 0.0, 1.0)
    keep_n = jnp.where(last, 0.0, 1.0)
    m = cg_ref[...] * xi_ref[...]
    m_prev = cgp_ref[7:8, :] * xip_ref[7:8, :] * keep_p
    m_next = cgn_ref[0:1, :] * xin_ref[0:1, :] * keep_n
    y = bg_ref[...] * _conv3(m, m_prev, m_next, cw_ref)
    o_ref[...] = h_ref[...] + mod_ref[2] * _dot(y.astype(BF16), w_ref[...])


def _shortconv_out(p, conv_w, w_out, h, mods_l, rows):
    d = h.shape[1]
    t_rows = p.shape[0]
    in_specs = [pl.BlockSpec((ROW_TILE, d), lambda i, j=j: (i, j)) for j in range(3)]
    in_specs += [*_halo_specs(d, 1, t_rows), *_halo_specs(d, 2, t_rows),
                 pl.BlockSpec((3, d), lambda i: (0, 0)),
                 pl.BlockSpec((d, d), lambda i: (0, 0)),
                 pl.BlockSpec((ROW_TILE, d), lambda i: (i, 0)),
                 pl.BlockSpec((None, 6, 1, d), lambda i: (rows.mod_row(i), 0, 0, 0))]
    return pl.pallas_call(
        functools.partial(_shortconv_out_kernel, rows=rows),
        grid=(rows.n_tiles,),
        in_specs=in_specs,
        out_specs=pl.BlockSpec((ROW_TILE, d), lambda i: (i, 0)),
        out_shape=jax.ShapeDtypeStruct((rows.n_tiles * ROW_TILE, d), F32),
        compiler_params=_cparams("parallel"),
        name="shortconv_out",
    )(p, p, p, p, p, p, p, conv_w, w_out, h, mods_l)


META_E, META_RANK, META_GATE = 0, 2, 4


def _lane_min_index(mask, lane_f):
    return jnp.min(jnp.where(mask, lane_f, float(LANE)), axis=1, keepdims=True)


def _router_kernel(h_ref, g_ref, mod_ref, w_ref, b_ref, v_ref, meta_ref, cnt_ref, carry_scr):
    @pl.when(pl.program_id(0) == 0)
    def _():
        carry_scr[...] = jnp.zeros_like(carry_scr)

    v = _norm_mod(h_ref[...], g_ref[...], mod_ref[3], mod_ref[4])
    v_ref[...] = v
    lg = _dot3(v, w_ref[...]) + b_ref[...]
    tm = lg.shape[0]
    lane = lax.broadcasted_iota(jnp.int32, lg.shape, 1)
    lane_f = lane.astype(F32)
    neg = -jnp.inf

    is_grp = lane < N_GROUPS
    lgm = jnp.where(is_grp, lg, neg)
    m_g = jnp.max(lgm, axis=1, keepdims=True)
    s_g = jnp.sum(jnp.where(is_grp, jnp.exp(lg - m_g), 0.0), axis=1, keepdims=True)
    p_g = 1.0 / s_g
    grp = _lane_min_index(lgm == m_g, lane_f)

    ex_lane = lane - N_GROUPS
    in_grp = jnp.logical_and(jnp.logical_and(ex_lane >= 0, ex_lane < N_EXPERTS),
                             (ex_lane // EXP_PER_GROUP).astype(F32) == grp)
    m_e = jnp.max(jnp.where(in_grp, lg, neg), axis=1, keepdims=True)
    ee = jnp.where(in_grp, jnp.exp(lg - m_e), 0.0)
    pe = ee / jnp.sum(ee, axis=1, keepdims=True)
    pe1 = jnp.where(in_grp, pe, -1.0)
    p1 = jnp.max(pe1, axis=1, keepdims=True)
    i1 = _lane_min_index(pe1 == p1, lane_f)
    pe2 = jnp.where(lane_f == i1, -1.0, pe1)
    p2 = jnp.max(pe2, axis=1, keepdims=True)
    i2 = _lane_min_index(pe2 == p2, lane_f)
    denom = p1 + p2
    g1 = p_g * p1 / denom
    g2 = p_g * p2 / denom
    e1 = i1 - float(N_GROUPS)
    e2 = i2 - float(N_GROUPS)

    oh1 = lane_f == e1
    oh2 = lane_f == e2
    row = lax.broadcasted_iota(jnp.int32, (tm, tm), 0)
    col = lax.broadcasted_iota(jnp.int32, (tm, tm), 1)
    earlier = jnp.where(row > col, 1.0, 0.0).astype(BF16)
    c1 = _dot(earlier, jnp.where(oh1, 1.0, 0.0).astype(BF16))
    c2 = _dot(earlier, jnp.where(oh2, 1.0, 0.0).astype(BF16))
    tot1 = jnp.sum(jnp.where(oh1, 1.0, 0.0), axis=0, keepdims=True)
    tot2 = jnp.sum(jnp.where(oh2, 1.0, 0.0), axis=0, keepdims=True)
    carry = carry_scr[...]
    r1 = jnp.sum(jnp.where(oh1, carry + c1, 0.0), axis=1, keepdims=True)
    r2 = jnp.sum(jnp.where(oh2, carry + tot1 + c2, 0.0), axis=1, keepdims=True)
    carry = carry + tot1 + tot2
    carry_scr[...] = carry
    cnt_ref[...] = carry

    meta = jnp.zeros_like(lg)
    for idx, val in ((META_E, e1), (META_E + 1, e2), (META_RANK, r1), (META_RANK + 1, r2),
                     (META_GATE, g1), (META_GATE + 1, g2)):
        meta = jnp.where(lane == idx, val, meta)
    meta_ref[...] = meta


def _router(h, g, mods_l, w_rt, b_rt, rows):
    d = h.shape[1]
    nt = rows.n_tiles * ROW_TILE
    return pl.pallas_call(
        _router_kernel,
        grid=(rows.n_tiles,),
        in_specs=[pl.BlockSpec((ROW_TILE, d), lambda i: (i, 0)),
                  pl.BlockSpec((1, d), lambda i: (0, 0)),
                  pl.BlockSpec((None, 6, 1, d), lambda i: (rows.mod_row(i), 0, 0, 0)),
                  pl.BlockSpec((d, LANE), lambda i: (0, 0)),
                  pl.BlockSpec((1, LANE), lambda i: (0, 0))],
        out_specs=[pl.BlockSpec((ROW_TILE, d), lambda i: (i, 0)),
                   pl.BlockSpec((ROW_TILE, LANE), lambda i: (i, 0)),
                   pl.BlockSpec((1, LANE), lambda i: (0, 0))],
        out_shape=[jax.ShapeDtypeStruct((nt, d), F32), jax.ShapeDtypeStruct((nt, LANE), F32),
                   jax.ShapeDtypeStruct((1, LANE), F32)],
        scratch_shapes=[pltpu.VMEM((1, LANE), F32)],
        compiler_params=_cparams("arbitrary"),
        name="router",
    )(h, g.reshape(1, d), mods_l, w_rt, b_rt)


def _dispatch_plan(meta, counts, n_assign):
    bm = MOE_ROWS
    n_blocks = -(-(n_assign + N_EXPERTS * (bm - 1)) // bm)
    counts = counts[0, :N_EXPERTS].astype(jnp.int32)
    nblk = (counts + bm - 1) // bm
    blk_end = jnp.cumsum(nblk).astype(jnp.int32)
    slot0 = (blk_end - nblk) * bm
    expert = meta[:, META_E:META_E + TOP_K].astype(jnp.int32)
    rank = meta[:, META_RANK:META_RANK + TOP_K].astype(jnp.int32)
    onehot = expert[:, :, None] == jnp.arange(N_EXPERTS, dtype=jnp.int32)
    slot = rank + jnp.sum(jnp.where(onehot, slot0, 0), axis=-1)
    n_tiles = slot.shape[0] // ROW_TILE
    slot_tiles = slot.reshape(n_tiles, ROW_TILE, TOP_K).transpose(0, 2, 1)
    blk = jnp.arange(n_blocks, dtype=jnp.int32)
    blk_e = jnp.minimum(jnp.sum(blk[:, None] >= blk_end[None, :], axis=1), N_EXPERTS - 1).astype(jnp.int32)
    blk_rows = jnp.clip(counts[blk_e] - (blk - (blk_end - nblk)[blk_e]) * bm, 0, bm)
    blk_rows = jnp.where(blk < blk_end[-1], blk_rows, 0).astype(jnp.int32)
    return slot_tiles, blk_e, blk_rows, jnp.concatenate([blk_end, nblk]), n_blocks


def _dispatch_kernel(ends_ref, slot_ref, v_ref, x_hbm, zbuf, sem, zsem, *, n_blocks):
    bm = zbuf.shape[0]
    tm = v_ref.shape[0]

    @pl.when(pl.program_id(0) == 0)
    def _():
        zbuf[...] = jnp.zeros_like(zbuf)

        def zero_block(j):
            return pltpu.make_async_copy(zbuf, x_hbm.at[pl.ds(pl.multiple_of(j * bm, bm), bm), :], zsem)

        def each_padded_block(fn):
            for e in range(N_EXPERTS):
                @pl.when(ends_ref[N_EXPERTS + e] > 0)
                def _():
                    fn(zero_block(ends_ref[e] - 1))

            def body(j, c):
                fn(zero_block(j))
                return c

            lax.fori_loop(ends_ref[N_EXPERTS - 1], n_blocks, body, 0)

        each_padded_block(lambda cp: cp.start())
        each_padded_block(lambda cp: cp.wait())

    def row_copy(k, r):
        return pltpu.make_async_copy(v_ref.at[pl.ds(r, 1), :], x_hbm.at[pl.ds(slot_ref[k, r], 1), :], sem)

    def start(r, c):
        for k in range(TOP_K):
            row_copy(k, r).start()
        return c

    def wait(r, c):
        for k in range(TOP_K):
            row_copy(k, r).wait()
        return c

    lax.fori_loop(0, tm, start, 0, unroll=8)
    lax.fori_loop(0, tm, wait, 0, unroll=8)


def _dispatch(v, slot_tiles, ends, n_blocks):
    t, d = v.shape
    grid_spec = pltpu.PrefetchScalarGridSpec(
        num_scalar_prefetch=1,
        grid=(t // ROW_TILE,),
        in_specs=[pl.BlockSpec((None, TOP_K, ROW_TILE), lambda i, ends: (i, 0, 0), memory_space=pltpu.SMEM),
                  pl.BlockSpec((ROW_TILE, d), lambda i, ends: (i, 0))],
        out_specs=pl.BlockSpec(memory_space=pl.ANY),
        scratch_shapes=[pltpu.VMEM((MOE_ROWS, d), F32), pltpu.SemaphoreType.DMA, pltpu.SemaphoreType.DMA],
    )
    return pl.pallas_call(
        functools.partial(_dispatch_kernel, n_blocks=n_blocks),
        grid_spec=grid_spec,
        out_shape=jax.ShapeDtypeStruct((n_blocks * MOE_ROWS, d), F32),
        compiler_params=_cparams("arbitrary"),
        name="moe_dispatch",
    )(ends, slot_tiles, v)


def _expert_kernel(blk_e_ref, blk_rows_ref, x_ref, w1_ref, w3_ref, w2_ref, y_ref, w1_scr, w3_scr, w2_scr):
    j = pl.program_id(0)
    rows = blk_rows_ref[j]
    new_expert = jnp.logical_or(j == 0, blk_e_ref[j] != blk_e_ref[jnp.maximum(j - 1, 0)])

    @pl.when(jnp.logical_and(rows > 0, new_expert))
    def _():
        w1_scr[...] = w1_ref[...].astype(BF16)
        w3_scr[...] = w3_ref[...].astype(BF16)
        w2_scr[...] = w2_ref[...].astype(BF16)

    @pl.when(rows > 0)
    def _():
        x = x_ref[...].astype(BF16)
        hid = _silu(_dot(x, w1_scr[...])) * _dot(x, w3_scr[...])
        y_ref[...] = _dot(hid.astype(BF16), w2_scr[...])

    @pl.when(rows == 0)
    def _():
        y_ref[...] = jnp.zeros_like(y_ref)


def _experts(xs, blk_e, blk_rows, w1, w3, w2):
    p, d = xs.shape
    bm = MOE_ROWS
    de = w1.shape[2]
    grid_spec = pltpu.PrefetchScalarGridSpec(
        num_scalar_prefetch=2,
        grid=(p // bm,),
        in_specs=[pl.BlockSpec((bm, d), lambda j, e, n: (j, 0)),
                  pl.BlockSpec((None, d, de), lambda j, e, n: (e[j], 0, 0)),
                  pl.BlockSpec((None, d, de), lambda j, e, n: (e[j], 0, 0)),
                  pl.BlockSpec((None, de, d), lambda j, e, n: (e[j], 0, 0))],
        out_specs=pl.BlockSpec((bm, d), lambda j, e, n: (j, 0)),
        scratch_shapes=[pltpu.VMEM((d, de), BF16), pltpu.VMEM((d, de), BF16), pltpu.VMEM((de, d), BF16)],
    )
    return pl.pallas_call(
        _expert_kernel,
        grid_spec=grid_spec,
        out_shape=jax.ShapeDtypeStruct((p, d), F32),
        compiler_params=_cparams("arbitrary"),
        name="experts",
    )(blk_e, blk_rows, xs, w1, w3, w2)


def _combine_kernel(slot_ref, slot_next_ref, h_ref, meta_ref, mod_ref, y_hbm, o_ref, buf, sem):
    i = pl.program_id(0)
    n = pl.num_programs(0)
    tm = h_ref.shape[0]

    def fetch(s_ref, b):
        def body(r, c):
            for k in range(TOP_K):
                pltpu.make_async_copy(y_hbm.at[pl.ds(s_ref[k, r], 1), :], buf.at[b, k, pl.ds(r, 1), :],
                                      sem.at[b]).start()
            return c

        lax.fori_loop(0, tm, body, 0, unroll=8)

    @pl.when(i == 0)
    def _():
        fetch(slot_ref, 0)

    @pl.when(i + 1 < n)
    def _():
        fetch(slot_next_ref, (i + 1) % 2)

    b = i % 2

    def wait(r, c):
        for k in range(TOP_K):
            pltpu.make_async_copy(y_hbm.at[pl.ds(0, 1), :], buf.at[b, k, pl.ds(r, 1), :], sem.at[b]).wait()
        return c

    lax.fori_loop(0, tm, wait, 0, unroll=8)
    meta = meta_ref[...]
    f = meta[:, META_GATE:META_GATE + 1] * buf[b, 0] + meta[:, META_GATE + 1:META_GATE + 2] * buf[b, 1]
    o_ref[...] = h_ref[...] + mod_ref[5] * f


def _combine(h, ys, slot_tiles, meta, mods_l, rows):
    d = h.shape[1]
    n_tiles = rows.n_tiles
    slot_spec = lambda fn: pl.BlockSpec((None, TOP_K, ROW_TILE), lambda i: (fn(i), 0, 0), memory_space=pltpu.SMEM)
    return pl.pallas_call(
        _combine_kernel,
        grid=(n_tiles,),
        in_specs=[slot_spec(lambda i: i),
                  slot_spec(lambda i: jnp.minimum(i + 1, n_tiles - 1)),
                  pl.BlockSpec((ROW_TILE, d), lambda i: (i, 0)),
                  pl.BlockSpec((ROW_TILE, LANE), lambda i: (i, 0)),
                  pl.BlockSpec((None, 6, 1, d), lambda i: (rows.mod_row(i), 0, 0, 0)),
                  pl.BlockSpec(memory_space=pl.ANY)],
        out_specs=pl.BlockSpec((ROW_TILE, d), lambda i: (i, 0)),
        out_shape=jax.ShapeDtypeStruct(h.shape, F32),
        scratch_shapes=[pltpu.VMEM((2, TOP_K, ROW_TILE, d), F32), pltpu.SemaphoreType.DMA((2,))],
        compiler_params=_cparams("arbitrary"),
        name="moe_combine",
    )(slot_tiles, slot_tiles, h, meta, mods_l, ys)


def _final_norm_kernel(h_ref, g_ref, o_ref):
    x = h_ref[...]
    o_ref[...] = x * lax.rsqrt(jnp.mean(x * x, axis=-1, keepdims=True) + EPS) * g_ref[...]


def _final_norm(h, g):
    t, d = h.shape
    return pl.pallas_call(
        _final_norm_kernel,
        grid=(t // ROW_TILE,),
        in_specs=[pl.BlockSpec((ROW_TILE, d), lambda i: (i, 0)), pl.BlockSpec((1, d), lambda i: (0, 0))],
        out_specs=pl.BlockSpec((ROW_TILE, d), lambda i: (i, 0)),
        out_shape=jax.ShapeDtypeStruct((t, d), F32),
        compiler_params=_cparams("parallel"),
        name="final_norm",
    )(h, g.reshape(1, d))


def _even_w_in(w):
    d = w.shape[0]
    n_main = 2 * GLA_QK + 2 * GLA_V
    ranks = w[:, n_main:n_main + 2 * GLA_RANK]
    hy = w[:, n_main + 2 * GLA_RANK:]
    pad = jnp.zeros((d, R_COLS - 2 * GLA_RANK), w.dtype)
    return jnp.concatenate([w[:, :n_main], hy, ranks, pad], axis=1).astype(BF16)


def _rank_proj(wa, first_row):
    return jnp.zeros((R_COLS, GLA_QK), F32).at[first_row:first_row + GLA_RANK].set(wa)


def _grid_transpose(h_lat, batch, a, b):
    d = h_lat.shape[1]
    return h_lat.reshape(batch, a, b, d).transpose(0, 2, 1, 3).reshape(-1, d)


def kernel(x, c, ctx, c_ctx, mod_w, mod_b, norm_mix, norm_ffn, norm_final, ev_w_in, ev_w_out, gla_wa_f, gla_ba_f, gla_wa_b, gla_ba_b, gla_norm, hy_conv_w, hy_conv_b, hy_w1, hy_b1, hy_f1, hy_w2, hy_b2, hy_f2, hy_w3, hy_skip, od_w_in, od_conv_w, od_w_out, rt_w_grp, rt_b_grp, rt_w_exp, rt_b_exp, ex_w1, ex_w3, ex_w2):
    batch, s, d = x.shape
    lc = ctx.shape[1]
    depth = mod_w.shape[0]
    tc, tl = batch * lc, batch * s
    assert lc % ROW_TILE == 0 and s % ROW_TILE == 0 and tc % s == 0 and s % GRID_W == 0
    assert lc % GLA_CHUNK == 0 and s % GLA_CHUNK == 0
    assert depth % 2 == 0
    grid_rows = s // GRID_W

    n_cond = -(-(batch + 1) // 8) * 8
    cond = jnp.concatenate([c, c_ctx[None], jnp.zeros((n_cond - batch - 1, d), F32)], axis=0)
    mods = _mods(cond, mod_w, mod_b).reshape(depth, n_cond, 6, 1, d)

    h = jnp.concatenate([ctx.reshape(tc, d), x.reshape(tl, d)], axis=0)
    col_major_now = False

    for l in range(depth):
        i = l // 2
        even = l % 2 == 0
        ctx_out = l < depth - 1
        col_major = i % 2 == 1
        if not ctx_out:
            h = h[tc:]
        lat_only = h.shape[0] == tl
        if col_major != col_major_now:
            lat = h[-tl:]
            lat = (_grid_transpose(lat, batch, grid_rows, GRID_W) if col_major
                   else _grid_transpose(lat, batch, GRID_W, grid_rows))
            h = lat if lat_only else jnp.concatenate([h[:tc], lat], axis=0)
            col_major_now = col_major
        arr_rows = _Rows(batch, lc, s, lat_only)
        mods_l = mods[l]

        if even:
            p = _normmod_mm(h, norm_mix[l], mods_l, _even_w_in(ev_w_in[i]), arr_rows, 0, 1, "even_in")
            waf = _rank_proj(gla_wa_f[i], 0)
            wab = _rank_proj(gla_wa_b[i], GLA_RANK)
            baf, bab = gla_ba_f[i].reshape(1, -1), gla_ba_b[i].reshape(1, -1)
            gain = gla_norm[i].reshape(1, -1)
            zeros = jnp.zeros((batch, 2, 2 * GLA_DV, 2 * GLA_DK), F32)
            filt_args = (hy_w1[i], hy_b1[i], hy_f1[i], hy_w2[i], hy_b2[i], hy_f2[i], hy_w3[i])
            x0c, z = _hyena_pre(p, hy_conv_w[i], hy_conv_b[i], arr_rows)
            gla_c, sc_f, sc_b = _gla(p, waf, wab, baf, bab, gain, zeros, zeros, batch, lc, 0)
            gla_l, _, _ = _gla(p, waf, wab, baf, bab, gain, sc_f, sc_b, batch, s, tc // s)
            kt_c = _hyena_ktab(_hyena_filters(lc, *filt_args), lc)
            kt_l = _hyena_ktab(_hyena_filters(s, *filt_args), s)
            hy_c = _hyena_conv(z, x0c, kt_c, hy_skip[i], batch, lc, 0)
            hy_l = _hyena_conv(z, x0c, kt_l, hy_skip[i], batch, s, tc // s)
            h = _mix_out(gla_c, gla_l, hy_c, hy_l, ev_w_out[i].astype(BF16), h, mods_l, arr_rows)
        else:
            p = _normmod_mm(h, norm_mix[l], mods_l, od_w_in[i].astype(BF16), arr_rows, 0, 1, "odd_in")
            h = _shortconv_out(p, od_conv_w[i], od_w_out[i].astype(BF16), h, mods_l, arr_rows)

        w_rt = jnp.concatenate([rt_w_grp[l], rt_w_exp[l],
                                jnp.zeros((d, LANE - N_GROUPS - N_EXPERTS), F32)], axis=1)
        b_rt = jnp.concatenate([rt_b_grp[l], rt_b_exp[l],
                                jnp.zeros((LANE - N_GROUPS - N_EXPERTS,), F32)]).reshape(1, LANE)
        v, meta, counts = _router(h, norm_ffn[l], mods_l, w_rt, b_rt, arr_rows)
        slot_tiles, blk_e, blk_rows, ends, n_blocks = _dispatch_plan(meta, counts, v.shape[0] * TOP_K)
        xs = _dispatch(v, slot_tiles, ends, n_blocks)
        ys = _experts(xs, blk_e, blk_rows, ex_w1[l], ex_w3[l], ex_w2[l])
        h = _combine(h, ys, slot_tiles, meta, mods_l, arr_rows)

    lat = h[-tl:]
    out = _final_norm(lat, norm_final)
    if col_major_now:
        out = _grid_transpose(out, batch, GRID_W, grid_rows)
    return out.reshape(batch, s, d)
```

```python
import functools
import math

import numpy as np
import jax
import jax.numpy as jnp
from jax import lax
from jax.experimental import pallas as pl
from jax.experimental.pallas import tpu as pltpu

F32 = jnp.float32
BF16 = jnp.bfloat16

EPS = 1e-6
GRID_W = 64

GLA_HEADS = 4
GLA_DK = 64
GLA_DV = 128
GLA_RANK = 16
GLA_TAU = 16.0
GLA_CHUNK = 64
GLA_SLAB = 256
GLA_STEP_UNROLL = 4
GLA_QK = GLA_HEADS * GLA_DK
GLA_V = GLA_HEADS * GLA_DV

HY_WIDTH = 512
HY_EMB = 33
HY_BANDS = (HY_EMB - 1) // 2
HY_HIDDEN = 64
HY_FAST_DECAY = 0.3
HY_SLOW_DECAY = 1.5
HY_TARGET = 1e-2

N_GROUPS = 4
EXP_PER_GROUP = 8
N_EXPERTS = N_GROUPS * EXP_PER_GROUP
TOP_K = 2

LANE = 128
ROW_TILE = 256
MOE_ROWS = 256
R_COLS = LANE
VMEM_LIMIT = 56 * 1024 * 1024


def _cparams(*sem):
    return pltpu.CompilerParams(dimension_semantics=sem, vmem_limit_bytes=VMEM_LIMIT)


def _split_bf16(a):
    hi = a.astype(BF16)
    lo = (a - hi.astype(F32)).astype(BF16)
    return hi, lo


def _dot(a, b):
    return jnp.dot(a, b, preferred_element_type=F32)


def _dot_nt(a, b):
    return lax.dot_general(a, b, (((1,), (1,)), ((), ())), preferred_element_type=F32)


def _dot_tn(a, b):
    return lax.dot_general(a, b, (((0,), (0,)), ((), ())), preferred_element_type=F32)


def _dot3(a, b):
    ah, al = _split_bf16(a)
    bh, bl = _split_bf16(b)
    return _dot(ah, bh) + _dot(ah, bl) + _dot(al, bh)


def _silu(x):
    return x / (1.0 + jnp.exp(-x))


def _log_sigmoid(x):
    return jnp.minimum(x, 0.0) - jnp.log1p(jnp.exp(-jnp.abs(x)))


def _norm_mod(x, g, shift, scale):
    y = x * lax.rsqrt(jnp.mean(x * x, axis=-1, keepdims=True) + EPS)
    return (y * g) * (1.0 + scale) + shift


def _mods_kernel(s_ref, w_ref, b_ref, o_ref):
    s = s_ref[...]
    s = _silu(s)
    o_ref[...] = _dot3(s, w_ref[...]) + b_ref[...]


def _mods(cond, mod_w, mod_b):
    depth, d, n = mod_w.shape
    r = cond.shape[0]
    tn = 1024
    return pl.pallas_call(
        _mods_kernel,
        grid=(depth, n // tn),
        in_specs=[pl.BlockSpec((r, d), lambda l, j: (0, 0)),
                  pl.BlockSpec((None, d, tn), lambda l, j: (l, 0, j)),
                  pl.BlockSpec((None, 1, tn), lambda l, j: (l, 0, j))],
        out_specs=pl.BlockSpec((None, r, tn), lambda l, j: (l, 0, j)),
        out_shape=jax.ShapeDtypeStruct((depth, r, n), F32),
        compiler_params=_cparams("parallel", "parallel"),
        name="mods",
    )(cond, mod_w, mod_b.reshape(depth, 1, n))


class _Rows:
    def __init__(self, batch, lc, s, lat_only):
        self.batch = batch
        self.nct = 0 if lat_only else batch * lc // ROW_TILE
        self.tps = s // ROW_TILE
        self.tpc = lc // ROW_TILE
        self.n_tiles = self.nct + batch * self.tps

    def mod_row(self, i):
        return jnp.where(i < self.nct, self.batch, (i - self.nct) // self.tps)

    def seq_edges(self, i):
        pos_c = i % self.tpc
        pos_l = (i - self.nct) % self.tps
        is_c = i < self.nct
        first = jnp.where(is_c, pos_c == 0, pos_l == 0)
        last = jnp.where(is_c, pos_c == self.tpc - 1, pos_l == self.tps - 1)
        return first, last


def _normmod_mm_kernel(h_ref, g_ref, mod_ref, w_ref, o_ref, *, shift_i, scale_i):
    u = _norm_mod(h_ref[...], g_ref[...], mod_ref[shift_i], mod_ref[scale_i])
    o_ref[...] = _dot(u.astype(BF16), w_ref[...])


def _normmod_mm(h, g, mods_l, w, rows, shift_i, scale_i, name):
    d = h.shape[1]
    n = w.shape[1]
    return pl.pallas_call(
        functools.partial(_normmod_mm_kernel, shift_i=shift_i, scale_i=scale_i),
        grid=(rows.n_tiles,),
        in_specs=[pl.BlockSpec((ROW_TILE, d), lambda i: (i, 0)),
                  pl.BlockSpec((1, d), lambda i: (0, 0)),
                  pl.BlockSpec((None, 6, 1, d), lambda i: (rows.mod_row(i), 0, 0, 0)),
                  pl.BlockSpec((d, n), lambda i: (0, 0))],
        out_specs=pl.BlockSpec((ROW_TILE, n), lambda i: (i, 0)),
        out_shape=jax.ShapeDtypeStruct((rows.n_tiles * ROW_TILE, n), F32),
        compiler_params=_cparams("parallel"),
        name=name,
    )(h, g.reshape(1, d), mods_l, w)


def _gla_kernel(q_ref, k_ref, v_ref, g_ref, r_ref, waf_ref, wab_ref, baf_ref, bab_ref, gain_ref,
                s0f_ref, s0b_ref, o_ref, sf_ref, sb_ref, gl_scr, tot_scr, qd_scr, ds_scr, o_scr, st_scr,
                *, seq_len):
    C, SL = GLA_CHUNK, GLA_SLAB
    cps = SL // C
    n_chunks, n_slabs = seq_len // C, seq_len // SL
    head_of_lane = lax.broadcasted_iota(jnp.int32, (SL, 2 * GLA_DK), 1) // GLA_DK
    row = lax.broadcasted_iota(jnp.int32, (SL, SL), 0)
    col = lax.broadcasted_iota(jnp.int32, (SL, SL), 1)
    same_chunk = (row // C) == (col // C)
    srow = lax.broadcasted_iota(jnp.int32, (2 * GLA_DV, 2 * GLA_DK), 0) // GLA_DV
    scol = lax.broadcasted_iota(jnp.int32, (2 * GLA_DV, 2 * GLA_DK), 1) // GLA_DK
    same_head = srow == scol
    r = r_ref[...]

    def direction(wa_ref, ba_ref, s0_ref, s_out_ref, forward):
        mask = jnp.logical_and(same_chunk, (row >= col) if forward else (row <= col))
        tri = jnp.where(mask, 1.0, 0.0).astype(BF16)
        gl_scr[...] = _log_sigmoid(_dot3(r, wa_ref[...]) + ba_ref[...]) * (1.0 / GLA_TAU)

        def slab(s, carry):
            rows = pl.ds(pl.multiple_of(s * SL, SL), SL)
            gl = gl_scr[rows, :]
            g_hi = gl.astype(BF16)
            g_r1 = gl - g_hi.astype(F32)
            g_mid = g_r1.astype(BF16)
            g_lo = (g_r1 - g_mid.astype(F32)).astype(BF16)
            b = _dot(tri, g_hi) + _dot(tri, g_mid) + _dot(tri, g_lo)
            b3 = b.reshape(cps, C, 2 * GLA_DK)
            last = b3[:, C - 1:C, :] if forward else b3[:, 0:1, :]
            tot = jnp.broadcast_to(last, b3.shape).reshape(SL, 2 * GLA_DK)
            tot_scr[rows, :] = tot
            q = q_ref[rows, :] * (GLA_DK ** -0.5)
            k = k_ref[rows, :]
            v = v_ref[rows, :].astype(BF16)
            qd = q * jnp.exp(b)
            kd = (k * jnp.exp(-b)).astype(BF16)
            kr = (k * jnp.exp(tot - b)).astype(BF16)
            qd_scr[rows, :] = qd.astype(BF16)
            o_parts = []
            for h in range(2):
                qh = jnp.where(head_of_lane == h, qd, 0.0).astype(BF16)
                a = jnp.where(mask, _dot_nt(qh, kd), 0.0).astype(BF16)
                o_parts.append(_dot(a, v[:, h * GLA_DV:(h + 1) * GLA_DV]))
            o = jnp.concatenate(o_parts, axis=1)
            if forward:
                o_scr[rows, :] = o
            else:
                o_scr[rows, :] += o
            for c in range(cps):
                ds = _dot_tn(v[c * C:(c + 1) * C], kr[c * C:(c + 1) * C])
                ds_scr[s * cps + c] = jnp.where(same_head, ds, 0.0)
            return carry

        lax.fori_loop(0, n_slabs, slab, 0)
        st_scr[...] = s0_ref[...]

        def step(i, carry):
            c = i if forward else n_chunks - 1 - i
            first = pl.multiple_of(c * C, C)
            rows = pl.ds(first, C)
            st = st_scr[...]
            o_scr[rows, :] += _dot_nt(qd_scr[rows, :], st.astype(BF16))
            st_scr[...] = st * jnp.exp(tot_scr[pl.ds(first, 1), :]) + ds_scr[c]
            return carry

        lax.fori_loop(0, n_chunks, step, 0, unroll=GLA_STEP_UNROLL)
        s_out_ref[...] = st_scr[...]

    direction(waf_ref, baf_ref, s0f_ref, sf_ref, True)
    direction(wab_ref, bab_ref, s0b_ref, sb_ref, False)
    gain = gain_ref[...]

    def readout(s, carry):
        rows = pl.ds(pl.multiple_of(s * SL, SL), SL)
        o = o_scr[rows, :]
        outs = []
        for h in range(2):
            oh = o[:, h * GLA_DV:(h + 1) * GLA_DV]
            outs.append(oh * lax.rsqrt(jnp.mean(oh * oh, axis=-1, keepdims=True) + EPS) * gain)
        o_ref[rows, :] = jnp.concatenate(outs, axis=1) * _silu(g_ref[rows, :])
        return carry

    lax.fori_loop(0, n_slabs, readout, 0)


def _gla(p, waf, wab, baf, bab, gain, s0f, s0b, batch, seq_len, blk0):
    dk2, dv2 = 2 * GLA_DK, 2 * GLA_DV
    seq = lambda width, cb: pl.BlockSpec((seq_len, width), lambda b, hp: (blk0 + b, cb(hp)))
    state_spec = pl.BlockSpec((None, None, dv2, dk2), lambda b, hp: (b, hp, 0, 0))
    in_specs = [seq(dk2, lambda hp: hp),
                seq(dk2, lambda hp: GLA_QK // dk2 + hp),
                seq(dv2, lambda hp: 2 * GLA_QK // dv2 + hp),
                seq(dv2, lambda hp: (2 * GLA_QK + GLA_V) // dv2 + hp),
                seq(R_COLS, lambda hp: (2 * GLA_QK + 2 * GLA_V + 3 * HY_WIDTH) // R_COLS),
                pl.BlockSpec((R_COLS, dk2), lambda b, hp: (0, hp)),
                pl.BlockSpec((R_COLS, dk2), lambda b, hp: (0, hp)),
                pl.BlockSpec((1, dk2), lambda b, hp: (0, hp)),
                pl.BlockSpec((1, dk2), lambda b, hp: (0, hp)),
                pl.BlockSpec((1, GLA_DV), lambda b, hp: (0, 0)),
                state_spec, state_spec]
    state_shape = jax.ShapeDtypeStruct((batch, 2, dv2, dk2), F32)
    return pl.pallas_call(
        functools.partial(_gla_kernel, seq_len=seq_len),
        grid=(batch, 2),
        in_specs=in_specs,
        out_specs=[pl.BlockSpec((seq_len, dv2), lambda b, hp: (b, hp)), state_spec, state_spec],
        out_shape=[jax.ShapeDtypeStruct((batch * seq_len, GLA_V), F32), state_shape, state_shape],
        scratch_shapes=[pltpu.VMEM((seq_len, dk2), F32),
                        pltpu.VMEM((seq_len, dk2), F32),
                        pltpu.VMEM((seq_len, dk2), BF16),
                        pltpu.VMEM((seq_len // GLA_CHUNK, dv2, dk2), F32),
                        pltpu.VMEM((seq_len, dv2), F32),
                        pltpu.VMEM((dv2, dk2), F32)],
        compiler_params=_cparams("parallel", "parallel"),
        name="gla",
    )(p, p, p, p, p, waf, wab, baf, bab, gain, s0f, s0b)


def _conv3(m, prev_row, next_row, w_ref):
    n = m.shape[0]
    ridx = lax.broadcasted_iota(jnp.int32, m.shape, 0)
    m_prev = jnp.where(ridx == 0, prev_row, pltpu.roll(m, 1, 0))
    m_next = jnp.where(ridx == n - 1, next_row, pltpu.roll(m, n - 1, 0))
    return w_ref[0:1, :] * m_prev + w_ref[1:2, :] * m + w_ref[2:3, :] * m_next


def _halo_specs(width, col_block, t_rows):
    g = ROW_TILE // 8
    last = t_rows // 8 - 1
    prev = pl.BlockSpec((8, width), lambda i: (jnp.maximum(i * g - 1, 0), col_block))
    nxt = pl.BlockSpec((8, width), lambda i: (jnp.minimum((i + 1) * g, last), col_block))
    return prev, nxt


def _hyena_pre_kernel(x0_ref, x1_ref, v_ref, x0p_ref, x0n_ref, x1p_ref, x1n_ref, vp_ref, vn_ref,
                      w_ref, b_ref, x0c_ref, z_ref, *, rows):
    first, last = rows.seq_edges(pl.program_id(0))
    keep_p = jnp.where(first, 0.0, 1.0)
    keep_n = jnp.where(last, 0.0, 1.0)
    hw = HY_WIDTH

    def conv(ref, p_ref, n_ref, j):
        w = w_ref.at[:, j * hw:(j + 1) * hw]
        y = _conv3(ref[...], p_ref[7:8, :] * keep_p, n_ref[0:1, :] * keep_n, w)
        return y + b_ref[:, j * hw:(j + 1) * hw]

    x0c_ref[...] = conv(x0_ref, x0p_ref, x0n_ref, 0)
    z_ref[...] = conv(v_ref, vp_ref, vn_ref, 2) * conv(x1_ref, x1p_ref, x1n_ref, 1)


def _hyena_pre(p, conv_w, conv_b, rows):
    t_rows = p.shape[0]
    hw = HY_WIDTH
    cb0 = (2 * GLA_QK + 2 * GLA_V) // hw
    in_specs = [pl.BlockSpec((ROW_TILE, hw), lambda i, j=j: (i, cb0 + j)) for j in range(3)]
    for j in range(3):
        in_specs.extend(_halo_specs(hw, cb0 + j, t_rows))
    in_specs += [pl.BlockSpec((3, 3 * hw), lambda i: (0, 0)), pl.BlockSpec((1, 3 * hw), lambda i: (0, 0))]
    out_spec = pl.BlockSpec((ROW_TILE, hw), lambda i: (i, 0))
    shape = jax.ShapeDtypeStruct((rows.n_tiles * ROW_TILE, hw), F32)
    return pl.pallas_call(
        functools.partial(_hyena_pre_kernel, rows=rows),
        grid=(rows.n_tiles,),
        in_specs=in_specs,
        out_specs=[out_spec, out_spec],
        out_shape=[shape, shape],
        compiler_params=_cparams("parallel"),
        name="hyena_pre",
    )(p, p, p, p, p, p, p, p, p, conv_w, conv_b.reshape(1, 3 * hw))


def _filter_kernel(z_ref, w1_ref, b1_ref, f1_ref, w2_ref, b2_ref, f2_ref, w3_ref, win_ref, o_ref):
    hh = jnp.sin(f1_ref[...] * (_dot3(z_ref[...], w1_ref[...]) + b1_ref[...]))
    hh = jnp.sin(f2_ref[...] * (_dot3(hh, w2_ref[...]) + b2_ref[...]))
    win = win_ref[...]
    o_ref[...] = _dot3(hh, w3_ref[...]) * jnp.concatenate([win, win], axis=1)


@functools.lru_cache(maxsize=None)
def _filter_features(L):
    t = np.linspace(0.0, 1.0, L, dtype=np.float32)[:, None]
    pos = np.arange(L, dtype=np.float32)[:, None]
    bands = np.linspace(1e-4, HY_BANDS - 1, HY_BANDS, dtype=np.float32)[None]
    ang = (np.float32(2.0 * math.pi / L) * pos * bands).astype(np.float32)
    z = np.concatenate([t, np.cos(ang), np.sin(ang)], axis=-1).astype(np.float32)
    z = np.pad(z, ((0, 0), (0, LANE - HY_EMB)))
    max_decay = math.log(HY_TARGET) / HY_FAST_DECAY
    min_decay = math.log(HY_TARGET) / HY_SLOW_DECAY
    deltas = np.linspace(min_decay, max_decay, HY_WIDTH, dtype=np.float32)
    window = np.exp(-t * np.abs(deltas)[None]).astype(np.float32)
    return z, window


def _hyena_filters(L, w1, b1, f1, w2, b2, f2, w3):
    z, window = _filter_features(L)
    w1p = jnp.pad(w1, ((0, LANE - HY_EMB), (0, 0)))
    tl = min(L, 512)
    full = lambda a: pl.BlockSpec(a.shape, lambda i: (0,) * a.ndim)
    row = lambda a: a.reshape(1, -1)
    ops = [w1p, row(b1), row(f1), w2, row(b2), row(f2), w3]
    return pl.pallas_call(
        _filter_kernel,
        grid=(L // tl,),
        in_specs=[pl.BlockSpec((tl, LANE), lambda i: (i, 0))] + [full(a) for a in ops]
                 + [pl.BlockSpec((tl, HY_WIDTH), lambda i: (i, 0))],
        out_specs=pl.BlockSpec((tl, 2 * HY_WIDTH), lambda i: (i, 0)),
        out_shape=jax.ShapeDtypeStruct((L, 2 * HY_WIDTH), F32),
        compiler_params=_cparams("parallel"),
        name="hyena_filters",
    )(jnp.asarray(z), *ops, jnp.asarray(window))


def _freq_tile(L):
    return min(2 * L, 512)


@functools.lru_cache(maxsize=None)
def _dft_tables(L):
    n = 2 * L
    tf = _freq_tile(L)
    half = tf // 2
    t = np.arange(L, dtype=np.int64)[None, :]
    fm = np.zeros((n, L), np.float64)
    scale = np.zeros((n, 1), np.float64)
    sign = np.zeros((n, 1), np.float64)
    for j in range(n // tf):
        k = (np.arange(half, dtype=np.int64) + j * half)[:, None]
        ang = 2.0 * np.pi * ((k * t) % n).astype(np.float64) / n
        fm[j * tf:j * tf + half] = np.cos(ang)
        fm[j * tf + half:(j + 1) * tf] = -np.sin(ang)
        scale[j * tf:(j + 1) * tf] = 2.0 / n
        sign[j * tf:j * tf + half] = 1.0
        sign[j * tf + half:(j + 1) * tf] = -1.0
    fm[half] = np.cos(np.pi * t[0])
    scale[0] = 1.0 / n
    scale[half] = 1.0 / n
    sign[half] = 1.0
    return (fm.astype(np.float32), np.ascontiguousarray(fm.T).astype(np.float32),
            scale.astype(np.float32), sign.astype(np.float32))


def _ktab_kernel(f_ref, h_ref, scale_ref, sign_ref, o_ref):
    hw = HY_WIDTH
    hh, hl = _split_bf16(h_ref[...])
    f = f_ref[...]
    kk = _dot(f, hh) + _dot(f, hl)
    o_ref[...] = scale_ref[...] * (kk[:, :hw] + sign_ref[...] * kk[:, hw:])


def _hyena_ktab(filt, L):
    fm, _, scale, sign = _dft_tables(L)
    n = 2 * L
    tf = _freq_tile(L)
    return pl.pallas_call(
        _ktab_kernel,
        grid=(n // tf,),
        in_specs=[pl.BlockSpec((tf, L), lambda j: (j, 0)),
                  pl.BlockSpec((L, 2 * HY_WIDTH), lambda j: (0, 0)),
                  pl.BlockSpec((tf, 1), lambda j: (j, 0)),
                  pl.BlockSpec((tf, 1), lambda j: (j, 0))],
        out_specs=pl.BlockSpec((tf, HY_WIDTH), lambda j: (j, 0)),
        out_shape=jax.ShapeDtypeStruct((n, HY_WIDTH), F32),
        compiler_params=_cparams("parallel"),
        name="hyena_ktab",
    )(jnp.asarray(fm, dtype=BF16), filt, jnp.asarray(scale), jnp.asarray(sign))


def _hyena_conv_kernel(z_ref, x0_ref, f_ref, ft_ref, k_ref, skip_ref, o_ref, zb_scr, acc_scr, *, tf):
    j = pl.program_id(1)
    half = tf // 2

    @pl.when(j == 0)
    def _():
        zb_scr[...] = z_ref[...].astype(BF16)
        acc_scr[...] = jnp.zeros_like(acc_scr)

    zf = _dot(f_ref[...], zb_scr[...])
    re, im = zf[:half], zf[half:]
    kre, kim = k_ref[:half, :], k_ref[half:, :]
    ridx = lax.broadcasted_iota(jnp.int32, re.shape, 0)
    mix = jnp.where(jnp.logical_and(j == 0, ridx == 0), 0.0, 1.0)
    yre = re * kre - mix * (im * kim)
    yim = mix * (re * kim) + im * jnp.where(mix == 0.0, kim, kre)
    y = jnp.concatenate([yre, yim], axis=0).astype(BF16)
    acc_scr[...] += _dot(ft_ref[...], y)

    @pl.when(j == pl.num_programs(1) - 1)
    def _():
        o_ref[...] = x0_ref[...] * (acc_scr[...] + z_ref[...] * skip_ref[...])


def _hyena_conv(z, x0c, ktab, skip, batch, L, blk0):
    fm, fmt, _, _ = _dft_tables(L)
    n = 2 * L
    tf = _freq_tile(L)
    hw = HY_WIDTH
    return pl.pallas_call(
        functools.partial(_hyena_conv_kernel, tf=tf),
        grid=(batch, n // tf),
        in_specs=[pl.BlockSpec((L, hw), lambda b, j: (blk0 + b, 0)),
                  pl.BlockSpec((L, hw), lambda b, j: (blk0 + b, 0)),
                  pl.BlockSpec((tf, L), lambda b, j: (j, 0)),
                  pl.BlockSpec((L, tf), lambda b, j: (0, j)),
                  pl.BlockSpec((tf, hw), lambda b, j: (j, 0)),
                  pl.BlockSpec((1, hw), lambda b, j: (0, 0))],
        out_specs=pl.BlockSpec((L, hw), lambda b, j: (b, 0)),
        out_shape=jax.ShapeDtypeStruct((batch * L, hw), F32),
        scratch_shapes=[pltpu.VMEM((L, hw), BF16), pltpu.VMEM((L, hw), F32)],
        compiler_params=_cparams("parallel", "arbitrary"),
        name="hyena_conv",
    )(z, x0c, jnp.asarray(fm, dtype=BF16), jnp.asarray(fmt, dtype=BF16), ktab, skip.reshape(1, hw))


def _mix_out_kernel(ac_ref, al_ref, bc_ref, bl_ref, wa_ref, wb_ref, h_ref, mod_ref, o_ref, *, nct):
    is_ctx = pl.program_id(0) < nct
    a = jnp.where(is_ctx, ac_ref[...], al_ref[...])
    b = jnp.where(is_ctx, bc_ref[...], bl_ref[...])
    y = _dot(a.astype(BF16), wa_ref[...]) + _dot(b.astype(BF16), wb_ref[...])
    o_ref[...] = h_ref[...] + mod_ref[2] * y


def _mix_out(a_ctx, a_lat, b_ctx, b_lat, w_out, h, mods_l, rows):
    d = h.shape[1]
    ka, kb = a_ctx.shape[1], b_ctx.shape[1]
    nct = rows.nct
    ctx_spec = lambda k: pl.BlockSpec((ROW_TILE, k), lambda i: (jnp.minimum(i, nct - 1), 0))
    lat_spec = lambda k: pl.BlockSpec((ROW_TILE, k), lambda i: (jnp.maximum(i - nct, 0), 0))
    return pl.pallas_call(
        functools.partial(_mix_out_kernel, nct=nct),
        grid=(rows.n_tiles,),
        in_specs=[ctx_spec(ka), lat_spec(ka), ctx_spec(kb), lat_spec(kb),
                  pl.BlockSpec((ka, d), lambda i: (0, 0)),
                  pl.BlockSpec((kb, d), lambda i: (ka // kb, 0)),
                  pl.BlockSpec((ROW_TILE, d), lambda i: (i, 0)),
                  pl.BlockSpec((None, 6, 1, d), lambda i: (rows.mod_row(i), 0, 0, 0))],
        out_specs=pl.BlockSpec((ROW_TILE, d), lambda i: (i, 0)),
        out_shape=jax.ShapeDtypeStruct((rows.n_tiles * ROW_TILE, d), F32),
        compiler_params=_cparams("parallel"),
        name="mix_out",
    )(a_ctx, a_lat, b_ctx, b_lat, w_out, w_out, h, mods_l)


def _shortconv_out_kernel(bg_ref, cg_ref, xi_ref, cgp_ref, cgn_ref, xip_ref, xin_ref, cw_ref, w_ref,
                          h_ref, mod_ref, o_ref, *, rows):
    first, last = rows.seq_edges(pl.program_id(0))
    keep_p = jnp.where(first, 0.0, 1.0)
    keep_n = jnp.where(last, 0.0, 1.0)
    m = cg_ref[...] * xi_ref[...]
    m_prev = cgp_ref[7:8, :] * xip_ref[7:8, :] * keep_p
    m_next = cgn_ref[0:1, :] * xin_ref[0:1, :] * keep_n
    y = bg_ref[...] * _conv3(m, m_prev, m_next, cw_ref)
    o_ref[...] = h_ref[...] + mod_ref[2] * _dot(y.astype(BF16), w_ref[...])


def _shortconv_out(p, conv_w, w_out, h, mods_l, rows):
    d = h.shape[1]
    t_rows = p.shape[0]
    in_specs = [pl.BlockSpec((ROW_TILE, d), lambda i, j=j: (i, j)) for j in range(3)]
    in_specs += [*_halo_specs(d, 1, t_rows), *_halo_specs(d, 2, t_rows),
                 pl.BlockSpec((3, d), lambda i: (0, 0)),
                 pl.BlockSpec((d, d), lambda i: (0, 0)),
                 pl.BlockSpec((ROW_TILE, d), lambda i: (i, 0)),
                 pl.BlockSpec((None, 6, 1, d), lambda i: (rows.mod_row(i), 0, 0, 0))]
    return pl.pallas_call(
        functools.partial(_shortconv_out_kernel, rows=rows),
        grid=(rows.n_tiles,),
        in_specs=in_specs,
        out_specs=pl.BlockSpec((ROW_TILE, d), lambda i: (i, 0)),
        out_shape=jax.ShapeDtypeStruct((rows.n_tiles * ROW_TILE, d), F32),
        compiler_params=_cparams("parallel"),
        name="shortconv_out",
    )(p, p, p, p, p, p, p, conv_w, w_out, h, mods_l)


META_E, META_RANK, META_GATE = 0, 2, 4


def _lane_min_index(mask, lane_f):
    return jnp.min(jnp.where(mask, lane_f, float(LANE)), axis=1, keepdims=True)


def _router_kernel(h_ref, g_ref, mod_ref, w_ref, b_ref, v_ref, meta_ref, cnt_ref, carry_scr):
    @pl.when(pl.program_id(0) == 0)
    def _():
        carry_scr[...] = jnp.zeros_like(carry_scr)

    v = _norm_mod(h_ref[...], g_ref[...], mod_ref[3], mod_ref[4])
    v_ref[...] = v
    lg = _dot3(v, w_ref[...]) + b_ref[...]
    tm = lg.shape[0]
    lane = lax.broadcasted_iota(jnp.int32, lg.shape, 1)
    lane_f = lane.astype(F32)
    neg = -jnp.inf

    is_grp = lane < N_GROUPS
    lgm = jnp.where(is_grp, lg, neg)
    m_g = jnp.max(lgm, axis=1, keepdims=True)
    s_g = jnp.sum(jnp.where(is_grp, jnp.exp(lg - m_g), 0.0), axis=1, keepdims=True)
    p_g = 1.0 / s_g
    grp = _lane_min_index(lgm == m_g, lane_f)

    ex_lane = lane - N_GROUPS
    in_grp = jnp.logical_and(jnp.logical_and(ex_lane >= 0, ex_lane < N_EXPERTS),
                             (ex_lane // EXP_PER_GROUP).astype(F32) == grp)
    m_e = jnp.max(jnp.where(in_grp, lg, neg), axis=1, keepdims=True)
    ee = jnp.where(in_grp, jnp.exp(lg - m_e), 0.0)
    pe = ee / jnp.sum(ee, axis=1, keepdims=True)
    pe1 = jnp.where(in_grp, pe, -1.0)
    p1 = jnp.max(pe1, axis=1, keepdims=True)
    i1 = _lane_min_index(pe1 == p1, lane_f)
    pe2 = jnp.where(lane_f == i1, -1.0, pe1)
    p2 = jnp.max(pe2, axis=1, keepdims=True)
    i2 = _lane_min_index(pe2 == p2, lane_f)
    denom = p1 + p2
    g1 = p_g * p1 / denom
    g2 = p_g * p2 / denom
    e1 = i1 - float(N_GROUPS)
    e2 = i2 - float(N_GROUPS)

    oh1 = lane_f == e1
    oh2 = lane_f == e2
    row = lax.broadcasted_iota(jnp.int32, (tm, tm), 0)
    col = lax.broadcasted_iota(jnp.int32, (tm, tm), 1)
    earlier = jnp.where(row > col, 1.0, 0.0).astype(BF16)
    c1 = _dot(earlier, jnp.where(oh1, 1.0, 0.0).astype(BF16))
    c2 = _dot(earlier, jnp.where(oh2, 1.0, 0.0).astype(BF16))
    tot1 = jnp.sum(jnp.where(oh1, 1.0, 0.0), axis=0, keepdims=True)
    tot2 = jnp.sum(jnp.where(oh2, 1.0, 0.0), axis=0, keepdims=True)
    carry = carry_scr[...]
    r1 = jnp.sum(jnp.where(oh1, carry + c1, 0.0), axis=1, keepdims=True)
    r2 = jnp.sum(jnp.where(oh2, carry + tot1 + c2, 0.0), axis=1, keepdims=True)
    carry = carry + tot1 + tot2
    carry_scr[...] = carry
    cnt_ref[...] = carry

    meta = jnp.zeros_like(lg)
    for idx, val in ((META_E, e1), (META_E + 1, e2), (META_RANK, r1), (META_RANK + 1, r2),
                     (META_GATE, g1), (META_GATE + 1, g2)):
        meta = jnp.where(lane == idx, val, meta)
    meta_ref[...] = meta


def _router(h, g, mods_l, w_rt, b_rt, rows):
    d = h.shape[1]
    nt = rows.n_tiles * ROW_TILE
    return pl.pallas_call(
        _router_kernel,
        grid=(rows.n_tiles,),
        in_specs=[pl.BlockSpec((ROW_TILE, d), lambda i: (i, 0)),
                  pl.BlockSpec((1, d), lambda i: (0, 0)),
                  pl.BlockSpec((None, 6, 1, d), lambda i: (rows.mod_row(i), 0, 0, 0)),
                  pl.BlockSpec((d, LANE), lambda i: (0, 0)),
                  pl.BlockSpec((1, LANE), lambda i: (0, 0))],
        out_specs=[pl.BlockSpec((ROW_TILE, d), lambda i: (i, 0)),
                   pl.BlockSpec((ROW_TILE, LANE), lambda i: (i, 0)),
                   pl.BlockSpec((1, LANE), lambda i: (0, 0))],
        out_shape=[jax.ShapeDtypeStruct((nt, d), F32), jax.ShapeDtypeStruct((nt, LANE), F32),
                   jax.ShapeDtypeStruct((1, LANE), F32)],
        scratch_shapes=[pltpu.VMEM((1, LANE), F32)],
        compiler_params=_cparams("arbitrary"),
        name="router",
    )(h, g.reshape(1, d), mods_l, w_rt, b_rt)


def _dispatch_plan(meta, counts):
    bm = MOE_ROWS
    t = meta.shape[0]
    n_blocks = -(-(t * TOP_K + N_EXPERTS * (bm - 1)) // bm)
    counts = counts[0, :N_EXPERTS].astype(jnp.int32)
    nblk = (counts + bm - 1) // bm
    blk_end = jnp.cumsum(nblk).astype(jnp.int32)
    slot0 = (blk_end - nblk) * bm
    expert = meta[:, META_E:META_E + TOP_K].astype(jnp.int32)
    rank = meta[:, META_RANK:META_RANK + TOP_K].astype(jnp.int32)
    onehot = expert[:, :, None] == jnp.arange(N_EXPERTS, dtype=jnp.int32)
    slot = rank + jnp.sum(jnp.where(onehot, slot0, 0), axis=-1)
    token = jnp.broadcast_to(jnp.arange(t, dtype=jnp.int32)[:, None], slot.shape)
    src = jnp.zeros((n_blocks * bm,), jnp.int32).at[slot.reshape(-1)].set(token.reshape(-1), unique_indices=True)
    blk = jnp.arange(n_blocks, dtype=jnp.int32)
    blk_e = jnp.minimum(jnp.sum(blk[:, None] >= blk_end[None, :], axis=1), N_EXPERTS - 1).astype(jnp.int32)
    blk_used = (blk < blk_end[-1]).astype(jnp.int32)
    return slot, src.reshape(n_blocks, 1, bm), blk_e, blk_used


def _slot_tiles(slot):
    return slot.reshape(slot.shape[0] // ROW_TILE, ROW_TILE, TOP_K).transpose(0, 2, 1)


def _row_copies_start(src_hbm, idx_ref, k, dst, sem):
    for r in range(dst.shape[0]):
        pltpu.make_async_copy(src_hbm.at[pl.ds(idx_ref[k, r], 1), :], dst.at[pl.ds(r, 1), :], sem).start()


def _row_copies_wait(src_hbm, dst, sem):
    pltpu.make_async_copy(src_hbm.at[pl.ds(0, dst.shape[0]), :], dst, sem).wait()


def _expert_kernel(blk_e_ref, blk_used_ref, src_ref, src_next_ref, v_hbm, w1_ref, w3_ref, w2_ref, y_ref,
                   xbuf, w1_scr, w3_scr, w2_scr, sem):
    j = pl.program_id(0)
    nb = pl.num_programs(0)
    used = blk_used_ref[j] > 0
    new_expert = jnp.logical_or(j == 0, blk_e_ref[j] != blk_e_ref[jnp.maximum(j - 1, 0)])

    @pl.when(j == 0)
    def _():
        _row_copies_start(v_hbm, src_ref, 0, xbuf.at[0], sem.at[0])

    @pl.when(jnp.logical_and(used, new_expert))
    def _():
        w1_scr[...] = w1_ref[...].astype(BF16)
        w3_scr[...] = w3_ref[...].astype(BF16)
        w2_scr[...] = w2_ref[...].astype(BF16)

    @pl.when(used)
    def _():
        b = j % 2
        _row_copies_wait(v_hbm, xbuf.at[b], sem.at[b])
        x = xbuf[b].astype(BF16)
        _row_copies_start(v_hbm, src_next_ref, 0, xbuf.at[1 - b], sem.at[1 - b])
        hid = _silu(_dot(x, w1_scr[...])) * _dot(x, w3_scr[...])
        y_ref[...] = _dot(hid.astype(BF16), w2_scr[...])

        @pl.when(jnp.logical_or(j == nb - 1, blk_used_ref[jnp.minimum(j + 1, nb - 1)] == 0))
        def _():
            _row_copies_wait(v_hbm, xbuf.at[1 - b], sem.at[1 - b])

    @pl.when(jnp.logical_not(used))
    def _():
        y_ref[...] = jnp.zeros_like(y_ref)


def _experts(v, src, blk_e, blk_used, w1, w3, w2):
    t, d = v.shape
    n_blocks = src.shape[0]
    bm = MOE_ROWS
    de = w1.shape[2]

    def next_block(j, e, used):
        nxt = jnp.minimum(j + 1, n_blocks - 1)
        return jnp.where(used[nxt] > 0, nxt, j)

    grid_spec = pltpu.PrefetchScalarGridSpec(
        num_scalar_prefetch=2,
        grid=(n_blocks,),
        in_specs=[pl.BlockSpec((None, 1, bm), lambda j, e, u: (j, 0, 0), memory_space=pltpu.SMEM),
                  pl.BlockSpec((None, 1, bm), lambda j, e, u: (next_block(j, e, u), 0, 0), memory_space=pltpu.SMEM),
                  pl.BlockSpec(memory_space=pl.ANY),
                  pl.BlockSpec((None, d, de), lambda j, e, u: (e[j], 0, 0)),
                  pl.BlockSpec((None, d, de), lambda j, e, u: (e[j], 0, 0)),
                  pl.BlockSpec((None, de, d), lambda j, e, u: (e[j], 0, 0))],
        out_specs=pl.BlockSpec((bm, d), lambda j, e, u: (j, 0)),
        scratch_shapes=[pltpu.VMEM((2, bm, d), F32), pltpu.VMEM((d, de), BF16), pltpu.VMEM((d, de), BF16),
                        pltpu.VMEM((de, d), BF16), pltpu.SemaphoreType.DMA((2,))],
    )
    return pl.pallas_call(
        _expert_kernel,
        grid_spec=grid_spec,
        out_shape=jax.ShapeDtypeStruct((n_blocks * bm, d), F32),
        compiler_params=_cparams("arbitrary"),
        name="experts",
    )(blk_e, blk_used, src, src, v, w1, w3, w2)


def _combined_tile(slot_ref, slot_next_ref, h_ref, meta_ref, modp_ref, y_hbm, buf, sem, overlap_with):
    i = pl.program_id(0)
    n = pl.num_programs(0)

    @pl.when(i == 0)
    def _():
        for k in range(TOP_K):
            _row_copies_start(y_hbm, slot_ref, k, buf.at[0, k], sem.at[0])

    b = i % 2
    for k in range(TOP_K):
        _row_copies_wait(y_hbm, buf.at[b, k], sem.at[b])
    meta = meta_ref[...]
    f = meta[:, META_GATE:META_GATE + 1] * buf[b, 0] + meta[:, META_GATE + 1:META_GATE + 2] * buf[b, 1]
    hn = h_ref[...] + modp_ref[5] * f
    for k in range(TOP_K):
        _row_copies_start(y_hbm, slot_next_ref, k, buf.at[1 - b, k], sem.at[1 - b])
    overlap_with(hn)

    @pl.when(i == n - 1)
    def _():
        for k in range(TOP_K):
            _row_copies_wait(y_hbm, buf.at[1 - b, k], sem.at[1 - b])


def _combine_mm_kernel(slot_ref, slot_next_ref, h_ref, meta_ref, modp_ref, g_ref, mod_ref, w_ref, y_hbm,
                       hn_ref, o_ref, buf, sem):
    def project(hn):
        hn_ref[...] = hn
        u = _norm_mod(hn, g_ref[...], mod_ref[0], mod_ref[1])
        o_ref[...] = _dot(u.astype(BF16), w_ref[...])

    _combined_tile(slot_ref, slot_next_ref, h_ref, meta_ref, modp_ref, y_hbm, buf, sem, project)


def _combine_norm_kernel(slot_ref, slot_next_ref, h_ref, meta_ref, modp_ref, g_ref, y_hbm, o_ref, buf, sem):
    def finish(hn):
        o_ref[...] = hn * lax.rsqrt(jnp.mean(hn * hn, axis=-1, keepdims=True) + EPS) * g_ref[...]

    _combined_tile(slot_ref, slot_next_ref, h_ref, meta_ref, modp_ref, y_hbm, buf, sem, finish)


def _combine_then(h, ys, slot_tiles, meta, mods_prev, rows, g, mods_l=None, w=None, name="combine_norm"):
    d = h.shape[1]
    n_tiles = rows.n_tiles
    slot_spec = lambda fn: pl.BlockSpec((None, TOP_K, ROW_TILE), lambda i: (fn(i), 0, 0), memory_space=pltpu.SMEM)
    mod_spec = pl.BlockSpec((None, 6, 1, d), lambda i: (rows.mod_row(i), 0, 0, 0))
    tile = lambda width: pl.BlockSpec((ROW_TILE, width), lambda i: (i, 0))
    rows_shape = lambda width: jax.ShapeDtypeStruct((n_tiles * ROW_TILE, width), F32)
    in_specs = [slot_spec(lambda i: i), slot_spec(lambda i: jnp.minimum(i + 1, n_tiles - 1)),
                tile(d), tile(LANE), mod_spec, pl.BlockSpec((1, d), lambda i: (0, 0))]
    args = [slot_tiles, slot_tiles, h, meta, mods_prev, g.reshape(1, d)]
    if w is None:
        kern, out_specs, out_shape = _combine_norm_kernel, tile(d), rows_shape(d)
    else:
        nw = w.shape[1]
        in_specs += [mod_spec, pl.BlockSpec((d, nw), lambda i: (0, 0))]
        args += [mods_l, w]
        kern, out_specs, out_shape = _combine_mm_kernel, [tile(d), tile(nw)], [rows_shape(d), rows_shape(nw)]
    return pl.pallas_call(
        kern,
        grid=(n_tiles,),
        in_specs=in_specs + [pl.BlockSpec(memory_space=pl.ANY)],
        out_specs=out_specs,
        out_shape=out_shape,
        scratch_shapes=[pltpu.VMEM((2, TOP_K, ROW_TILE, d), F32), pltpu.SemaphoreType.DMA((2,))],
        compiler_params=_cparams("arbitrary"),
        name=name,
    )(*args, ys)


def _even_w_in(w):
    d = w.shape[0]
    n_main = 2 * GLA_QK + 2 * GLA_V
    ranks = w[:, n_main:n_main + 2 * GLA_RANK]
    hy = w[:, n_main + 2 * GLA_RANK:]
    pad = jnp.zeros((d, R_COLS - 2 * GLA_RANK), w.dtype)
    return jnp.concatenate([w[:, :n_main], hy, ranks, pad], axis=1).astype(BF16)


def _rank_proj(wa, first_row):
    return jnp.zeros((R_COLS, GLA_QK), F32).at[first_row:first_row + GLA_RANK].set(wa)


def _grid_transpose(h_lat, batch, a, b):
    d = h_lat.shape[1]
    return h_lat.reshape(batch, a, b, d).transpose(0, 2, 1, 3).reshape(-1, d)


def kernel(x, c, ctx, c_ctx, mod_w, mod_b, norm_mix, norm_ffn, norm_final, ev_w_in, ev_w_out, gla_wa_f, gla_ba_f, gla_wa_b, gla_ba_b, gla_norm, hy_conv_w, hy_conv_b, hy_w1, hy_b1, hy_f1, hy_w2, hy_b2, hy_f2, hy_w3, hy_skip, od_w_in, od_conv_w, od_w_out, rt_w_grp, rt_b_grp, rt_w_exp, rt_b_exp, ex_w1, ex_w3, ex_w2):
    batch, s, d = x.shape
    lc = ctx.shape[1]
    depth = mod_w.shape[0]
    tc, tl = batch * lc, batch * s
    assert lc % ROW_TILE == 0 and s % ROW_TILE == 0 and tc % s == 0 and s % GRID_W == 0
    assert lc % GLA_CHUNK == 0 and s % GLA_CHUNK == 0
    assert depth % 2 == 0
    grid_rows = s // GRID_W

    n_cond = -(-(batch + 1) // 8) * 8
    cond = jnp.concatenate([c, c_ctx[None], jnp.zeros((n_cond - batch - 1, d), F32)], axis=0)
    mods = _mods(cond, mod_w, mod_b).reshape(depth, n_cond, 6, 1, d)

    h = jnp.concatenate([ctx.reshape(tc, d), x.reshape(tl, d)], axis=0)
    col_major_now = False
    moe = None

    def lat_part(a, fn):
        return fn(a) if a.shape[0] == tl else jnp.concatenate([a[:tc], fn(a[tc:])], axis=0)

    for l in range(depth):
        i = l // 2
        even = l % 2 == 0
        ctx_out = l < depth - 1
        col_major = i % 2 == 1
        per_token = [h] if moe is None else [h, moe[1], moe[2]]
        if not ctx_out:
            per_token = [a[tc:] for a in per_token]
        if col_major != col_major_now:
            perm = ((lambda a: _grid_transpose(a, batch, grid_rows, GRID_W)) if col_major
                    else (lambda a: _grid_transpose(a, batch, GRID_W, grid_rows)))
            per_token = [lat_part(a, perm) for a in per_token]
            col_major_now = col_major
        h = per_token[0]
        arr_rows = _Rows(batch, lc, s, h.shape[0] == tl)
        mods_l = mods[l]
        w_in = _even_w_in(ev_w_in[i]) if even else od_w_in[i].astype(BF16)
        name = "even_in" if even else "odd_in"
        if moe is None:
            p = _normmod_mm(h, norm_mix[l], mods_l, w_in, arr_rows, 0, 1, name)
        else:
            h, p = _combine_then(h, moe[0], _slot_tiles(per_token[1]), per_token[2], moe[3], arr_rows,
                                 norm_mix[l], mods_l, w_in, name)

        if even:
            waf = _rank_proj(gla_wa_f[i], 0)
            wab = _rank_proj(gla_wa_b[i], GLA_RANK)
            baf, bab = gla_ba_f[i].reshape(1, -1), gla_ba_b[i].reshape(1, -1)
            gain = gla_norm[i].reshape(1, -1)
            zeros = jnp.zeros((batch, 2, 2 * GLA_DV, 2 * GLA_DK), F32)
            filt_args = (hy_w1[i], hy_b1[i], hy_f1[i], hy_w2[i], hy_b2[i], hy_f2[i], hy_w3[i])
            x0c, z = _hyena_pre(p, hy_conv_w[i], hy_conv_b[i], arr_rows)
            gla_c, sc_f, sc_b = _gla(p, waf, wab, baf, bab, gain, zeros, zeros, batch, lc, 0)
            gla_l, _, _ = _gla(p, waf, wab, baf, bab, gain, sc_f, sc_b, batch, s, tc // s)
            kt_c = _hyena_ktab(_hyena_filters(lc, *filt_args), lc)
            kt_l = _hyena_ktab(_hyena_filters(s, *filt_args), s)
            hy_c = _hyena_conv(z, x0c, kt_c, hy_skip[i], batch, lc, 0)
            hy_l = _hyena_conv(z, x0c, kt_l, hy_skip[i], batch, s, tc // s)
            h = _mix_out(gla_c, gla_l, hy_c, hy_l, ev_w_out[i].astype(BF16), h, mods_l, arr_rows)
        else:
            h = _shortconv_out(p, od_conv_w[i], od_w_out[i].astype(BF16), h, mods_l, arr_rows)

        w_rt = jnp.concatenate([rt_w_grp[l], rt_w_exp[l],
                                jnp.zeros((d, LANE - N_GROUPS - N_EXPERTS), F32)], axis=1)
        b_rt = jnp.concatenate([rt_b_grp[l], rt_b_exp[l],
                                jnp.zeros((LANE - N_GROUPS - N_EXPERTS,), F32)]).reshape(1, LANE)
        v, meta, counts = _router(h, norm_ffn[l], mods_l, w_rt, b_rt, arr_rows)
        slot, src, blk_e, blk_used = _dispatch_plan(meta, counts)
        ys = _experts(v, src, blk_e, blk_used, ex_w1[l], ex_w3[l], ex_w2[l])
        moe = (ys, slot, meta, mods_l)

    out = _combine_then(h, moe[0], _slot_tiles(moe[1]), moe[2], moe[3], arr_rows, norm_final)
    if col_major_now:
        out = _grid_transpose(out, batch, GRID_W, grid_rows)
    return out.reshape(batch, s, d)
```

```python
import functools
import math

import numpy as np
import jax
import jax.numpy as jnp
from jax import lax
from jax.experimental import pallas as pl
from jax.experimental.pallas import tpu as pltpu

F32 = jnp.float32
BF16 = jnp.bfloat16

EPS = 1e-6
GRID_W = 64

GLA_HEADS = 4
GLA_DK = 64
GLA_DV = 128
GLA_RANK = 16
GLA_TAU = 16.0
GLA_CHUNK = 64
GLA_SLAB = 256
GLA_STEP_UNROLL = 4
GLA_QK = GLA_HEADS * GLA_DK
GLA_V = GLA_HEADS * GLA_DV

HY_WIDTH = 512
HY_EMB = 33
HY_BANDS = (HY_EMB - 1) // 2
HY_HIDDEN = 64
HY_FAST_DECAY = 0.3
HY_SLOW_DECAY = 1.5
HY_TARGET = 1e-2

N_GROUPS = 4
EXP_PER_GROUP = 8
N_EXPERTS = N_GROUPS * EXP_PER_GROUP
TOP_K = 2

LANE = 128
ROW_TILE = 256
MOE_ROWS = 256
R_COLS = LANE
VMEM_LIMIT = 56 * 1024 * 1024


def _cparams(*sem):
    return pltpu.CompilerParams(dimension_semantics=sem, vmem_limit_bytes=VMEM_LIMIT)


def _split_bf16(a):
    hi = a.astype(BF16)
    lo = (a - hi.astype(F32)).astype(BF16)
    return hi, lo


def _dot(a, b):
    return jnp.dot(a, b, preferred_element_type=F32)


def _dot_nt(a, b):
    return lax.dot_general(a, b, (((1,), (1,)), ((), ())), preferred_element_type=F32)


def _dot_tn(a, b):
    return lax.dot_general(a, b, (((0,), (0,)), ((), ())), preferred_element_type=F32)


def _dot3(a, b):
    ah, al = _split_bf16(a)
    bh, bl = _split_bf16(b)
    return _dot(ah, bh) + _dot(ah, bl) + _dot(al, bh)


def _silu(x):
    return x / (1.0 + jnp.exp(-x))


def _log_sigmoid(x):
    return jnp.minimum(x, 0.0) - jnp.log1p(jnp.exp(-jnp.abs(x)))


def _norm_mod(x, g, shift, scale):
    y = x * lax.rsqrt(jnp.mean(x * x, axis=-1, keepdims=True) + EPS)
    return (y * g) * (1.0 + scale) + shift


def _mods_kernel(s_ref, w_ref, b_ref, o_ref):
    s = s_ref[...]
    s = _silu(s)
    o_ref[...] = _dot3(s, w_ref[...]) + b_ref[...]


def _mods(cond, mod_w, mod_b):
    depth, d, n = mod_w.shape
    r = cond.shape[0]
    tn = 1024
    return pl.pallas_call(
        _mods_kernel,
        grid=(depth, n // tn),
        in_specs=[pl.BlockSpec((r, d), lambda l, j: (0, 0)),
                  pl.BlockSpec((None, d, tn), lambda l, j: (l, 0, j)),
                  pl.BlockSpec((None, 1, tn), lambda l, j: (l, 0, j))],
        out_specs=pl.BlockSpec((None, r, tn), lambda l, j: (l, 0, j)),
        out_shape=jax.ShapeDtypeStruct((depth, r, n), F32),
        compiler_params=_cparams("parallel", "parallel"),
        name="mods",
    )(cond, mod_w, mod_b.reshape(depth, 1, n))


class _Rows:
    def __init__(self, batch, lc, s, lat_only):
        self.batch = batch
        self.nct = 0 if lat_only else batch * lc // ROW_TILE
        self.tps = s // ROW_TILE
        self.tpc = lc // ROW_TILE
        self.n_tiles = self.nct + batch * self.tps

    def mod_row(self, i):
        return jnp.where(i < self.nct, self.batch, (i - self.nct) // self.tps)

    def seq_edges(self, i):
        pos_c = i % self.tpc
        pos_l = (i - self.nct) % self.tps
        is_c = i < self.nct
        first = jnp.where(is_c, pos_c == 0, pos_l == 0)
        last = jnp.where(is_c, pos_c == self.tpc - 1, pos_l == self.tps - 1)
        return first, last


def _normmod_mm_kernel(h_ref, g_ref, mod_ref, w_ref, o_ref, *, shift_i, scale_i):
    u = _norm_mod(h_ref[...], g_ref[...], mod_ref[shift_i], mod_ref[scale_i])
    o_ref[...] = _dot(u.astype(BF16), w_ref[...])


def _normmod_mm(h, g, mods_l, w, rows, shift_i, scale_i, name):
    d = h.shape[1]
    n = w.shape[1]
    return pl.pallas_call(
        functools.partial(_normmod_mm_kernel, shift_i=shift_i, scale_i=scale_i),
        grid=(rows.n_tiles,),
        in_specs=[pl.BlockSpec((ROW_TILE, d), lambda i: (i, 0)),
                  pl.BlockSpec((1, d), lambda i: (0, 0)),
                  pl.BlockSpec((None, 6, 1, d), lambda i: (rows.mod_row(i), 0, 0, 0)),
                  pl.BlockSpec((d, n), lambda i: (0, 0))],
        out_specs=pl.BlockSpec((ROW_TILE, n), lambda i: (i, 0)),
        out_shape=jax.ShapeDtypeStruct((rows.n_tiles * ROW_TILE, n), F32),
        compiler_params=_cparams("parallel"),
        name=name,
    )(h, g.reshape(1, d), mods_l, w)


def _gla_kernel(q_ref, k_ref, v_ref, g_ref, r_ref, waf_ref, wab_ref, baf_ref, bab_ref, gain_ref,
                s0f_ref, s0b_ref, o_ref, sf_ref, sb_ref, gl_scr, tot_scr, qd_scr, ds_scr, o_scr, st_scr,
                *, seq_len):
    C, SL = GLA_CHUNK, GLA_SLAB
    cps = SL // C
    n_chunks, n_slabs = seq_len // C, seq_len // SL
    head_of_lane = lax.broadcasted_iota(jnp.int32, (SL, 2 * GLA_DK), 1) // GLA_DK
    row = lax.broadcasted_iota(jnp.int32, (SL, SL), 0)
    col = lax.broadcasted_iota(jnp.int32, (SL, SL), 1)
    same_chunk = (row // C) == (col // C)
    srow = lax.broadcasted_iota(jnp.int32, (2 * GLA_DV, 2 * GLA_DK), 0) // GLA_DV
    scol = lax.broadcasted_iota(jnp.int32, (2 * GLA_DV, 2 * GLA_DK), 1) // GLA_DK
    same_head = srow == scol
    r = r_ref[...]

    def direction(wa_ref, ba_ref, s0_ref, s_out_ref, forward):
        mask = jnp.logical_and(same_chunk, (row >= col) if forward else (row <= col))
        tri = jnp.where(mask, 1.0, 0.0).astype(BF16)
        gl_scr[...] = _log_sigmoid(_dot3(r, wa_ref[...]) + ba_ref[...]) * (1.0 / GLA_TAU)

        def slab(s, carry):
            rows = pl.ds(pl.multiple_of(s * SL, SL), SL)
            gl = gl_scr[rows, :]
            g_hi = gl.astype(BF16)
            g_r1 = gl - g_hi.astype(F32)
            g_mid = g_r1.astype(BF16)
            g_lo = (g_r1 - g_mid.astype(F32)).astype(BF16)
            b = _dot(tri, g_hi) + _dot(tri, g_mid) + _dot(tri, g_lo)
            b3 = b.reshape(cps, C, 2 * GLA_DK)
            last = b3[:, C - 1:C, :] if forward else b3[:, 0:1, :]
            tot = jnp.broadcast_to(last, b3.shape).reshape(SL, 2 * GLA_DK)
            tot_scr[rows, :] = tot
            q = q_ref[rows, :] * (GLA_DK ** -0.5)
            k = k_ref[rows, :]
            v = v_ref[rows, :].astype(BF16)
            qd = q * jnp.exp(b)
            kd = (k * jnp.exp(-b)).astype(BF16)
            kr = (k * jnp.exp(tot - b)).astype(BF16)
            qd_scr[rows, :] = qd.astype(BF16)
            o_parts = []
            for h in range(2):
                qh = jnp.where(head_of_lane == h, qd, 0.0).astype(BF16)
                a = jnp.where(mask, _dot_nt(qh, kd), 0.0).astype(BF16)
                o_parts.append(_dot(a, v[:, h * GLA_DV:(h + 1) * GLA_DV]))
            o = jnp.concatenate(o_parts, axis=1)
            if forward:
                o_scr[rows, :] = o
            else:
                o_scr[rows, :] += o
            for c in range(cps):
                ds = _dot_tn(v[c * C:(c + 1) * C], kr[c * C:(c + 1) * C])
                ds_scr[s * cps + c] = jnp.where(same_head, ds, 0.0)
            return carry

        lax.fori_loop(0, n_slabs, slab, 0, unroll=2)
        st_scr[...] = s0_ref[...]

        def step(i, carry):
            c = i if forward else n_chunks - 1 - i
            first = pl.multiple_of(c * C, C)
            rows = pl.ds(first, C)
            st = st_scr[...]
            o_scr[rows, :] += _dot_nt(qd_scr[rows, :], st.astype(BF16))
            st_scr[...] = st * jnp.exp(tot_scr[pl.ds(first, 1), :]) + ds_scr[c]
            return carry

        lax.fori_loop(0, n_chunks, step, 0, unroll=GLA_STEP_UNROLL)
        s_out_ref[...] = st_scr[...]

    direction(waf_ref, baf_ref, s0f_ref, sf_ref, True)
    direction(wab_ref, bab_ref, s0b_ref, sb_ref, False)
    gain = gain_ref[...]

    def readout(s, carry):
        rows = pl.ds(pl.multiple_of(s * SL, SL), SL)
        o = o_scr[rows, :]
        outs = []
        for h in range(2):
            oh = o[:, h * GLA_DV:(h + 1) * GLA_DV]
            outs.append(oh * lax.rsqrt(jnp.mean(oh * oh, axis=-1, keepdims=True) + EPS) * gain)
        o_ref[rows, :] = jnp.concatenate(outs, axis=1) * _silu(g_ref[rows, :])
        return carry

    lax.fori_loop(0, n_slabs, readout, 0)


def _gla(p, waf, wab, baf, bab, gain, s0f, s0b, batch, seq_len, blk0):
    dk2, dv2 = 2 * GLA_DK, 2 * GLA_DV
    seq = lambda width, cb: pl.BlockSpec((seq_len, width), lambda b, hp: (blk0 + b, cb(hp)))
    state_spec = pl.BlockSpec((None, None, dv2, dk2), lambda b, hp: (b, hp, 0, 0))
    in_specs = [seq(dk2, lambda hp: hp),
                seq(dk2, lambda hp: GLA_QK // dk2 + hp),
                seq(dv2, lambda hp: 2 * GLA_QK // dv2 + hp),
                seq(dv2, lambda hp: (2 * GLA_QK + GLA_V) // dv2 + hp),
                seq(R_COLS, lambda hp: (2 * GLA_QK + 2 * GLA_V + 3 * HY_WIDTH) // R_COLS),
                pl.BlockSpec((R_COLS, dk2), lambda b, hp: (0, hp)),
                pl.BlockSpec((R_COLS, dk2), lambda b, hp: (0, hp)),
                pl.BlockSpec((1, dk2), lambda b, hp: (0, hp)),
                pl.BlockSpec((1, dk2), lambda b, hp: (0, hp)),
                pl.BlockSpec((1, GLA_DV), lambda b, hp: (0, 0)),
                state_spec, state_spec]
    state_shape = jax.ShapeDtypeStruct((batch, 2, dv2, dk2), F32)
    return pl.pallas_call(
        functools.partial(_gla_kernel, seq_len=seq_len),
        grid=(batch, 2),
        in_specs=in_specs,
        out_specs=[pl.BlockSpec((seq_len, dv2), lambda b, hp: (b, hp)), state_spec, state_spec],
        out_shape=[jax.ShapeDtypeStruct((batch * seq_len, GLA_V), F32), state_shape, state_shape],
        scratch_shapes=[pltpu.VMEM((seq_len, dk2), F32),
                        pltpu.VMEM((seq_len, dk2), F32),
                        pltpu.VMEM((seq_len, dk2), BF16),
                        pltpu.VMEM((seq_len // GLA_CHUNK, dv2, dk2), F32),
                        pltpu.VMEM((seq_len, dv2), F32),
                        pltpu.VMEM((dv2, dk2), F32)],
        compiler_params=_cparams("parallel", "parallel"),
        name="gla",
    )(p, p, p, p, p, waf, wab, baf, bab, gain, s0f, s0b)


def _conv3(m, prev_row, next_row, w_ref):
    n = m.shape[0]
    ridx = lax.broadcasted_iota(jnp.int32, m.shape, 0)
    m_prev = jnp.where(ridx == 0, prev_row, pltpu.roll(m, 1, 0))
    m_next = jnp.where(ridx == n - 1, next_row, pltpu.roll(m, n - 1, 0))
    return w_ref[0:1, :] * m_prev + w_ref[1:2, :] * m + w_ref[2:3, :] * m_next


def _halo_specs(width, col_block, t_rows):
    g = ROW_TILE // 8
    last = t_rows // 8 - 1
    prev = pl.BlockSpec((8, width), lambda i: (jnp.maximum(i * g - 1, 0), col_block))
    nxt = pl.BlockSpec((8, width), lambda i: (jnp.minimum((i + 1) * g, last), col_block))
    return prev, nxt


def _hyena_pre_kernel(x0_ref, x1_ref, v_ref, x0p_ref, x0n_ref, x1p_ref, x1n_ref, vp_ref, vn_ref,
                      w_ref, b_ref, x0c_ref, z_ref, *, rows):
    first, last = rows.seq_edges(pl.program_id(0))
    keep_p = jnp.where(first, 0.0, 1.0)
    keep_n = jnp.where(last, 0.0, 1.0)
    hw = HY_WIDTH

    def conv(ref, p_ref, n_ref, j):
        w = w_ref.at[:, j * hw:(j + 1) * hw]
        y = _conv3(ref[...], p_ref[7:8, :] * keep_p, n_ref[0:1, :] * keep_n, w)
        return y + b_ref[:, j * hw:(j + 1) * hw]

    x0c_ref[...] = conv(x0_ref, x0p_ref, x0n_ref, 0)
    z_ref[...] = conv(v_ref, vp_ref, vn_ref, 2) * conv(x1_ref, x1p_ref, x1n_ref, 1)


def _hyena_pre(p, conv_w, conv_b, rows):
    t_rows = p.shape[0]
    hw = HY_WIDTH
    cb0 = (2 * GLA_QK + 2 * GLA_V) // hw
    in_specs = [pl.BlockSpec((ROW_TILE, hw), lambda i, j=j: (i, cb0 + j)) for j in range(3)]
    for j in range(3):
        in_specs.extend(_halo_specs(hw, cb0 + j, t_rows))
    in_specs += [pl.BlockSpec((3, 3 * hw), lambda i: (0, 0)), pl.BlockSpec((1, 3 * hw), lambda i: (0, 0))]
    out_spec = pl.BlockSpec((ROW_TILE, hw), lambda i: (i, 0))
    shape = jax.ShapeDtypeStruct((rows.n_tiles * ROW_TILE, hw), F32)
    return pl.pallas_call(
        functools.partial(_hyena_pre_kernel, rows=rows),
        grid=(rows.n_tiles,),
        in_specs=in_specs,
        out_specs=[out_spec, out_spec],
        out_shape=[shape, shape],
        compiler_params=_cparams("parallel"),
        name="hyena_pre",
    )(p, p, p, p, p, p, p, p, p, conv_w, conv_b.reshape(1, 3 * hw))


def _filter_kernel(z_ref, w1_ref, b1_ref, f1_ref, w2_ref, b2_ref, f2_ref, w3_ref, win_ref, o_ref):
    hh = jnp.sin(f1_ref[...] * (_dot3(z_ref[...], w1_ref[...]) + b1_ref[...]))
    hh = jnp.sin(f2_ref[...] * (_dot3(hh, w2_ref[...]) + b2_ref[...]))
    win = win_ref[...]
    o_ref[...] = _dot3(hh, w3_ref[...]) * jnp.concatenate([win, win], axis=1)


@functools.lru_cache(maxsize=None)
def _filter_features(L):
    t = np.linspace(0.0, 1.0, L, dtype=np.float32)[:, None]
    pos = np.arange(L, dtype=np.float32)[:, None]
    bands = np.linspace(1e-4, HY_BANDS - 1, HY_BANDS, dtype=np.float32)[None]
    ang = (np.float32(2.0 * math.pi / L) * pos * bands).astype(np.float32)
    z = np.concatenate([t, np.cos(ang), np.sin(ang)], axis=-1).astype(np.float32)
    z = np.pad(z, ((0, 0), (0, LANE - HY_EMB)))
    max_decay = math.log(HY_TARGET) / HY_FAST_DECAY
    min_decay = math.log(HY_TARGET) / HY_SLOW_DECAY
    deltas = np.linspace(min_decay, max_decay, HY_WIDTH, dtype=np.float32)
    window = np.exp(-t * np.abs(deltas)[None]).astype(np.float32)
    return z, window


def _hyena_filters(L, w1, b1, f1, w2, b2, f2, w3):
    z, window = _filter_features(L)
    w1p = jnp.pad(w1, ((0, LANE - HY_EMB), (0, 0)))
    tl = min(L, 512)
    full = lambda a: pl.BlockSpec(a.shape, lambda i: (0,) * a.ndim)
    row = lambda a: a.reshape(1, -1)
    ops = [w1p, row(b1), row(f1), w2, row(b2), row(f2), w3]
    return pl.pallas_call(
        _filter_kernel,
        grid=(L // tl,),
        in_specs=[pl.BlockSpec((tl, LANE), lambda i: (i, 0))] + [full(a) for a in ops]
                 + [pl.BlockSpec((tl, HY_WIDTH), lambda i: (i, 0))],
        out_specs=pl.BlockSpec((tl, 2 * HY_WIDTH), lambda i: (i, 0)),
        out_shape=jax.ShapeDtypeStruct((L, 2 * HY_WIDTH), F32),
        compiler_params=_cparams("parallel"),
        name="hyena_filters",
    )(jnp.asarray(z), *ops, jnp.asarray(window))


def _freq_tile(L):
    return min(2 * L, 512)


@functools.lru_cache(maxsize=None)
def _dft_tables(L):
    n = 2 * L
    tf = _freq_tile(L)
    half = tf // 2
    t = np.arange(L, dtype=np.int64)[None, :]
    fm = np.zeros((n, L), np.float64)
    scale = np.zeros((n, 1), np.float64)
    sign = np.zeros((n, 1), np.float64)
    for j in range(n // tf):
        k = (np.arange(half, dtype=np.int64) + j * half)[:, None]
        ang = 2.0 * np.pi * ((k * t) % n).astype(np.float64) / n
        fm[j * tf:j * tf + half] = np.cos(ang)
        fm[j * tf + half:(j + 1) * tf] = -np.sin(ang)
        scale[j * tf:(j + 1) * tf] = 2.0 / n
        sign[j * tf:j * tf + half] = 1.0
        sign[j * tf + half:(j + 1) * tf] = -1.0
    fm[half] = np.cos(np.pi * t[0])
    scale[0] = 1.0 / n
    scale[half] = 1.0 / n
    sign[half] = 1.0
    return (fm.astype(np.float32), np.ascontiguousarray(fm.T).astype(np.float32),
            scale.astype(np.float32), sign.astype(np.float32))


def _ktab_kernel(f_ref, h_ref, scale_ref, sign_ref, o_ref):
    hw = HY_WIDTH
    hh, hl = _split_bf16(h_ref[...])
    f = f_ref[...]
    kk = _dot(f, hh) + _dot(f, hl)
    o_ref[...] = scale_ref[...] * (kk[:, :hw] + sign_ref[...] * kk[:, hw:])


def _hyena_ktab(filt, L):
    fm, _, scale, sign = _dft_tables(L)
    n = 2 * L
    tf = _freq_tile(L)
    return pl.pallas_call(
        _ktab_kernel,
        grid=(n // tf,),
        in_specs=[pl.BlockSpec((tf, L), lambda j: (j, 0)),
                  pl.BlockSpec((L, 2 * HY_WIDTH), lambda j: (0, 0)),
                  pl.BlockSpec((tf, 1), lambda j: (j, 0)),
                  pl.BlockSpec((tf, 1), lambda j: (j, 0))],
        out_specs=pl.BlockSpec((tf, HY_WIDTH), lambda j: (j, 0)),
        out_shape=jax.ShapeDtypeStruct((n, HY_WIDTH), F32),
        compiler_params=_cparams("parallel"),
        name="hyena_ktab",
    )(jnp.asarray(fm, dtype=BF16), filt, jnp.asarray(scale), jnp.asarray(sign))


def _hyena_conv_kernel(z_ref, x0_ref, f_ref, ft_ref, k_ref, skip_ref, o_ref, zb_scr, acc_scr, *, tf):
    j = pl.program_id(1)
    half = tf // 2

    @pl.when(j == 0)
    def _():
        zb_scr[...] = z_ref[...].astype(BF16)
        acc_scr[...] = jnp.zeros_like(acc_scr)

    zf = _dot(f_ref[...], zb_scr[...])
    re, im = zf[:half], zf[half:]
    kre, kim = k_ref[:half, :], k_ref[half:, :]
    ridx = lax.broadcasted_iota(jnp.int32, re.shape, 0)
    mix = jnp.where(jnp.logical_and(j == 0, ridx == 0), 0.0, 1.0)
    yre = re * kre - mix * (im * kim)
    yim = mix * (re * kim) + im * jnp.where(mix == 0.0, kim, kre)
    y = jnp.concatenate([yre, yim], axis=0).astype(BF16)
    acc_scr[...] += _dot(ft_ref[...], y)

    @pl.when(j == pl.num_programs(1) - 1)
    def _():
        o_ref[...] = x0_ref[...] * (acc_scr[...] + z_ref[...] * skip_ref[...])


def _hyena_conv(z, x0c, ktab, skip, batch, L, blk0):
    fm, fmt, _, _ = _dft_tables(L)
    n = 2 * L
    tf = _freq_tile(L)
    hw = HY_WIDTH
    return pl.pallas_call(
        functools.partial(_hyena_conv_kernel, tf=tf),
        grid=(batch, n // tf),
        in_specs=[pl.BlockSpec((L, hw), lambda b, j: (blk0 + b, 0)),
                  pl.BlockSpec((L, hw), lambda b, j: (blk0 + b, 0)),
                  pl.BlockSpec((tf, L), lambda b, j: (j, 0)),
                  pl.BlockSpec((L, tf), lambda b, j: (0, j)),
                  pl.BlockSpec((tf, hw), lambda b, j: (j, 0)),
                  pl.BlockSpec((1, hw), lambda b, j: (0, 0))],
        out_specs=pl.BlockSpec((L, hw), lambda b, j: (b, 0)),
        out_shape=jax.ShapeDtypeStruct((batch * L, hw), F32),
        scratch_shapes=[pltpu.VMEM((L, hw), BF16), pltpu.VMEM((L, hw), F32)],
        compiler_params=_cparams("parallel", "arbitrary"),
        name="hyena_conv",
    )(z, x0c, jnp.asarray(fm, dtype=BF16), jnp.asarray(fmt, dtype=BF16), ktab, skip.reshape(1, hw))


def _mix_out_kernel(ac_ref, al_ref, bc_ref, bl_ref, wa_ref, wb_ref, h_ref, mod_ref, o_ref, *, nct):
    is_ctx = pl.program_id(0) < nct
    a = jnp.where(is_ctx, ac_ref[...], al_ref[...])
    b = jnp.where(is_ctx, bc_ref[...], bl_ref[...])
    y = _dot(a.astype(BF16), wa_ref[...]) + _dot(b.astype(BF16), wb_ref[...])
    o_ref[...] = h_ref[...] + mod_ref[2] * y


def _mix_out(a_ctx, a_lat, b_ctx, b_lat, w_out, h, mods_l, rows):
    d = h.shape[1]
    ka, kb = a_ctx.shape[1], b_ctx.shape[1]
    nct = rows.nct
    ctx_spec = lambda k: pl.BlockSpec((ROW_TILE, k), lambda i: (jnp.minimum(i, nct - 1), 0))
    lat_spec = lambda k: pl.BlockSpec((ROW_TILE, k), lambda i: (jnp.maximum(i - nct, 0), 0))
    return pl.pallas_call(
        functools.partial(_mix_out_kernel, nct=nct),
        grid=(rows.n_tiles,),
        in_specs=[ctx_spec(ka), lat_spec(ka), ctx_spec(kb), lat_spec(kb),
                  pl.BlockSpec((ka, d), lambda i: (0, 0)),
                  pl.BlockSpec((kb, d), lambda i: (ka // kb, 0)),
                  pl.BlockSpec((ROW_TILE, d), lambda i: (i, 0)),
                  pl.BlockSpec((None, 6, 1, d), lambda i: (rows.mod_row(i), 0, 0, 0))],
        out_specs=pl.BlockSpec((ROW_TILE, d), lambda i: (i, 0)),
        out_shape=jax.ShapeDtypeStruct((rows.n_tiles * ROW_TILE, d), F32),
        compiler_params=_cparams("parallel"),
        name="mix_out",
    )(a_ctx, a_lat, b_ctx, b_lat, w_out, w_out, h, mods_l)


def _shortconv_out_kernel(bg_ref, cg_ref, xi_ref, cgp_ref, cgn_ref, xip_ref, xin_ref, cw_ref, w_ref,
                          h_ref, mod_ref, o_ref, *, rows):
    first, last = rows.seq_edges(pl.program_id(0))
    keep_p = jnp.where(first, 0.0, 1.0)
    keep_n = jnp.where(last, 0.0, 1.0)
    m = cg_ref[...] * xi_ref[...]
    m_prev = cgp_ref[7:8, :] * xip_ref[7:8, :] * keep_p
    m_next = cgn_ref[0:1, :] * xin_ref[0:1, :] * keep_n
    y = bg_ref[...] * _conv3(m, m_prev, m_next, cw_ref)
    o_ref[...] = h_ref[...] + mod_ref[2] * _dot(y.astype(BF16), w_ref[...])


def _shortconv_out(p, conv_w, w_out, h, mods_l, rows):
    d = h.shape[1]
    t_rows = p.shape[0]
    in_specs = [pl.BlockSpec((ROW_TILE, d), lambda i, j=j: (i, j)) for j in range(3)]
    in_specs += [*_halo_specs(d, 1, t_rows), *_halo_specs(d, 2, t_rows),
                 pl.BlockSpec((3, d), lambda i: (0, 0)),
                 pl.BlockSpec((d, d), lambda i: (0, 0)),
                 pl.BlockSpec((ROW_TILE, d), lambda i: (i, 0)),
                 pl.BlockSpec((None, 6, 1, d), lambda i: (rows.mod_row(i), 0, 0, 0))]
    return pl.pallas_call(
        functools.partial(_shortconv_out_kernel, rows=rows),
        grid=(rows.n_tiles,),
        in_specs=in_specs,
        out_specs=pl.BlockSpec((ROW_TILE, d), lambda i: (i, 0)),
        out_shape=jax.ShapeDtypeStruct((rows.n_tiles * ROW_TILE, d), F32),
        compiler_params=_cparams("parallel"),
        name="shortconv_out",
    )(p, p, p, p, p, p, p, conv_w, w_out, h, mods_l)


META_E, META_RANK, META_GATE = 0, 2, 4


def _lane_min_index(mask, lane_f):
    return jnp.min(jnp.where(mask, lane_f, float(LANE)), axis=1, keepdims=True)


def _router_kernel(h_ref, g_ref, mod_ref, w_ref, b_ref, v_ref, meta_ref, cnt_ref, carry_scr):
    @pl.when(pl.program_id(0) == 0)
    def _():
        carry_scr[...] = jnp.zeros_like(carry_scr)

    v = _norm_mod(h_ref[...], g_ref[...], mod_ref[3], mod_ref[4])
    _store_row_tiles(v_ref, v)
    lg = _dot3(v, w_ref[...]) + b_ref[...]
    tm = lg.shape[0]
    lane = lax.broadcasted_iota(jnp.int32, lg.shape, 1)
    lane_f = lane.astype(F32)
    neg = -jnp.inf

    is_grp = lane < N_GROUPS
    lgm = jnp.where(is_grp, lg, neg)
    m_g = jnp.max(lgm, axis=1, keepdims=True)
    s_g = jnp.sum(jnp.where(is_grp, jnp.exp(lg - m_g), 0.0), axis=1, keepdims=True)
    p_g = 1.0 / s_g
    grp = _lane_min_index(lgm == m_g, lane_f)

    ex_lane = lane - N_GROUPS
    in_grp = jnp.logical_and(jnp.logical_and(ex_lane >= 0, ex_lane < N_EXPERTS),
                             (ex_lane // EXP_PER_GROUP).astype(F32) == grp)
    m_e = jnp.max(jnp.where(in_grp, lg, neg), axis=1, keepdims=True)
    ee = jnp.where(in_grp, jnp.exp(lg - m_e), 0.0)
    pe = ee / jnp.sum(ee, axis=1, keepdims=True)
    pe1 = jnp.where(in_grp, pe, -1.0)
    p1 = jnp.max(pe1, axis=1, keepdims=True)
    i1 = _lane_min_index(pe1 == p1, lane_f)
    pe2 = jnp.where(lane_f == i1, -1.0, pe1)
    p2 = jnp.max(pe2, axis=1, keepdims=True)
    i2 = _lane_min_index(pe2 == p2, lane_f)
    denom = p1 + p2
    g1 = p_g * p1 / denom
    g2 = p_g * p2 / denom
    e1 = i1 - float(N_GROUPS)
    e2 = i2 - float(N_GROUPS)

    oh1 = lane_f == e1
    oh2 = lane_f == e2
    row = lax.broadcasted_iota(jnp.int32, (tm, tm), 0)
    col = lax.broadcasted_iota(jnp.int32, (tm, tm), 1)
    earlier = jnp.where(row > col, 1.0, 0.0).astype(BF16)
    c1 = _dot(earlier, jnp.where(oh1, 1.0, 0.0).astype(BF16))
    c2 = _dot(earlier, jnp.where(oh2, 1.0, 0.0).astype(BF16))
    tot1 = jnp.sum(jnp.where(oh1, 1.0, 0.0), axis=0, keepdims=True)
    tot2 = jnp.sum(jnp.where(oh2, 1.0, 0.0), axis=0, keepdims=True)
    carry = carry_scr[...]
    r1 = jnp.sum(jnp.where(oh1, carry + c1, 0.0), axis=1, keepdims=True)
    r2 = jnp.sum(jnp.where(oh2, carry + tot1 + c2, 0.0), axis=1, keepdims=True)
    carry = carry + tot1 + tot2
    carry_scr[...] = carry
    cnt_ref[...] = carry

    meta = jnp.zeros_like(lg)
    for idx, val in ((META_E, e1), (META_E + 1, e2), (META_RANK, r1), (META_RANK + 1, r2),
                     (META_GATE, g1), (META_GATE + 1, g2)):
        meta = jnp.where(lane == idx, val, meta)
    meta_ref[...] = meta


def _router(h, g, mods_l, w_rt, b_rt, rows):
    d = h.shape[1]
    nt = rows.n_tiles * ROW_TILE
    return pl.pallas_call(
        _router_kernel,
        grid=(rows.n_tiles,),
        in_specs=[pl.BlockSpec((ROW_TILE, d), lambda i: (i, 0)),
                  pl.BlockSpec((1, d), lambda i: (0, 0)),
                  pl.BlockSpec((None, 6, 1, d), lambda i: (rows.mod_row(i), 0, 0, 0)),
                  pl.BlockSpec((d, LANE), lambda i: (0, 0)),
                  pl.BlockSpec((1, LANE), lambda i: (0, 0))],
        out_specs=[pl.BlockSpec((ROW_TILE, d // LANE, LANE), lambda i: (i, 0, 0)),
                   pl.BlockSpec((ROW_TILE, LANE), lambda i: (i, 0)),
                   pl.BlockSpec((1, LANE), lambda i: (0, 0))],
        out_shape=[jax.ShapeDtypeStruct((nt, d // LANE, LANE), F32), jax.ShapeDtypeStruct((nt, LANE), F32),
                   jax.ShapeDtypeStruct((1, LANE), F32)],
        scratch_shapes=[pltpu.VMEM((1, LANE), F32)],
        compiler_params=_cparams("arbitrary"),
        name="router",
    )(h, g.reshape(1, d), mods_l, w_rt, b_rt)


def _dispatch_plan(meta, counts):
    bm = MOE_ROWS
    t = meta.shape[0]
    n_blocks = -(-(t * TOP_K + N_EXPERTS * (bm - 1)) // bm)
    counts = counts[0, :N_EXPERTS].astype(jnp.int32)
    nblk = (counts + bm - 1) // bm
    blk_end = jnp.cumsum(nblk).astype(jnp.int32)
    slot0 = (blk_end - nblk) * bm
    expert = meta[:, META_E:META_E + TOP_K].astype(jnp.int32)
    rank = meta[:, META_RANK:META_RANK + TOP_K].astype(jnp.int32)
    onehot = expert[:, :, None] == jnp.arange(N_EXPERTS, dtype=jnp.int32)
    slot = rank + jnp.sum(jnp.where(onehot, slot0, 0), axis=-1)
    src = _slot_tokens(_slot_tiles(slot), n_blocks)
    blk = jnp.arange(n_blocks, dtype=jnp.int32)
    blk_e = jnp.minimum(jnp.sum(blk[:, None] >= blk_end[None, :], axis=1), N_EXPERTS - 1).astype(jnp.int32)
    blk_used = (blk < blk_end[-1]).astype(jnp.int32)
    return slot, src.reshape(n_blocks, 1, bm), blk_e, blk_used


def _slot_tiles(slot):
    return slot.reshape(slot.shape[0] // ROW_TILE, ROW_TILE, TOP_K).transpose(0, 2, 1)


def _slot_tokens_kernel(slot_ref, src_ref):
    i = pl.program_id(0)
    tm = slot_ref.shape[1]

    @pl.when(i == 0)
    def _():
        def clear(j, c):
            src_ref[j] = 0
            return c

        lax.fori_loop(0, src_ref.shape[0], clear, 0, unroll=16)

    base = i * tm
    for r in range(tm):
        for k in range(TOP_K):
            src_ref[slot_ref[k, r]] = base + r


def _slot_tokens(slot_tiles, n_blocks):
    n = n_blocks * MOE_ROWS
    return pl.pallas_call(
        _slot_tokens_kernel,
        grid=(slot_tiles.shape[0],),
        in_specs=[pl.BlockSpec((None, TOP_K, ROW_TILE), lambda i: (i, 0, 0), memory_space=pltpu.SMEM)],
        out_specs=pl.BlockSpec((n,), lambda i: (0,), memory_space=pltpu.SMEM),
        out_shape=jax.ShapeDtypeStruct((n,), jnp.int32),
        compiler_params=_cparams("arbitrary"),
        name="slot_tokens",
    )(slot_tiles)


def _store_row_tiles(ref, x):
    for j in range(ref.shape[1]):
        ref[:, j, :] = x[:, j * LANE:(j + 1) * LANE]


def _from_row_tiles(ref):
    return jnp.concatenate([ref[:, j, :] for j in range(ref.shape[1])], axis=1)


def _row_copies_start(src_hbm, idx_ref, k, dst, sem):
    for r in range(dst.shape[0]):
        pltpu.make_async_copy(src_hbm.at[pl.ds(idx_ref[k, r], 1)], dst.at[pl.ds(r, 1)], sem).start()


def _row_copies_wait(src_hbm, dst, sem):
    pltpu.make_async_copy(src_hbm.at[pl.ds(0, dst.shape[0])], dst, sem).wait()


def _expert_kernel(blk_e_ref, blk_used_ref, src_ref, src_next_ref, v_hbm, w1_ref, w3_ref, w2_ref, y_ref,
                   xbuf, w1_scr, w3_scr, w2_scr, sem):
    j = pl.program_id(0)
    nb = pl.num_programs(0)
    used = blk_used_ref[j] > 0
    new_expert = jnp.logical_or(j == 0, blk_e_ref[j] != blk_e_ref[jnp.maximum(j - 1, 0)])

    @pl.when(j == 0)
    def _():
        _row_copies_start(v_hbm, src_ref, 0, xbuf.at[0], sem.at[0])

    @pl.when(jnp.logical_and(used, new_expert))
    def _():
        w1_scr[...] = w1_ref[...].astype(BF16)
        w3_scr[...] = w3_ref[...].astype(BF16)
        w2_scr[...] = w2_ref[...].astype(BF16)

    @pl.when(used)
    def _():
        b = j % 2
        _row_copies_wait(v_hbm, xbuf.at[b], sem.at[b])
        x = _from_row_tiles(xbuf.at[b]).astype(BF16)
        _row_copies_start(v_hbm, src_next_ref, 0, xbuf.at[1 - b], sem.at[1 - b])
        hid = _silu(_dot(x, w1_scr[...])) * _dot(x, w3_scr[...])
        _store_row_tiles(y_ref, _dot(hid.astype(BF16), w2_scr[...]))

        @pl.when(jnp.logical_or(j == nb - 1, blk_used_ref[jnp.minimum(j + 1, nb - 1)] == 0))
        def _():
            _row_copies_wait(v_hbm, xbuf.at[1 - b], sem.at[1 - b])

    @pl.when(jnp.logical_not(used))
    def _():
        y_ref[...] = jnp.zeros_like(y_ref)


def _experts(v, src, blk_e, blk_used, w1, w3, w2):
    t, nl, _ = v.shape
    d = nl * LANE
    n_blocks = src.shape[0]
    bm = MOE_ROWS
    de = w1.shape[2]

    def next_block(j, e, used):
        nxt = jnp.minimum(j + 1, n_blocks - 1)
        return jnp.where(used[nxt] > 0, nxt, j)

    grid_spec = pltpu.PrefetchScalarGridSpec(
        num_scalar_prefetch=2,
        grid=(n_blocks,),
        in_specs=[pl.BlockSpec((None, 1, bm), lambda j, e, u: (j, 0, 0), memory_space=pltpu.SMEM),
                  pl.BlockSpec((None, 1, bm), lambda j, e, u: (next_block(j, e, u), 0, 0), memory_space=pltpu.SMEM),
                  pl.BlockSpec(memory_space=pl.ANY),
                  pl.BlockSpec((None, d, de), lambda j, e, u: (e[j], 0, 0)),
                  pl.BlockSpec((None, d, de), lambda j, e, u: (e[j], 0, 0)),
                  pl.BlockSpec((None, de, d), lambda j, e, u: (e[j], 0, 0))],
        out_specs=pl.BlockSpec((bm, nl, LANE), lambda j, e, u: (j, 0, 0)),
        scratch_shapes=[pltpu.VMEM((2, bm, nl, LANE), F32), pltpu.VMEM((d, de), BF16), pltpu.VMEM((d, de), BF16),
                        pltpu.VMEM((de, d), BF16), pltpu.SemaphoreType.DMA((2,))],
    )
    return pl.pallas_call(
        _expert_kernel,
        grid_spec=grid_spec,
        out_shape=jax.ShapeDtypeStruct((n_blocks * bm, nl, LANE), F32),
        compiler_params=_cparams("arbitrary"),
        name="experts",
    )(blk_e, blk_used, src, src, v, w1, w3, w2)


def _combined_tile(slot_ref, slot_next_ref, h_ref, meta_ref, modp_ref, y_hbm, buf, sem, overlap_with):
    i = pl.program_id(0)
    n = pl.num_programs(0)

    @pl.when(i == 0)
    def _():
        for k in range(TOP_K):
            _row_copies_start(y_hbm, slot_ref, k, buf.at[0, k], sem.at[0])

    b = i % 2
    for k in range(TOP_K):
        _row_copies_wait(y_hbm, buf.at[b, k], sem.at[b])
    meta = meta_ref[...]
    f = (meta[:, META_GATE:META_GATE + 1] * _from_row_tiles(buf.at[b, 0])
         + meta[:, META_GATE + 1:META_GATE + 2] * _from_row_tiles(buf.at[b, 1]))
    hn = h_ref[...] + modp_ref[5] * f
    for k in range(TOP_K):
        _row_copies_start(y_hbm, slot_next_ref, k, buf.at[1 - b, k], sem.at[1 - b])
    overlap_with(hn)

    @pl.when(i == n - 1)
    def _():
        for k in range(TOP_K):
            _row_copies_wait(y_hbm, buf.at[1 - b, k], sem.at[1 - b])


def _combine_mm_kernel(slot_ref, slot_next_ref, h_ref, meta_ref, modp_ref, g_ref, mod_ref, w_ref, y_hbm,
                       hn_ref, o_ref, buf, sem):
    def project(hn):
        hn_ref[...] = hn
        u = _norm_mod(hn, g_ref[...], mod_ref[0], mod_ref[1])
        o_ref[...] = _dot(u.astype(BF16), w_ref[...])

    _combined_tile(slot_ref, slot_next_ref, h_ref, meta_ref, modp_ref, y_hbm, buf, sem, project)


def _combine_norm_kernel(slot_ref, slot_next_ref, h_ref, meta_ref, modp_ref, g_ref, y_hbm, o_ref, buf, sem):
    def finish(hn):
        o_ref[...] = hn * lax.rsqrt(jnp.mean(hn * hn, axis=-1, keepdims=True) + EPS) * g_ref[...]

    _combined_tile(slot_ref, slot_next_ref, h_ref, meta_ref, modp_ref, y_hbm, buf, sem, finish)


def _combine_then(h, ys, slot_tiles, meta, mods_prev, rows, g, mods_l=None, w=None, name="combine_norm", tile0=0):
    d = h.shape[1]
    n_tiles = rows.n_tiles
    slot_spec = lambda fn: pl.BlockSpec((None, TOP_K, ROW_TILE), lambda i: (tile0 + fn(i), 0, 0),
                                        memory_space=pltpu.SMEM)
    mod_spec = pl.BlockSpec((None, 6, 1, d), lambda i: (rows.mod_row(i), 0, 0, 0))
    tile_in = lambda width: pl.BlockSpec((ROW_TILE, width), lambda i: (tile0 + i, 0))
    tile = lambda width: pl.BlockSpec((ROW_TILE, width), lambda i: (i, 0))
    rows_shape = lambda width: jax.ShapeDtypeStruct((n_tiles * ROW_TILE, width), F32)
    in_specs = [slot_spec(lambda i: i), slot_spec(lambda i: jnp.minimum(i + 1, n_tiles - 1)),
                tile_in(d), tile_in(LANE), mod_spec, pl.BlockSpec((1, d), lambda i: (0, 0))]
    args = [slot_tiles, slot_tiles, h, meta, mods_prev, g.reshape(1, d)]
    if w is None:
        kern, out_specs, out_shape = _combine_norm_kernel, tile(d), rows_shape(d)
    else:
        nw = w.shape[1]
        in_specs += [mod_spec, pl.BlockSpec((d, nw), lambda i: (0, 0))]
        args += [mods_l, w]
        kern, out_specs, out_shape = _combine_mm_kernel, [tile(d), tile(nw)], [rows_shape(d), rows_shape(nw)]
    return pl.pallas_call(
        kern,
        grid=(n_tiles,),
        in_specs=in_specs + [pl.BlockSpec(memory_space=pl.ANY)],
        out_specs=out_specs,
        out_shape=out_shape,
        scratch_shapes=[pltpu.VMEM((2, TOP_K, ROW_TILE, d // LANE, LANE), F32), pltpu.SemaphoreType.DMA((2,))],
        compiler_params=_cparams("arbitrary"),
        name=name,
    )(*args, ys)


def _even_w_in(w):
    d = w.shape[0]
    n_main = 2 * GLA_QK + 2 * GLA_V
    ranks = w[:, n_main:n_main + 2 * GLA_RANK]
    hy = w[:, n_main + 2 * GLA_RANK:]
    pad = jnp.zeros((d, R_COLS - 2 * GLA_RANK), w.dtype)
    return jnp.concatenate([w[:, :n_main], hy, ranks, pad], axis=1).astype(BF16)


def _rank_proj(wa, first_row):
    return jnp.zeros((R_COLS, GLA_QK), F32).at[first_row:first_row + GLA_RANK].set(wa)


def _grid_transpose(h_lat, batch, a, b):
    d = h_lat.shape[1]
    return h_lat.reshape(batch, a, b, d).transpose(0, 2, 1, 3).reshape(-1, d)


def kernel(x, c, ctx, c_ctx, mod_w, mod_b, norm_mix, norm_ffn, norm_final, ev_w_in, ev_w_out, gla_wa_f, gla_ba_f, gla_wa_b, gla_ba_b, gla_norm, hy_conv_w, hy_conv_b, hy_w1, hy_b1, hy_f1, hy_w2, hy_b2, hy_f2, hy_w3, hy_skip, od_w_in, od_conv_w, od_w_out, rt_w_grp, rt_b_grp, rt_w_exp, rt_b_exp, ex_w1, ex_w3, ex_w2):
    batch, s, d = x.shape
    lc = ctx.shape[1]
    depth = mod_w.shape[0]
    tc, tl = batch * lc, batch * s
    assert lc % ROW_TILE == 0 and s % ROW_TILE == 0 and tc % s == 0 and s % GRID_W == 0
    assert lc % GLA_CHUNK == 0 and s % GLA_CHUNK == 0
    assert depth % 2 == 0
    grid_rows = s // GRID_W

    n_cond = -(-(batch + 1) // 8) * 8
    cond = jnp.concatenate([c, c_ctx[None], jnp.zeros((n_cond - batch - 1, d), F32)], axis=0)
    mods = _mods(cond, mod_w, mod_b).reshape(depth, n_cond, 6, 1, d)

    h = jnp.concatenate([ctx.reshape(tc, d), x.reshape(tl, d)], axis=0)
    col_major_now = False
    moe = None

    def lat_part(a, fn):
        return fn(a) if a.shape[0] == tl else jnp.concatenate([a[:tc], fn(a[tc:])], axis=0)

    for l in range(depth):
        i = l // 2
        even = l % 2 == 0
        ctx_out = l < depth - 1
        col_major = i % 2 == 1
        per_token = [h] if moe is None else [h, moe[1], moe[2]]
        if col_major != col_major_now:
            perm = ((lambda a: _grid_transpose(a, batch, grid_rows, GRID_W)) if col_major
                    else (lambda a: _grid_transpose(a, batch, GRID_W, grid_rows)))
            per_token = [lat_part(a, perm) for a in per_token]
            col_major_now = col_major
        h = per_token[0]
        arr_rows = _Rows(batch, lc, s, not ctx_out)
        tile0 = (h.shape[0] - arr_rows.n_tiles * ROW_TILE) // ROW_TILE
        mods_l = mods[l]
        w_in = _even_w_in(ev_w_in[i]) if even else od_w_in[i].astype(BF16)
        name = "even_in" if even else "odd_in"
        if moe is None:
            p = _normmod_mm(h, norm_mix[l], mods_l, w_in, arr_rows, 0, 1, name)
        else:
            h, p = _combine_then(h, moe[0], _slot_tiles(per_token[1]), per_token[2], moe[3], arr_rows,
                                 norm_mix[l], mods_l, w_in, name, tile0)

        if even:
            waf = _rank_proj(gla_wa_f[i], 0)
            wab = _rank_proj(gla_wa_b[i], GLA_RANK)
            baf, bab = gla_ba_f[i].reshape(1, -1), gla_ba_b[i].reshape(1, -1)
            gain = gla_norm[i].reshape(1, -1)
            zeros = jnp.zeros((batch, 2, 2 * GLA_DV, 2 * GLA_DK), F32)
            filt_args = (hy_w1[i], hy_b1[i], hy_f1[i], hy_w2[i], hy_b2[i], hy_f2[i], hy_w3[i])
            x0c, z = _hyena_pre(p, hy_conv_w[i], hy_conv_b[i], arr_rows)
            gla_c, sc_f, sc_b = _gla(p, waf, wab, baf, bab, gain, zeros, zeros, batch, lc, 0)
            gla_l, _, _ = _gla(p, waf, wab, baf, bab, gain, sc_f, sc_b, batch, s, tc // s)
            kt_c = _hyena_ktab(_hyena_filters(lc, *filt_args), lc)
            kt_l = _hyena_ktab(_hyena_filters(s, *filt_args), s)
            hy_c = _hyena_conv(z, x0c, kt_c, hy_skip[i], batch, lc, 0)
            hy_l = _hyena_conv(z, x0c, kt_l, hy_skip[i], batch, s, tc // s)
            h = _mix_out(gla_c, gla_l, hy_c, hy_l, ev_w_out[i].astype(BF16), h, mods_l, arr_rows)
        else:
            h = _shortconv_out(p, od_conv_w[i], od_w_out[i].astype(BF16), h, mods_l, arr_rows)

        w_rt = jnp.concatenate([rt_w_grp[l], rt_w_exp[l],
                                jnp.zeros((d, LANE - N_GROUPS - N_EXPERTS), F32)], axis=1)
        b_rt = jnp.concatenate([rt_b_grp[l], rt_b_exp[l],
                                jnp.zeros((LANE - N_GROUPS - N_EXPERTS,), F32)]).reshape(1, LANE)
        v, meta, counts = _router(h, norm_ffn[l], mods_l, w_rt, b_rt, arr_rows)
        slot, src, blk_e, blk_used = _dispatch_plan(meta, counts)
        ys = _experts(v, src, blk_e, blk_used, ex_w1[l], ex_w3[l], ex_w2[l])
        moe = (ys, slot, meta, mods_l)

    out = _combine_then(h, moe[0], _slot_tiles(moe[1]), moe[2], moe[3], arr_rows, norm_final)
    if col_major_now:
        out = _grid_transpose(out, batch, GRID_W, grid_rows)
    return out.reshape(batch, s, d)
```

```python
import functools
import math

import numpy as np
import jax
import jax.numpy as jnp
from jax import lax
from jax.experimental import pallas as pl
from jax.experimental.pallas import tpu as pltpu

F32 = jnp.float32
BF16 = jnp.bfloat16

EPS = 1e-6
GRID_W = 64

GLA_HEADS = 4
GLA_DK = 64
GLA_DV = 128
GLA_RANK = 16
GLA_TAU = 16.0
GLA_CHUNK = 64
GLA_SLAB = 256
GLA_STEP_UNROLL = 4
GLA_QK = GLA_HEADS * GLA_DK
GLA_V = GLA_HEADS * GLA_DV

HY_WIDTH = 512
HY_EMB = 33
HY_BANDS = (HY_EMB - 1) // 2
HY_HIDDEN = 64
HY_FAST_DECAY = 0.3
HY_SLOW_DECAY = 1.5
HY_TARGET = 1e-2

N_GROUPS = 4
EXP_PER_GROUP = 8
N_EXPERTS = N_GROUPS * EXP_PER_GROUP
TOP_K = 2

LANE = 128
ROW_TILE = 256
MOE_ROWS = 256
R_COLS = LANE
VMEM_LIMIT = 56 * 1024 * 1024


def _cparams(*sem):
    return pltpu.CompilerParams(dimension_semantics=sem, vmem_limit_bytes=VMEM_LIMIT)


def _split_bf16(a):
    hi = a.astype(BF16)
    lo = (a - hi.astype(F32)).astype(BF16)
    return hi, lo


def _dot(a, b):
    return jnp.dot(a, b, preferred_element_type=F32)


def _dot_nt(a, b):
    return lax.dot_general(a, b, (((1,), (1,)), ((), ())), preferred_element_type=F32)


def _dot_tn(a, b):
    return lax.dot_general(a, b, (((0,), (0,)), ((), ())), preferred_element_type=F32)


def _dot3(a, b):
    ah, al = _split_bf16(a)
    bh, bl = _split_bf16(b)
    return _dot(ah, bh) + _dot(ah, bl) + _dot(al, bh)


def _silu(x):
    return x / (1.0 + jnp.exp(-x))


def _log_sigmoid(x):
    return jnp.minimum(x, 0.0) - jnp.log1p(jnp.exp(-jnp.abs(x)))


def _norm_mod(x, g, shift, scale):
    y = x * lax.rsqrt(jnp.mean(x * x, axis=-1, keepdims=True) + EPS)
    return (y * g) * (1.0 + scale) + shift


def _mods_kernel(s_ref, w_ref, b_ref, o_ref):
    s = s_ref[...]
    s = _silu(s)
    o_ref[...] = _dot3(s, w_ref[...]) + b_ref[...]


def _mods(cond, mod_w, mod_b):
    depth, d, n = mod_w.shape
    r = cond.shape[0]
    tn = 1024
    return pl.pallas_call(
        _mods_kernel,
        grid=(depth, n // tn),
        in_specs=[pl.BlockSpec((r, d), lambda l, j: (0, 0)),
                  pl.BlockSpec((None, d, tn), lambda l, j: (l, 0, j)),
                  pl.BlockSpec((None, 1, tn), lambda l, j: (l, 0, j))],
        out_specs=pl.BlockSpec((None, r, tn), lambda l, j: (l, 0, j)),
        out_shape=jax.ShapeDtypeStruct((depth, r, n), F32),
        compiler_params=_cparams("parallel", "parallel"),
        name="mods",
    )(cond, mod_w, mod_b.reshape(depth, 1, n))


class _Rows:
    def __init__(self, batch, lc, s, lat_only):
        self.batch = batch
        self.nct = 0 if lat_only else batch * lc // ROW_TILE
        self.tps = s // ROW_TILE
        self.tpc = lc // ROW_TILE
        self.n_tiles = self.nct + batch * self.tps

    def mod_row(self, i):
        return jnp.where(i < self.nct, self.batch, (i - self.nct) // self.tps)

    def seq_edges(self, i):
        pos_c = i % self.tpc
        pos_l = (i - self.nct) % self.tps
        is_c = i < self.nct
        first = jnp.where(is_c, pos_c == 0, pos_l == 0)
        last = jnp.where(is_c, pos_c == self.tpc - 1, pos_l == self.tps - 1)
        return first, last


def _normmod_mm_kernel(h_ref, g_ref, mod_ref, w_ref, o_ref, *, shift_i, scale_i):
    u = _norm_mod(h_ref[...], g_ref[...], mod_ref[shift_i], mod_ref[scale_i])
    o_ref[...] = _dot(u.astype(BF16), w_ref[...])


def _normmod_mm(h, g, mods_l, w, rows, shift_i, scale_i, name):
    d = h.shape[1]
    n = w.shape[1]
    return pl.pallas_call(
        functools.partial(_normmod_mm_kernel, shift_i=shift_i, scale_i=scale_i),
        grid=(rows.n_tiles,),
        in_specs=[pl.BlockSpec((ROW_TILE, d), lambda i: (i, 0)),
                  pl.BlockSpec((1, d), lambda i: (0, 0)),
                  pl.BlockSpec((None, 6, 1, d), lambda i: (rows.mod_row(i), 0, 0, 0)),
                  pl.BlockSpec((d, n), lambda i: (0, 0))],
        out_specs=pl.BlockSpec((ROW_TILE, n), lambda i: (i, 0)),
        out_shape=jax.ShapeDtypeStruct((rows.n_tiles * ROW_TILE, n), F32),
        compiler_params=_cparams("parallel"),
        name=name,
    )(h, g.reshape(1, d), mods_l, w)


def _gla_kernel(q_ref, k_ref, v_ref, g_ref, r_ref, waf_ref, wab_ref, baf_ref, bab_ref, gain_ref,
                s0f_ref, s0b_ref, o_ref, sf_ref, sb_ref, gl_scr, tot_scr, qd_scr, ds_scr, o_scr, st_scr,
                *, seq_len):
    C, SL = GLA_CHUNK, GLA_SLAB
    cps = SL // C
    n_chunks, n_slabs = seq_len // C, seq_len // SL
    head_of_lane = lax.broadcasted_iota(jnp.int32, (SL, 2 * GLA_DK), 1) // GLA_DK
    row = lax.broadcasted_iota(jnp.int32, (SL, SL), 0)
    col = lax.broadcasted_iota(jnp.int32, (SL, SL), 1)
    same_chunk = (row // C) == (col // C)
    srow = lax.broadcasted_iota(jnp.int32, (2 * GLA_DV, 2 * GLA_DK), 0) // GLA_DV
    scol = lax.broadcasted_iota(jnp.int32, (2 * GLA_DV, 2 * GLA_DK), 1) // GLA_DK
    same_head = srow == scol
    r = r_ref[...]

    def direction(wa_ref, ba_ref, s0_ref, s_out_ref, forward):
        mask = jnp.logical_and(same_chunk, (row >= col) if forward else (row <= col))
        tri = jnp.where(mask, 1.0, 0.0).astype(BF16)
        gl_scr[...] = _log_sigmoid(_dot3(r, wa_ref[...]) + ba_ref[...]) * (1.0 / GLA_TAU)

        def slab(s, carry):
            rows = pl.ds(pl.multiple_of(s * SL, SL), SL)
            gl = gl_scr[rows, :]
            g_hi = gl.astype(BF16)
            g_r1 = gl - g_hi.astype(F32)
            g_mid = g_r1.astype(BF16)
            g_lo = (g_r1 - g_mid.astype(F32)).astype(BF16)
            b = _dot(tri, g_hi) + _dot(tri, g_mid) + _dot(tri, g_lo)
            b3 = b.reshape(cps, C, 2 * GLA_DK)
            last = b3[:, C - 1:C, :] if forward else b3[:, 0:1, :]
            tot = jnp.broadcast_to(last, b3.shape).reshape(SL, 2 * GLA_DK)
            tot_scr[rows, :] = tot
            q = q_ref[rows, :] * (GLA_DK ** -0.5)
            k = k_ref[rows, :]
            v = v_ref[rows, :].astype(BF16)
            qd = q * jnp.exp(b)
            kd = (k * jnp.exp(-b)).astype(BF16)
            kr = (k * jnp.exp(tot - b)).astype(BF16)
            qd_scr[rows, :] = qd.astype(BF16)
            o_parts = []
            for h in range(2):
                qh = jnp.where(head_of_lane == h, qd, 0.0).astype(BF16)
                a = jnp.where(mask, _dot_nt(qh, kd), 0.0).astype(BF16)
                o_parts.append(_dot(a, v[:, h * GLA_DV:(h + 1) * GLA_DV]))
            o = jnp.concatenate(o_parts, axis=1)
            if forward:
                o_scr[rows, :] = o
            else:
                o_scr[rows, :] += o
            for c in range(cps):
                ds = _dot_tn(v[c * C:(c + 1) * C], kr[c * C:(c + 1) * C])
                ds_scr[s * cps + c] = jnp.where(same_head, ds, 0.0)
            return carry

        lax.fori_loop(0, n_slabs, slab, 0, unroll=2)
        st_scr[...] = s0_ref[...]

        def step(i, carry):
            c = i if forward else n_chunks - 1 - i
            first = pl.multiple_of(c * C, C)
            rows = pl.ds(first, C)
            st = st_scr[...]
            o_scr[rows, :] += _dot_nt(qd_scr[rows, :], st.astype(BF16))
            st_scr[...] = st * jnp.exp(tot_scr[pl.ds(first, 1), :]) + ds_scr[c]
            return carry

        lax.fori_loop(0, n_chunks, step, 0, unroll=GLA_STEP_UNROLL)
        s_out_ref[...] = st_scr[...]

    direction(waf_ref, baf_ref, s0f_ref, sf_ref, True)
    direction(wab_ref, bab_ref, s0b_ref, sb_ref, False)
    gain = gain_ref[...]

    def readout(s, carry):
        rows = pl.ds(pl.multiple_of(s * SL, SL), SL)
        o = o_scr[rows, :]
        outs = []
        for h in range(2):
            oh = o[:, h * GLA_DV:(h + 1) * GLA_DV]
            outs.append(oh * lax.rsqrt(jnp.mean(oh * oh, axis=-1, keepdims=True) + EPS) * gain)
        o_ref[rows, :] = jnp.concatenate(outs, axis=1) * _silu(g_ref[rows, :])
        return carry

    lax.fori_loop(0, n_slabs, readout, 0)


def _gla(p, waf, wab, baf, bab, gain, s0f, s0b, batch, seq_len, blk0):
    dk2, dv2 = 2 * GLA_DK, 2 * GLA_DV
    seq = lambda width, cb: pl.BlockSpec((seq_len, width), lambda b, hp: (blk0 + b, cb(hp)))
    state_spec = pl.BlockSpec((None, None, dv2, dk2), lambda b, hp: (b, hp, 0, 0))
    in_specs = [seq(dk2, lambda hp: hp),
                seq(dk2, lambda hp: GLA_QK // dk2 + hp),
                seq(dv2, lambda hp: 2 * GLA_QK // dv2 + hp),
                seq(dv2, lambda hp: (2 * GLA_QK + GLA_V) // dv2 + hp),
                seq(R_COLS, lambda hp: (2 * GLA_QK + 2 * GLA_V + 3 * HY_WIDTH) // R_COLS),
                pl.BlockSpec((R_COLS, dk2), lambda b, hp: (0, hp)),
                pl.BlockSpec((R_COLS, dk2), lambda b, hp: (0, hp)),
                pl.BlockSpec((1, dk2), lambda b, hp: (0, hp)),
                pl.BlockSpec((1, dk2), lambda b, hp: (0, hp)),
                pl.BlockSpec((1, GLA_DV), lambda b, hp: (0, 0)),
                state_spec, state_spec]
    state_shape = jax.ShapeDtypeStruct((batch, 2, dv2, dk2), F32)
    return pl.pallas_call(
        functools.partial(_gla_kernel, seq_len=seq_len),
        grid=(batch, 2),
        in_specs=in_specs,
        out_specs=[pl.BlockSpec((seq_len, dv2), lambda b, hp: (b, hp)), state_spec, state_spec],
        out_shape=[jax.ShapeDtypeStruct((batch * seq_len, GLA_V), F32), state_shape, state_shape],
        scratch_shapes=[pltpu.VMEM((seq_len, dk2), F32),
                        pltpu.VMEM((seq_len, dk2), F32),
                        pltpu.VMEM((seq_len, dk2), BF16),
                        pltpu.VMEM((seq_len // GLA_CHUNK, dv2, dk2), F32),
                        pltpu.VMEM((seq_len, dv2), F32),
                        pltpu.VMEM((dv2, dk2), F32)],
        compiler_params=_cparams("parallel", "parallel"),
        name="gla",
    )(p, p, p, p, p, waf, wab, baf, bab, gain, s0f, s0b)


def _conv3(m, prev_row, next_row, w_ref):
    n = m.shape[0]
    ridx = lax.broadcasted_iota(jnp.int32, m.shape, 0)
    m_prev = jnp.where(ridx == 0, prev_row, pltpu.roll(m, 1, 0))
    m_next = jnp.where(ridx == n - 1, next_row, pltpu.roll(m, n - 1, 0))
    return w_ref[0:1, :] * m_prev + w_ref[1:2, :] * m + w_ref[2:3, :] * m_next


def _halo_specs(width, col_block, t_rows):
    g = ROW_TILE // 8
    last = t_rows // 8 - 1
    prev = pl.BlockSpec((8, width), lambda i: (jnp.maximum(i * g - 1, 0), col_block))
    nxt = pl.BlockSpec((8, width), lambda i: (jnp.minimum((i + 1) * g, last), col_block))
    return prev, nxt


def _hyena_pre_kernel(x0_ref, x1_ref, v_ref, x0p_ref, x0n_ref, x1p_ref, x1n_ref, vp_ref, vn_ref,
                      w_ref, b_ref, x0c_ref, z_ref, *, rows):
    first, last = rows.seq_edges(pl.program_id(0))
    keep_p = jnp.where(first, 0.0, 1.0)
    keep_n = jnp.where(last, 0.0, 1.0)
    hw = HY_WIDTH

    def conv(ref, p_ref, n_ref, j):
        w = w_ref.at[:, j * hw:(j + 1) * hw]
        y = _conv3(ref[...], p_ref[7:8, :] * keep_p, n_ref[0:1, :] * keep_n, w)
        return y + b_ref[:, j * hw:(j + 1) * hw]

    x0c_ref[...] = conv(x0_ref, x0p_ref, x0n_ref, 0)
    z_ref[...] = conv(v_ref, vp_ref, vn_ref, 2) * conv(x1_ref, x1p_ref, x1n_ref, 1)


def _hyena_pre(p, conv_w, conv_b, rows):
    t_rows = p.shape[0]
    hw = HY_WIDTH
    cb0 = (2 * GLA_QK + 2 * GLA_V) // hw
    in_specs = [pl.BlockSpec((ROW_TILE, hw), lambda i, j=j: (i, cb0 + j)) for j in range(3)]
    for j in range(3):
        in_specs.extend(_halo_specs(hw, cb0 + j, t_rows))
    in_specs += [pl.BlockSpec((3, 3 * hw), lambda i: (0, 0)), pl.BlockSpec((1, 3 * hw), lambda i: (0, 0))]
    out_spec = pl.BlockSpec((ROW_TILE, hw), lambda i: (i, 0))
    shape = jax.ShapeDtypeStruct((rows.n_tiles * ROW_TILE, hw), F32)
    return pl.pallas_call(
        functools.partial(_hyena_pre_kernel, rows=rows),
        grid=(rows.n_tiles,),
        in_specs=in_specs,
        out_specs=[out_spec, out_spec],
        out_shape=[shape, shape],
        compiler_params=_cparams("parallel"),
        name="hyena_pre",
    )(p, p, p, p, p, p, p, p, p, conv_w, conv_b.reshape(1, 3 * hw))


def _filter_kernel(z_ref, w1_ref, b1_ref, f1_ref, w2_ref, b2_ref, f2_ref, w3_ref, win_ref, o_ref):
    hh = jnp.sin(f1_ref[...] * (_dot3(z_ref[...], w1_ref[...]) + b1_ref[...]))
    hh = jnp.sin(f2_ref[...] * (_dot3(hh, w2_ref[...]) + b2_ref[...]))
    win = win_ref[...]
    o_ref[...] = _dot3(hh, w3_ref[...]) * jnp.concatenate([win, win], axis=1)


@functools.lru_cache(maxsize=None)
def _filter_features(L):
    t = np.linspace(0.0, 1.0, L, dtype=np.float32)[:, None]
    pos = np.arange(L, dtype=np.float32)[:, None]
    bands = np.linspace(1e-4, HY_BANDS - 1, HY_BANDS, dtype=np.float32)[None]
    ang = (np.float32(2.0 * math.pi / L) * pos * bands).astype(np.float32)
    z = np.concatenate([t, np.cos(ang), np.sin(ang)], axis=-1).astype(np.float32)
    z = np.pad(z, ((0, 0), (0, LANE - HY_EMB)))
    max_decay = math.log(HY_TARGET) / HY_FAST_DECAY
    min_decay = math.log(HY_TARGET) / HY_SLOW_DECAY
    deltas = np.linspace(min_decay, max_decay, HY_WIDTH, dtype=np.float32)
    window = np.exp(-t * np.abs(deltas)[None]).astype(np.float32)
    return z, window


def _hyena_filters(L, w1, b1, f1, w2, b2, f2, w3):
    z, window = _filter_features(L)
    w1p = jnp.pad(w1, ((0, LANE - HY_EMB), (0, 0)))
    tl = min(L, 512)
    full = lambda a: pl.BlockSpec(a.shape, lambda i: (0,) * a.ndim)
    row = lambda a: a.reshape(1, -1)
    ops = [w1p, row(b1), row(f1), w2, row(b2), row(f2), w3]
    return pl.pallas_call(
        _filter_kernel,
        grid=(L // tl,),
        in_specs=[pl.BlockSpec((tl, LANE), lambda i: (i, 0))] + [full(a) for a in ops]
                 + [pl.BlockSpec((tl, HY_WIDTH), lambda i: (i, 0))],
        out_specs=pl.BlockSpec((tl, 2 * HY_WIDTH), lambda i: (i, 0)),
        out_shape=jax.ShapeDtypeStruct((L, 2 * HY_WIDTH), F32),
        compiler_params=_cparams("parallel"),
        name="hyena_filters",
    )(jnp.asarray(z), *ops, jnp.asarray(window))


def _freq_tile(L):
    return min(2 * L, 512)


@functools.lru_cache(maxsize=None)
def _dft_tables(L):
    n = 2 * L
    tf = _freq_tile(L)
    half = tf // 2
    t = np.arange(L, dtype=np.int64)[None, :]
    fm = np.zeros((n, L), np.float64)
    scale = np.zeros((n, 1), np.float64)
    sign = np.zeros((n, 1), np.float64)
    for j in range(n // tf):
        k = (np.arange(half, dtype=np.int64) + j * half)[:, None]
        ang = 2.0 * np.pi * ((k * t) % n).astype(np.float64) / n
        fm[j * tf:j * tf + half] = np.cos(ang)
        fm[j * tf + half:(j + 1) * tf] = -np.sin(ang)
        scale[j * tf:(j + 1) * tf] = 2.0 / n
        sign[j * tf:j * tf + half] = 1.0
        sign[j * tf + half:(j + 1) * tf] = -1.0
    fm[half] = np.cos(np.pi * t[0])
    scale[0] = 1.0 / n
    scale[half] = 1.0 / n
    sign[half] = 1.0
    return (fm.astype(np.float32), np.ascontiguousarray(fm.T).astype(np.float32),
            scale.astype(np.float32), sign.astype(np.float32))


def _ktab_kernel(f_ref, h_ref, scale_ref, sign_ref, o_ref):
    hw = HY_WIDTH
    hh, hl = _split_bf16(h_ref[...])
    f = f_ref[...]
    kk = _dot(f, hh) + _dot(f, hl)
    o_ref[...] = scale_ref[...] * (kk[:, :hw] + sign_ref[...] * kk[:, hw:])


def _hyena_ktab(filt, L):
    fm, _, scale, sign = _dft_tables(L)
    n = 2 * L
    tf = _freq_tile(L)
    return pl.pallas_call(
        _ktab_kernel,
        grid=(n // tf,),
        in_specs=[pl.BlockSpec((tf, L), lambda j: (j, 0)),
                  pl.BlockSpec((L, 2 * HY_WIDTH), lambda j: (0, 0)),
                  pl.BlockSpec((tf, 1), lambda j: (j, 0)),
                  pl.BlockSpec((tf, 1), lambda j: (j, 0))],
        out_specs=pl.BlockSpec((tf, HY_WIDTH), lambda j: (j, 0)),
        out_shape=jax.ShapeDtypeStruct((n, HY_WIDTH), F32),
        compiler_params=_cparams("parallel"),
        name="hyena_ktab",
    )(jnp.asarray(fm, dtype=BF16), filt, jnp.asarray(scale), jnp.asarray(sign))


def _hyena_conv_kernel(z_ref, x0_ref, f_ref, ft_ref, k_ref, skip_ref, o_ref, zb_scr, acc_scr, *, tf):
    j = pl.program_id(1)
    half = tf // 2

    @pl.when(j == 0)
    def _():
        zb_scr[...] = z_ref[...].astype(BF16)
        acc_scr[...] = jnp.zeros_like(acc_scr)

    zf = _dot(f_ref[...], zb_scr[...])
    re, im = zf[:half], zf[half:]
    kre, kim = k_ref[:half, :], k_ref[half:, :]
    ridx = lax.broadcasted_iota(jnp.int32, re.shape, 0)
    mix = jnp.where(jnp.logical_and(j == 0, ridx == 0), 0.0, 1.0)
    yre = re * kre - mix * (im * kim)
    yim = mix * (re * kim) + im * jnp.where(mix == 0.0, kim, kre)
    y = jnp.concatenate([yre, yim], axis=0).astype(BF16)
    acc_scr[...] += _dot(ft_ref[...], y)

    @pl.when(j == pl.num_programs(1) - 1)
    def _():
        o_ref[...] = x0_ref[...] * (acc_scr[...] + z_ref[...] * skip_ref[...])


def _hyena_conv(z, x0c, ktab, skip, batch, L, blk0):
    fm, fmt, _, _ = _dft_tables(L)
    n = 2 * L
    tf = _freq_tile(L)
    hw = HY_WIDTH
    return pl.pallas_call(
        functools.partial(_hyena_conv_kernel, tf=tf),
        grid=(batch, n // tf),
        in_specs=[pl.BlockSpec((L, hw), lambda b, j: (blk0 + b, 0)),
                  pl.BlockSpec((L, hw), lambda b, j: (blk0 + b, 0)),
                  pl.BlockSpec((tf, L), lambda b, j: (j, 0)),
                  pl.BlockSpec((L, tf), lambda b, j: (0, j)),
                  pl.BlockSpec((tf, hw), lambda b, j: (j, 0)),
                  pl.BlockSpec((1, hw), lambda b, j: (0, 0))],
        out_specs=pl.BlockSpec((L, hw), lambda b, j: (b, 0)),
        out_shape=jax.ShapeDtypeStruct((batch * L, hw), F32),
        scratch_shapes=[pltpu.VMEM((L, hw), BF16), pltpu.VMEM((L, hw), F32)],
        compiler_params=_cparams("parallel", "arbitrary"),
        name="hyena_conv",
    )(z, x0c, jnp.asarray(fm, dtype=BF16), jnp.asarray(fmt, dtype=BF16), ktab, skip.reshape(1, hw))


def _mix_out_kernel(ac_ref, al_ref, bc_ref, bl_ref, wa_ref, wb_ref, h_ref, mod_ref, o_ref, *, nct):
    is_ctx = pl.program_id(0) < nct
    a = jnp.where(is_ctx, ac_ref[...], al_ref[...])
    b = jnp.where(is_ctx, bc_ref[...], bl_ref[...])
    y = _dot(a.astype(BF16), wa_ref[...]) + _dot(b.astype(BF16), wb_ref[...])
    o_ref[...] = h_ref[...] + mod_ref[2] * y


def _mix_out(a_ctx, a_lat, b_ctx, b_lat, w_out, h, mods_l, rows):
    d = h.shape[1]
    ka, kb = a_ctx.shape[1], b_ctx.shape[1]
    nct = rows.nct
    ctx_spec = lambda k: pl.BlockSpec((ROW_TILE, k), lambda i: (jnp.minimum(i, nct - 1), 0))
    lat_spec = lambda k: pl.BlockSpec((ROW_TILE, k), lambda i: (jnp.maximum(i - nct, 0), 0))
    return pl.pallas_call(
        functools.partial(_mix_out_kernel, nct=nct),
        grid=(rows.n_tiles,),
        in_specs=[ctx_spec(ka), lat_spec(ka), ctx_spec(kb), lat_spec(kb),
                  pl.BlockSpec((ka, d), lambda i: (0, 0)),
                  pl.BlockSpec((kb, d), lambda i: (ka // kb, 0)),
                  pl.BlockSpec((ROW_TILE, d), lambda i: (i, 0)),
                  pl.BlockSpec((None, 6, 1, d), lambda i: (rows.mod_row(i), 0, 0, 0))],
        out_specs=pl.BlockSpec((ROW_TILE, d), lambda i: (i, 0)),
        out_shape=jax.ShapeDtypeStruct((rows.n_tiles * ROW_TILE, d), F32),
        compiler_params=_cparams("parallel"),
        name="mix_out",
    )(a_ctx, a_lat, b_ctx, b_lat, w_out, w_out, h, mods_l)


def _shortconv_out_kernel(bg_ref, cg_ref, xi_ref, cgp_ref, cgn_ref, xip_ref, xin_ref, cw_ref, w_ref,
                          h_ref, mod_ref, o_ref, *, rows):
    first, last = rows.seq_edges(pl.program_id(0))
    keep_p = jnp.where(first, 0.0, 1.0)
    keep_n = jnp.where(last, 0.0, 1.0)
    m = cg_ref[...] * xi_ref[...]
    m_prev = cgp_ref[7:8, :] * xip_ref[7:8, :] * keep_p
    m_next = cgn_ref[0:1, :] * xin_ref[0:1, :] * keep_n
    y = bg_ref[...] * _conv3(m, m_prev, m_next, cw_ref)
    o_ref[...] = h_ref[...] + mod_ref[2] * _dot(y.astype(BF16), w_ref[...])


def _shortconv_out(p, conv_w, w_out, h, mods_l, rows):
    d = h.shape[1]
    t_rows = p.shape[0]
    in_specs = [pl.BlockSpec((ROW_TILE, d), lambda i, j=j: (i, j)) for j in range(3)]
    in_specs += [*_halo_specs(d, 1, t_rows), *_halo_specs(d, 2, t_rows),
                 pl.BlockSpec((3, d), lambda i: (0, 0)),
                 pl.BlockSpec((d, d), lambda i: (0, 0)),
                 pl.BlockSpec((ROW_TILE, d), lambda i: (i, 0)),
                 pl.BlockSpec((None, 6, 1, d), lambda i: (rows.mod_row(i), 0, 0, 0))]
    return pl.pallas_call(
        functools.partial(_shortconv_out_kernel, rows=rows),
        grid=(rows.n_tiles,),
        in_specs=in_specs,
        out_specs=pl.BlockSpec((ROW_TILE, d), lambda i: (i, 0)),
        out_shape=jax.ShapeDtypeStruct((rows.n_tiles * ROW_TILE, d), F32),
        compiler_params=_cparams("parallel"),
        name="shortconv_out",
    )(p, p, p, p, p, p, p, conv_w, w_out, h, mods_l)


META_E, META_RANK, META_GATE = 0, 2, 4


def _lane_min_index(mask, lane_f):
    return jnp.min(jnp.where(mask, lane_f, float(LANE)), axis=1, keepdims=True)


def _router_kernel(h_ref, g_ref, mod_ref, w_ref, b_ref, v_ref, meta_ref, cnt_ref, carry_scr):
    @pl.when(pl.program_id(0) == 0)
    def _():
        carry_scr[...] = jnp.zeros_like(carry_scr)

    v = _norm_mod(h_ref[...], g_ref[...], mod_ref[3], mod_ref[4])
    v_ref[...] = v
    lg = _dot3(v, w_ref[...]) + b_ref[...]
    tm = lg.shape[0]
    lane = lax.broadcasted_iota(jnp.int32, lg.shape, 1)
    lane_f = lane.astype(F32)
    neg = -jnp.inf

    is_grp = lane < N_GROUPS
    lgm = jnp.where(is_grp, lg, neg)
    m_g = jnp.max(lgm, axis=1, keepdims=True)
    s_g = jnp.sum(jnp.where(is_grp, jnp.exp(lg - m_g), 0.0), axis=1, keepdims=True)
    p_g = 1.0 / s_g
    grp = _lane_min_index(lgm == m_g, lane_f)

    ex_lane = lane - N_GROUPS
    in_grp = jnp.logical_and(jnp.logical_and(ex_lane >= 0, ex_lane < N_EXPERTS),
                             (ex_lane // EXP_PER_GROUP).astype(F32) == grp)
    m_e = jnp.max(jnp.where(in_grp, lg, neg), axis=1, keepdims=True)
    ee = jnp.where(in_grp, jnp.exp(lg - m_e), 0.0)
    pe = ee / jnp.sum(ee, axis=1, keepdims=True)
    pe1 = jnp.where(in_grp, pe, -1.0)
    p1 = jnp.max(pe1, axis=1, keepdims=True)
    i1 = _lane_min_index(pe1 == p1, lane_f)
    pe2 = jnp.where(lane_f == i1, -1.0, pe1)
    p2 = jnp.max(pe2, axis=1, keepdims=True)
    i2 = _lane_min_index(pe2 == p2, lane_f)
    denom = p1 + p2
    g1 = p_g * p1 / denom
    g2 = p_g * p2 / denom
    e1 = i1 - float(N_GROUPS)
    e2 = i2 - float(N_GROUPS)

    oh1 = lane_f == e1
    oh2 = lane_f == e2
    row = lax.broadcasted_iota(jnp.int32, (tm, tm), 0)
    col = lax.broadcasted_iota(jnp.int32, (tm, tm), 1)
    earlier = jnp.where(row > col, 1.0, 0.0).astype(BF16)
    c1 = _dot(earlier, jnp.where(oh1, 1.0, 0.0).astype(BF16))
    c2 = _dot(earlier, jnp.where(oh2, 1.0, 0.0).astype(BF16))
    tot1 = jnp.sum(jnp.where(oh1, 1.0, 0.0), axis=0, keepdims=True)
    tot2 = jnp.sum(jnp.where(oh2, 1.0, 0.0), axis=0, keepdims=True)
    carry = carry_scr[...]
    r1 = jnp.sum(jnp.where(oh1, carry + c1, 0.0), axis=1, keepdims=True)
    r2 = jnp.sum(jnp.where(oh2, carry + tot1 + c2, 0.0), axis=1, keepdims=True)
    carry = carry + tot1 + tot2
    carry_scr[...] = carry
    cnt_ref[...] = carry

    meta = jnp.zeros_like(lg)
    for idx, val in ((META_E, e1), (META_E + 1, e2), (META_RANK, r1), (META_RANK + 1, r2),
                     (META_GATE, g1), (META_GATE + 1, g2)):
        meta = jnp.where(lane == idx, val, meta)
    meta_ref[...] = meta


def _router(h, g, mods_l, w_rt, b_rt, rows):
    d = h.shape[1]
    nt = rows.n_tiles * ROW_TILE
    return pl.pallas_call(
        _router_kernel,
        grid=(rows.n_tiles,),
        in_specs=[pl.BlockSpec((ROW_TILE, d), lambda i: (i, 0)),
                  pl.BlockSpec((1, d), lambda i: (0, 0)),
                  pl.BlockSpec((None, 6, 1, d), lambda i: (rows.mod_row(i), 0, 0, 0)),
                  pl.BlockSpec((d, LANE), lambda i: (0, 0)),
                  pl.BlockSpec((1, LANE), lambda i: (0, 0))],
        out_specs=[pl.BlockSpec((ROW_TILE, d), lambda i: (i, 0)),
                   pl.BlockSpec((ROW_TILE, LANE), lambda i: (i, 0)),
                   pl.BlockSpec((1, LANE), lambda i: (0, 0))],
        out_shape=[jax.ShapeDtypeStruct((nt, d), F32), jax.ShapeDtypeStruct((nt, LANE), F32),
                   jax.ShapeDtypeStruct((1, LANE), F32)],
        scratch_shapes=[pltpu.VMEM((1, LANE), F32)],
        compiler_params=_cparams("arbitrary"),
        name="router",
    )(h, g.reshape(1, d), mods_l, w_rt, b_rt)


def _dispatch_plan(meta, counts):
    bm = MOE_ROWS
    t = meta.shape[0]
    n_blocks = -(-(t * TOP_K + N_EXPERTS * (bm - 1)) // bm)
    counts = counts[0, :N_EXPERTS].astype(jnp.int32)
    nblk = (counts + bm - 1) // bm
    blk_end = jnp.cumsum(nblk).astype(jnp.int32)
    slot0 = (blk_end - nblk) * bm
    expert = meta[:, META_E:META_E + TOP_K].astype(jnp.int32)
    rank = meta[:, META_RANK:META_RANK + TOP_K].astype(jnp.int32)
    onehot = expert[:, :, None] == jnp.arange(N_EXPERTS, dtype=jnp.int32)
    slot = rank + jnp.sum(jnp.where(onehot, slot0, 0), axis=-1)
    src = _slot_tokens(_slot_tiles(slot), n_blocks)
    blk = jnp.arange(n_blocks, dtype=jnp.int32)
    blk_e = jnp.minimum(jnp.sum(blk[:, None] >= blk_end[None, :], axis=1), N_EXPERTS - 1).astype(jnp.int32)
    blk_used = (blk < blk_end[-1]).astype(jnp.int32)
    return slot, src.reshape(n_blocks, 1, bm), blk_e, blk_used


def _slot_tiles(slot):
    return slot.reshape(slot.shape[0] // ROW_TILE, ROW_TILE, TOP_K).transpose(0, 2, 1)


def _slot_tokens_kernel(slot_ref, src_ref):
    i = pl.program_id(0)
    tm = slot_ref.shape[1]

    @pl.when(i == 0)
    def _():
        def clear(j, c):
            src_ref[j] = 0
            return c

        lax.fori_loop(0, src_ref.shape[0], clear, 0, unroll=16)

    base = i * tm
    for r in range(tm):
        for k in range(TOP_K):
            src_ref[slot_ref[k, r]] = base + r


def _slot_tokens(slot_tiles, n_blocks):
    n = n_blocks * MOE_ROWS
    return pl.pallas_call(
        _slot_tokens_kernel,
        grid=(slot_tiles.shape[0],),
        in_specs=[pl.BlockSpec((None, TOP_K, ROW_TILE), lambda i: (i, 0, 0), memory_space=pltpu.SMEM)],
        out_specs=pl.BlockSpec((n,), lambda i: (0,), memory_space=pltpu.SMEM),
        out_shape=jax.ShapeDtypeStruct((n,), jnp.int32),
        compiler_params=_cparams("arbitrary"),
        name="slot_tokens",
    )(slot_tiles)


def _row_copies_start(src_hbm, idx_ref, k, dst, sem):
    for r in range(dst.shape[0]):
        pltpu.make_async_copy(src_hbm.at[pl.ds(idx_ref[k, r], 1), :], dst.at[pl.ds(r, 1), :], sem).start()


def _row_copies_wait(src_hbm, dst, sem):
    pltpu.make_async_copy(src_hbm.at[pl.ds(0, dst.shape[0]), :], dst, sem).wait()


def _expert_kernel(blk_e_ref, blk_used_ref, src_ref, src_next_ref, v_hbm, w1_ref, w3_ref, w2_ref, y_ref,
                   xbuf, w1_scr, w3_scr, w2_scr, sem):
    j = pl.program_id(0)
    nb = pl.num_programs(0)
    used = blk_used_ref[j] > 0
    new_expert = jnp.logical_or(j == 0, blk_e_ref[j] != blk_e_ref[jnp.maximum(j - 1, 0)])

    @pl.when(j == 0)
    def _():
        _row_copies_start(v_hbm, src_ref, 0, xbuf.at[0], sem.at[0])

    @pl.when(jnp.logical_and(used, new_expert))
    def _():
        w1_scr[...] = w1_ref[...].astype(BF16)
        w3_scr[...] = w3_ref[...].astype(BF16)
        w2_scr[...] = w2_ref[...].astype(BF16)

    @pl.when(used)
    def _():
        b = j % 2
        _row_copies_wait(v_hbm, xbuf.at[b], sem.at[b])
        x = xbuf[b].astype(BF16)
        _row_copies_start(v_hbm, src_next_ref, 0, xbuf.at[1 - b], sem.at[1 - b])
        hid = _silu(_dot(x, w1_scr[...])) * _dot(x, w3_scr[...])
        y_ref[...] = _dot(hid.astype(BF16), w2_scr[...])

        @pl.when(jnp.logical_or(j == nb - 1, blk_used_ref[jnp.minimum(j + 1, nb - 1)] == 0))
        def _():
            _row_copies_wait(v_hbm, xbuf.at[1 - b], sem.at[1 - b])

    @pl.when(jnp.logical_not(used))
    def _():
        y_ref[...] = jnp.zeros_like(y_ref)


def _experts(v, src, blk_e, blk_used, w1, w3, w2):
    t, d = v.shape
    n_blocks = src.shape[0]
    bm = MOE_ROWS
    de = w1.shape[2]

    def next_block(j, e, used):
        nxt = jnp.minimum(j + 1, n_blocks - 1)
        return jnp.where(used[nxt] > 0, nxt, j)

    grid_spec = pltpu.PrefetchScalarGridSpec(
        num_scalar_prefetch=2,
        grid=(n_blocks,),
        in_specs=[pl.BlockSpec((None, 1, bm), lambda j, e, u: (j, 0, 0), memory_space=pltpu.SMEM),
                  pl.BlockSpec((None, 1, bm), lambda j, e, u: (next_block(j, e, u), 0, 0), memory_space=pltpu.SMEM),
                  pl.BlockSpec(memory_space=pl.ANY),
                  pl.BlockSpec((None, d, de), lambda j, e, u: (e[j], 0, 0)),
                  pl.BlockSpec((None, d, de), lambda j, e, u: (e[j], 0, 0)),
                  pl.BlockSpec((None, de, d), lambda j, e, u: (e[j], 0, 0))],
        out_specs=pl.BlockSpec((bm, d), lambda j, e, u: (j, 0)),
        scratch_shapes=[pltpu.VMEM((2, bm, d), F32), pltpu.VMEM((d, de), BF16), pltpu.VMEM((d, de), BF16),
                        pltpu.VMEM((de, d), BF16), pltpu.SemaphoreType.DMA((2,))],
    )
    return pl.pallas_call(
        _expert_kernel,
        grid_spec=grid_spec,
        out_shape=jax.ShapeDtypeStruct((n_blocks * bm, d), F32),
        compiler_params=_cparams("arbitrary"),
        name="experts",
    )(blk_e, blk_used, src, src, v, w1, w3, w2)


def _combined_tile(slot_ref, slot_next_ref, h_ref, meta_ref, modp_ref, y_hbm, buf, sem, overlap_with):
    i = pl.program_id(0)
    n = pl.num_programs(0)

    @pl.when(i == 0)
    def _():
        for k in range(TOP_K):
            _row_copies_start(y_hbm, slot_ref, k, buf.at[0, k], sem.at[0])

    b = i % 2
    for k in range(TOP_K):
        _row_copies_wait(y_hbm, buf.at[b, k], sem.at[b])
    meta = meta_ref[...]
    f = meta[:, META_GATE:META_GATE + 1] * buf[b, 0] + meta[:, META_GATE + 1:META_GATE + 2] * buf[b, 1]
    hn = h_ref[...] + modp_ref[5] * f
    for k in range(TOP_K):
        _row_copies_start(y_hbm, slot_next_ref, k, buf.at[1 - b, k], sem.at[1 - b])
    overlap_with(hn)

    @pl.when(i == n - 1)
    def _():
        for k in range(TOP_K):
            _row_copies_wait(y_hbm, buf.at[1 - b, k], sem.at[1 - b])


def _combine_mm_kernel(slot_ref, slot_next_ref, h_ref, meta_ref, modp_ref, g_ref, mod_ref, w_ref, y_hbm,
                       hn_ref, o_ref, buf, sem):
    def project(hn):
        hn_ref[...] = hn
        u = _norm_mod(hn, g_ref[...], mod_ref[0], mod_ref[1])
        o_ref[...] = _dot(u.astype(BF16), w_ref[...])

    _combined_tile(slot_ref, slot_next_ref, h_ref, meta_ref, modp_ref, y_hbm, buf, sem, project)


def _combine_norm_kernel(slot_ref, slot_next_ref, h_ref, meta_ref, modp_ref, g_ref, y_hbm, o_ref, buf, sem):
    def finish(hn):
        o_ref[...] = hn * lax.rsqrt(jnp.mean(hn * hn, axis=-1, keepdims=True) + EPS) * g_ref[...]

    _combined_tile(slot_ref, slot_next_ref, h_ref, meta_ref, modp_ref, y_hbm, buf, sem, finish)


def _combine_then(h, ys, slot_tiles, meta, mods_prev, rows, g, mods_l=None, w=None, name="combine_norm", tile0=0):
    d = h.shape[1]
    n_tiles = rows.n_tiles
    slot_spec = lambda fn: pl.BlockSpec((None, TOP_K, ROW_TILE), lambda i: (tile0 + fn(i), 0, 0),
                                        memory_space=pltpu.SMEM)
    mod_spec = pl.BlockSpec((None, 6, 1, d), lambda i: (rows.mod_row(i), 0, 0, 0))
    tile_in = lambda width: pl.BlockSpec((ROW_TILE, width), lambda i: (tile0 + i, 0))
    tile = lambda width: pl.BlockSpec((ROW_TILE, width), lambda i: (i, 0))
    rows_shape = lambda width: jax.ShapeDtypeStruct((n_tiles * ROW_TILE, width), F32)
    in_specs = [slot_spec(lambda i: i), slot_spec(lambda i: jnp.minimum(i + 1, n_tiles - 1)),
                tile_in(d), tile_in(LANE), mod_spec, pl.BlockSpec((1, d), lambda i: (0, 0))]
    args = [slot_tiles, slot_tiles, h, meta, mods_prev, g.reshape(1, d)]
    if w is None:
        kern, out_specs, out_shape = _combine_norm_kernel, tile(d), rows_shape(d)
    else:
        nw = w.shape[1]
        in_specs += [mod_spec, pl.BlockSpec((d, nw), lambda i: (0, 0))]
        args += [mods_l, w]
        kern, out_specs, out_shape = _combine_mm_kernel, [tile(d), tile(nw)], [rows_shape(d), rows_shape(nw)]
    return pl.pallas_call(
        kern,
        grid=(n_tiles,),
        in_specs=in_specs + [pl.BlockSpec(memory_space=pl.ANY)],
        out_specs=out_specs,
        out_shape=out_shape,
        scratch_shapes=[pltpu.VMEM((2, TOP_K, ROW_TILE, d), F32), pltpu.SemaphoreType.DMA((2,))],
        compiler_params=_cparams("arbitrary"),
        name=name,
    )(*args, ys)


def _even_w_in(w):
    d = w.shape[0]
    n_main = 2 * GLA_QK + 2 * GLA_V
    ranks = w[:, n_main:n_main + 2 * GLA_RANK]
    hy = w[:, n_main + 2 * GLA_RANK:]
    pad = jnp.zeros((d, R_COLS - 2 * GLA_RANK), w.dtype)
    return jnp.concatenate([w[:, :n_main], hy, ranks, pad], axis=1).astype(BF16)


def _rank_proj(wa, first_row):
    return jnp.zeros((R_COLS, GLA_QK), F32).at[first_row:first_row + GLA_RANK].set(wa)


def _grid_transpose(h_lat, batch, a, b):
    d = h_lat.shape[1]
    return h_lat.reshape(batch, a, b, d).transpose(0, 2, 1, 3).reshape(-1, d)


def kernel(x, c, ctx, c_ctx, mod_w, mod_b, norm_mix, norm_ffn, norm_final, ev_w_in, ev_w_out, gla_wa_f, gla_ba_f, gla_wa_b, gla_ba_b, gla_norm, hy_conv_w, hy_conv_b, hy_w1, hy_b1, hy_f1, hy_w2, hy_b2, hy_f2, hy_w3, hy_skip, od_w_in, od_conv_w, od_w_out, rt_w_grp, rt_b_grp, rt_w_exp, rt_b_exp, ex_w1, ex_w3, ex_w2):
    batch, s, d = x.shape
    lc = ctx.shape[1]
    depth = mod_w.shape[0]
    tc, tl = batch * lc, batch * s
    assert lc % ROW_TILE == 0 and s % ROW_TILE == 0 and tc % s == 0 and s % GRID_W == 0
    assert lc % GLA_CHUNK == 0 and s % GLA_CHUNK == 0
    assert depth % 2 == 0
    grid_rows = s // GRID_W

    n_cond = -(-(batch + 1) // 8) * 8
    cond = jnp.concatenate([c, c_ctx[None], jnp.zeros((n_cond - batch - 1, d), F32)], axis=0)
    mods = _mods(cond, mod_w, mod_b).reshape(depth, n_cond, 6, 1, d)

    h = jnp.concatenate([ctx.reshape(tc, d), x.reshape(tl, d)], axis=0)
    col_major_now = False
    moe = None

    def lat_part(a, fn):
        return fn(a) if a.shape[0] == tl else jnp.concatenate([a[:tc], fn(a[tc:])], axis=0)

    for l in range(depth):
        i = l // 2
        even = l % 2 == 0
        ctx_out = l < depth - 1
        col_major = i % 2 == 1
        per_token = [h] if moe is None else [h, moe[1], moe[2]]
        if col_major != col_major_now:
            perm = ((lambda a: _grid_transpose(a, batch, grid_rows, GRID_W)) if col_major
                    else (lambda a: _grid_transpose(a, batch, GRID_W, grid_rows)))
            per_token = [lat_part(a, perm) for a in per_token]
            col_major_now = col_major
        h = per_token[0]
        arr_rows = _Rows(batch, lc, s, not ctx_out)
        tile0 = (h.shape[0] - arr_rows.n_tiles * ROW_TILE) // ROW_TILE
        mods_l = mods[l]
        w_in = _even_w_in(ev_w_in[i]) if even else od_w_in[i].astype(BF16)
        name = "even_in" if even else "odd_in"
        if moe is None:
            p = _normmod_mm(h, norm_mix[l], mods_l, w_in, arr_rows, 0, 1, name)
        else:
            h, p = _combine_then(h, moe[0], _slot_tiles(per_token[1]), per_token[2], moe[3], arr_rows,
                                 norm_mix[l], mods_l, w_in, name, tile0)

        if even:
            waf = _rank_proj(gla_wa_f[i], 0)
            wab = _rank_proj(gla_wa_b[i], GLA_RANK)
            baf, bab = gla_ba_f[i].reshape(1, -1), gla_ba_b[i].reshape(1, -1)
            gain = gla_norm[i].reshape(1, -1)
            zeros = jnp.zeros((batch, 2, 2 * GLA_DV, 2 * GLA_DK), F32)
            filt_args = (hy_w1[i], hy_b1[i], hy_f1[i], hy_w2[i], hy_b2[i], hy_f2[i], hy_w3[i])
            x0c, z = _hyena_pre(p, hy_conv_w[i], hy_conv_b[i], arr_rows)
            gla_c, sc_f, sc_b = _gla(p, waf, wab, baf, bab, gain, zeros, zeros, batch, lc, 0)
            gla_l, _, _ = _gla(p, waf, wab, baf, bab, gain, sc_f, sc_b, batch, s, tc // s)
            kt_c = _hyena_ktab(_hyena_filters(lc, *filt_args), lc)
            kt_l = _hyena_ktab(_hyena_filters(s, *filt_args), s)
            hy_c = _hyena_conv(z, x0c, kt_c, hy_skip[i], batch, lc, 0)
            hy_l = _hyena_conv(z, x0c, kt_l, hy_skip[i], batch, s, tc // s)
            h = _mix_out(gla_c, gla_l, hy_c, hy_l, ev_w_out[i].astype(BF16), h, mods_l, arr_rows)
        else:
            h = _shortconv_out(p, od_conv_w[i], od_w_out[i].astype(BF16), h, mods_l, arr_rows)

        w_rt = jnp.concatenate([rt_w_grp[l], rt_w_exp[l],
                                jnp.zeros((d, LANE - N_GROUPS - N_EXPERTS), F32)], axis=1)
        b_rt = jnp.concatenate([rt_b_grp[l], rt_b_exp[l],
                                jnp.zeros((LANE - N_GROUPS - N_EXPERTS,), F32)]).reshape(1, LANE)
        v, meta, counts = _router(h, norm_ffn[l], mods_l, w_rt, b_rt, arr_rows)
        slot, src, blk_e, blk_used = _dispatch_plan(meta, counts)
        ys = _experts(v, src, blk_e, blk_used, ex_w1[l], ex_w3[l], ex_w2[l])
        moe = (ys, slot, meta, mods_l)

    out = _combine_then(h, moe[0], _slot_tiles(moe[1]), moe[2], moe[3], arr_rows, norm_final)
    if col_major_now:
        out = _grid_transpose(out, batch, GRID_W, grid_rows)
    return out.reshape(batch, s, d)
```

```python
import functools
import math

import numpy as np
import jax
import jax.numpy as jnp
from jax import lax
from jax.experimental import pallas as pl
from jax.experimental.pallas import tpu as pltpu

F32 = jnp.float32
BF16 = jnp.bfloat16

EPS = 1e-6
GRID_W = 64

GLA_HEADS = 4
GLA_DK = 64
GLA_DV = 128
GLA_RANK = 16
GLA_TAU = 16.0
GLA_CHUNK = 64
GLA_SLAB = 256
GLA_STEP_UNROLL = 4
GLA_QK = GLA_HEADS * GLA_DK
GLA_V = GLA_HEADS * GLA_DV

HY_WIDTH = 512
HY_EMB = 33
HY_BANDS = (HY_EMB - 1) // 2
HY_HIDDEN = 64
HY_FAST_DECAY = 0.3
HY_SLOW_DECAY = 1.5
HY_TARGET = 1e-2

N_GROUPS = 4
EXP_PER_GROUP = 8
N_EXPERTS = N_GROUPS * EXP_PER_GROUP
TOP_K = 2

LANE = 128
ROW_TILE = 256
MOE_ROWS = 256
R_COLS = LANE
VMEM_LIMIT = 56 * 1024 * 1024
DMA_THREADS = 2


def _cparams(*sem):
    return pltpu.CompilerParams(dimension_semantics=sem, vmem_limit_bytes=VMEM_LIMIT)


def _split_bf16(a):
    hi = a.astype(BF16)
    lo = (a - hi.astype(F32)).astype(BF16)
    return hi, lo


def _dot(a, b):
    return jnp.dot(a, b, preferred_element_type=F32)


def _dot_nt(a, b):
    return lax.dot_general(a, b, (((1,), (1,)), ((), ())), preferred_element_type=F32)


def _dot_tn(a, b):
    return lax.dot_general(a, b, (((0,), (0,)), ((), ())), preferred_element_type=F32)


def _dot3(a, b):
    ah, al = _split_bf16(a)
    bh, bl = _split_bf16(b)
    return _dot(ah, bh) + _dot(ah, bl) + _dot(al, bh)


def _silu(x):
    return x / (1.0 + jnp.exp(-x))


def _log_sigmoid(x):
    return jnp.minimum(x, 0.0) - jnp.log1p(jnp.exp(-jnp.abs(x)))


def _norm_mod(x, g, shift, scale):
    y = x * lax.rsqrt(jnp.mean(x * x, axis=-1, keepdims=True) + EPS)
    return (y * g) * (1.0 + scale) + shift


def _mods_kernel(s_ref, w_ref, b_ref, o_ref):
    s = s_ref[...]
    s = _silu(s)
    o_ref[...] = _dot3(s, w_ref[...]) + b_ref[...]


def _mods(cond, mod_w, mod_b):
    depth, d, n = mod_w.shape
    r = cond.shape[0]
    tn = 1024
    return pl.pallas_call(
        _mods_kernel,
        grid=(depth, n // tn),
        in_specs=[pl.BlockSpec((r, d), lambda l, j: (0, 0)),
                  pl.BlockSpec((None, d, tn), lambda l, j: (l, 0, j)),
                  pl.BlockSpec((None, 1, tn), lambda l, j: (l, 0, j))],
        out_specs=pl.BlockSpec((None, r, tn), lambda l, j: (l, 0, j)),
        out_shape=jax.ShapeDtypeStruct((depth, r, n), F32),
        compiler_params=_cparams("parallel", "parallel"),
        name="mods",
    )(cond, mod_w, mod_b.reshape(depth, 1, n))


class _Rows:
    def __init__(self, batch, lc, s, lat_only):
        self.batch = batch
        self.nct = 0 if lat_only else batch * lc // ROW_TILE
        self.tps = s // ROW_TILE
        self.tpc = lc // ROW_TILE
        self.n_tiles = self.nct + batch * self.tps

    def mod_row(self, i):
        return jnp.where(i < self.nct, self.batch, (i - self.nct) // self.tps)

    def seq_edges(self, i):
        pos_c = i % self.tpc
        pos_l = (i - self.nct) % self.tps
        is_c = i < self.nct
        first = jnp.where(is_c, pos_c == 0, pos_l == 0)
        last = jnp.where(is_c, pos_c == self.tpc - 1, pos_l == self.tps - 1)
        return first, last


def _normmod_mm_kernel(h_ref, g_ref, mod_ref, w_ref, o_ref, *, shift_i, scale_i):
    u = _norm_mod(h_ref[...], g_ref[...], mod_ref[shift_i], mod_ref[scale_i])
    o_ref[...] = _dot(u.astype(BF16), w_ref[...])


def _normmod_mm(h, g, mods_l, w, rows, shift_i, scale_i, name):
    d = h.shape[1]
    n = w.shape[1]
    return pl.pallas_call(
        functools.partial(_normmod_mm_kernel, shift_i=shift_i, scale_i=scale_i),
        grid=(rows.n_tiles,),
        in_specs=[pl.BlockSpec((ROW_TILE, d), lambda i: (i, 0)),
                  pl.BlockSpec((1, d), lambda i: (0, 0)),
                  pl.BlockSpec((None, 6, 1, d), lambda i: (rows.mod_row(i), 0, 0, 0)),
                  pl.BlockSpec((d, n), lambda i: (0, 0))],
        out_specs=pl.BlockSpec((ROW_TILE, n), lambda i: (i, 0)),
        out_shape=jax.ShapeDtypeStruct((rows.n_tiles * ROW_TILE, n), F32),
        compiler_params=_cparams("parallel"),
        name=name,
    )(h, g.reshape(1, d), mods_l, w)


def _gla_kernel(q_ref, k_ref, v_ref, g_ref, r_ref, waf_ref, wab_ref, baf_ref, bab_ref, gain_ref,
                s0f_ref, s0b_ref, o_ref, sf_ref, sb_ref, gl_scr, tot_scr, qd_scr, ds_scr, o_scr, st_scr,
                *, seq_len):
    C, SL = GLA_CHUNK, GLA_SLAB
    cps = SL // C
    n_chunks, n_slabs = seq_len // C, seq_len // SL
    head_of_lane = lax.broadcasted_iota(jnp.int32, (SL, 2 * GLA_DK), 1) // GLA_DK
    row = lax.broadcasted_iota(jnp.int32, (SL, SL), 0)
    col = lax.broadcasted_iota(jnp.int32, (SL, SL), 1)
    same_chunk = (row // C) == (col // C)
    srow = lax.broadcasted_iota(jnp.int32, (2 * GLA_DV, 2 * GLA_DK), 0) // GLA_DV
    scol = lax.broadcasted_iota(jnp.int32, (2 * GLA_DV, 2 * GLA_DK), 1) // GLA_DK
    same_head = srow == scol
    r = r_ref[...]

    def direction(wa_ref, ba_ref, s0_ref, s_out_ref, forward):
        mask = jnp.logical_and(same_chunk, (row >= col) if forward else (row <= col))
        tri = jnp.where(mask, 1.0, 0.0).astype(BF16)
        gl_scr[...] = _log_sigmoid(_dot3(r, wa_ref[...]) + ba_ref[...]) * (1.0 / GLA_TAU)

        def slab(s, carry):
            rows = pl.ds(pl.multiple_of(s * SL, SL), SL)
            gl = gl_scr[rows, :]
            g_hi = gl.astype(BF16)
            g_r1 = gl - g_hi.astype(F32)
            g_mid = g_r1.astype(BF16)
            g_lo = (g_r1 - g_mid.astype(F32)).astype(BF16)
            b = _dot(tri, g_hi) + _dot(tri, g_mid) + _dot(tri, g_lo)
            b3 = b.reshape(cps, C, 2 * GLA_DK)
            last = b3[:, C - 1:C, :] if forward else b3[:, 0:1, :]
            tot = jnp.broadcast_to(last, b3.shape).reshape(SL, 2 * GLA_DK)
            tot_scr[rows, :] = tot
            q = q_ref[rows, :] * (GLA_DK ** -0.5)
            k = k_ref[rows, :]
            v = v_ref[rows, :].astype(BF16)
            qd = q * jnp.exp(b)
            kd = (k * jnp.exp(-b)).astype(BF16)
            kr = (k * jnp.exp(tot - b)).astype(BF16)
            qd_scr[rows, :] = qd.astype(BF16)
            o_parts = []
            for h in range(2):
                qh = jnp.where(head_of_lane == h, qd, 0.0).astype(BF16)
                a = jnp.where(mask, _dot_nt(qh, kd), 0.0).astype(BF16)
                o_parts.append(_dot(a, v[:, h * GLA_DV:(h + 1) * GLA_DV]))
            o = jnp.concatenate(o_parts, axis=1)
            if forward:
                o_scr[rows, :] = o
            else:
                o_scr[rows, :] += o
            for c in range(cps):
                ds = _dot_tn(v[c * C:(c + 1) * C], kr[c * C:(c + 1) * C])
                ds_scr[s * cps + c] = jnp.where(same_head, ds, 0.0)
            return carry

        lax.fori_loop(0, n_slabs, slab, 0, unroll=2)
        st_scr[...] = s0_ref[...]

        def step(i, carry):
            c = i if forward else n_chunks - 1 - i
            first = pl.multiple_of(c * C, C)
            rows = pl.ds(first, C)
            st = st_scr[...]
            o_scr[rows, :] += _dot_nt(qd_scr[rows, :], st.astype(BF16))
            st_scr[...] = st * jnp.exp(tot_scr[pl.ds(first, 1), :]) + ds_scr[c]
            return carry

        lax.fori_loop(0, n_chunks, step, 0, unroll=GLA_STEP_UNROLL)
        s_out_ref[...] = st_scr[...]

    direction(waf_ref, baf_ref, s0f_ref, sf_ref, True)
    direction(wab_ref, bab_ref, s0b_ref, sb_ref, False)
    gain = gain_ref[...]

    def readout(s, carry):
        rows = pl.ds(pl.multiple_of(s * SL, SL), SL)
        o = o_scr[rows, :]
        outs = []
        for h in range(2):
            oh = o[:, h * GLA_DV:(h + 1) * GLA_DV]
            outs.append(oh * lax.rsqrt(jnp.mean(oh * oh, axis=-1, keepdims=True) + EPS) * gain)
        o_ref[rows, :] = jnp.concatenate(outs, axis=1) * _silu(g_ref[rows, :])
        return carry

    lax.fori_loop(0, n_slabs, readout, 0)


def _gla(p, waf, wab, baf, bab, gain, s0f, s0b, batch, seq_len, blk0):
    dk2, dv2 = 2 * GLA_DK, 2 * GLA_DV
    seq = lambda width, cb: pl.BlockSpec((seq_len, width), lambda b, hp: (blk0 + b, cb(hp)))
    state_spec = pl.BlockSpec((None, None, dv2, dk2), lambda b, hp: (b, hp, 0, 0))
    in_specs = [seq(dk2, lambda hp: hp),
                seq(dk2, lambda hp: GLA_QK // dk2 + hp),
                seq(dv2, lambda hp: 2 * GLA_QK // dv2 + hp),
                seq(dv2, lambda hp: (2 * GLA_QK + GLA_V) // dv2 + hp),
                seq(R_COLS, lambda hp: (2 * GLA_QK + 2 * GLA_V + 3 * HY_WIDTH) // R_COLS),
                pl.BlockSpec((R_COLS, dk2), lambda b, hp: (0, hp)),
                pl.BlockSpec((R_COLS, dk2), lambda b, hp: (0, hp)),
                pl.BlockSpec((1, dk2), lambda b, hp: (0, hp)),
                pl.BlockSpec((1, dk2), lambda b, hp: (0, hp)),
                pl.BlockSpec((1, GLA_DV), lambda b, hp: (0, 0)),
                state_spec, state_spec]
    state_shape = jax.ShapeDtypeStruct((batch, 2, dv2, dk2), F32)
    return pl.pallas_call(
        functools.partial(_gla_kernel, seq_len=seq_len),
        grid=(batch, 2),
        in_specs=in_specs,
        out_specs=[pl.BlockSpec((seq_len, dv2), lambda b, hp: (b, hp)), state_spec, state_spec],
        out_shape=[jax.ShapeDtypeStruct((batch * seq_len, GLA_V), F32), state_shape, state_shape],
        scratch_shapes=[pltpu.VMEM((seq_len, dk2), F32),
                        pltpu.VMEM((seq_len, dk2), F32),
                        pltpu.VMEM((seq_len, dk2), BF16),
                        pltpu.VMEM((seq_len // GLA_CHUNK, dv2, dk2), F32),
                        pltpu.VMEM((seq_len, dv2), F32),
                        pltpu.VMEM((dv2, dk2), F32)],
        compiler_params=_cparams("parallel", "parallel"),
        name="gla",
    )(p, p, p, p, p, waf, wab, baf, bab, gain, s0f, s0b)


def _conv3(m, prev_row, next_row, w_ref):
    n = m.shape[0]
    ridx = lax.broadcasted_iota(jnp.int32, m.shape, 0)
    m_prev = jnp.where(ridx == 0, prev_row, pltpu.roll(m, 1, 0))
    m_next = jnp.where(ridx == n - 1, next_row, pltpu.roll(m, n - 1, 0))
    return w_ref[0:1, :] * m_prev + w_ref[1:2, :] * m + w_ref[2:3, :] * m_next


def _halo_specs(width, col_block, t_rows):
    g = ROW_TILE // 8
    last = t_rows // 8 - 1
    prev = pl.BlockSpec((8, width), lambda i: (jnp.maximum(i * g - 1, 0), col_block))
    nxt = pl.BlockSpec((8, width), lambda i: (jnp.minimum((i + 1) * g, last), col_block))
    return prev, nxt


def _hyena_pre_kernel(x0_ref, x1_ref, v_ref, x0p_ref, x0n_ref, x1p_ref, x1n_ref, vp_ref, vn_ref,
                      w_ref, b_ref, x0c_ref, z_ref, *, rows):
    first, last = rows.seq_edges(pl.program_id(0))
    keep_p = jnp.where(first, 0.0, 1.0)
    keep_n = jnp.where(last, 0.0, 1.0)
    hw = HY_WIDTH

    def conv(ref, p_ref, n_ref, j):
        w = w_ref.at[:, j * hw:(j + 1) * hw]
        y = _conv3(ref[...], p_ref[7:8, :] * keep_p, n_ref[0:1, :] * keep_n, w)
        return y + b_ref[:, j * hw:(j + 1) * hw]

    x0c_ref[...] = conv(x0_ref, x0p_ref, x0n_ref, 0)
    z_ref[...] = conv(v_ref, vp_ref, vn_ref, 2) * conv(x1_ref, x1p_ref, x1n_ref, 1)


def _hyena_pre(p, conv_w, conv_b, rows):
    t_rows = p.shape[0]
    hw = HY_WIDTH
    cb0 = (2 * GLA_QK + 2 * GLA_V) // hw
    in_specs = [pl.BlockSpec((ROW_TILE, hw), lambda i, j=j: (i, cb0 + j)) for j in range(3)]
    for j in range(3):
        in_specs.extend(_halo_specs(hw, cb0 + j, t_rows))
    in_specs += [pl.BlockSpec((3, 3 * hw), lambda i: (0, 0)), pl.BlockSpec((1, 3 * hw), lambda i: (0, 0))]
    out_spec = pl.BlockSpec((ROW_TILE, hw), lambda i: (i, 0))
    shape = jax.ShapeDtypeStruct((rows.n_tiles * ROW_TILE, hw), F32)
    return pl.pallas_call(
        functools.partial(_hyena_pre_kernel, rows=rows),
        grid=(rows.n_tiles,),
        in_specs=in_specs,
        out_specs=[out_spec, out_spec],
        out_shape=[shape, shape],
        compiler_params=_cparams("parallel"),
        name="hyena_pre",
    )(p, p, p, p, p, p, p, p, p, conv_w, conv_b.reshape(1, 3 * hw))


def _filter_kernel(z_ref, w1_ref, b1_ref, f1_ref, w2_ref, b2_ref, f2_ref, w3_ref, win_ref, o_ref):
    hh = jnp.sin(f1_ref[...] * (_dot3(z_ref[...], w1_ref[...]) + b1_ref[...]))
    hh = jnp.sin(f2_ref[...] * (_dot3(hh, w2_ref[...]) + b2_ref[...]))
    win = win_ref[...]
    o_ref[...] = _dot3(hh, w3_ref[...]) * jnp.concatenate([win, win], axis=1)


@functools.lru_cache(maxsize=None)
def _filter_features(L):
    t = np.linspace(0.0, 1.0, L, dtype=np.float32)[:, None]
    pos = np.arange(L, dtype=np.float32)[:, None]
    bands = np.linspace(1e-4, HY_BANDS - 1, HY_BANDS, dtype=np.float32)[None]
    ang = (np.float32(2.0 * math.pi / L) * pos * bands).astype(np.float32)
    z = np.concatenate([t, np.cos(ang), np.sin(ang)], axis=-1).astype(np.float32)
    z = np.pad(z, ((0, 0), (0, LANE - HY_EMB)))
    max_decay = math.log(HY_TARGET) / HY_FAST_DECAY
    min_decay = math.log(HY_TARGET) / HY_SLOW_DECAY
    deltas = np.linspace(min_decay, max_decay, HY_WIDTH, dtype=np.float32)
    window = np.exp(-t * np.abs(deltas)[None]).astype(np.float32)
    return z, window


def _hyena_filters(L, w1, b1, f1, w2, b2, f2, w3):
    z, window = _filter_features(L)
    w1p = jnp.pad(w1, ((0, LANE - HY_EMB), (0, 0)))
    tl = min(L, 512)
    full = lambda a: pl.BlockSpec(a.shape, lambda i: (0,) * a.ndim)
    row = lambda a: a.reshape(1, -1)
    ops = [w1p, row(b1), row(f1), w2, row(b2), row(f2), w3]
    return pl.pallas_call(
        _filter_kernel,
        grid=(L // tl,),
        in_specs=[pl.BlockSpec((tl, LANE), lambda i: (i, 0))] + [full(a) for a in ops]
                 + [pl.BlockSpec((tl, HY_WIDTH), lambda i: (i, 0))],
        out_specs=pl.BlockSpec((tl, 2 * HY_WIDTH), lambda i: (i, 0)),
        out_shape=jax.ShapeDtypeStruct((L, 2 * HY_WIDTH), F32),
        compiler_params=_cparams("parallel"),
        name="hyena_filters",
    )(jnp.asarray(z), *ops, jnp.asarray(window))


def _freq_tile(L):
    return min(2 * L, 512)


@functools.lru_cache(maxsize=None)
def _dft_tables(L):
    n = 2 * L
    tf = _freq_tile(L)
    half = tf // 2
    t = np.arange(L, dtype=np.int64)[None, :]
    fm = np.zeros((n, L), np.float64)
    scale = np.zeros((n, 1), np.float64)
    sign = np.zeros((n, 1), np.float64)
    for j in range(n // tf):
        k = (np.arange(half, dtype=np.int64) + j * half)[:, None]
        ang = 2.0 * np.pi * ((k * t) % n).astype(np.float64) / n
        fm[j * tf:j * tf + half] = np.cos(ang)
        fm[j * tf + half:(j + 1) * tf] = -np.sin(ang)
        scale[j * tf:(j + 1) * tf] = 2.0 / n
        sign[j * tf:j * tf + half] = 1.0
        sign[j * tf + half:(j + 1) * tf] = -1.0
    fm[half] = np.cos(np.pi * t[0])
    scale[0] = 1.0 / n
    scale[half] = 1.0 / n
    sign[half] = 1.0
    return (fm.astype(np.float32), np.ascontiguousarray(fm.T).astype(np.float32),
            scale.astype(np.float32), sign.astype(np.float32))


def _ktab_kernel(f_ref, h_ref, scale_ref, sign_ref, o_ref):
    hw = HY_WIDTH
    hh, hl = _split_bf16(h_ref[...])
    f = f_ref[...]
    kk = _dot(f, hh) + _dot(f, hl)
    o_ref[...] = scale_ref[...] * (kk[:, :hw] + sign_ref[...] * kk[:, hw:])


def _hyena_ktab(filt, L):
    fm, _, scale, sign = _dft_tables(L)
    n = 2 * L
    tf = _freq_tile(L)
    return pl.pallas_call(
        _ktab_kernel,
        grid=(n // tf,),
        in_specs=[pl.BlockSpec((tf, L), lambda j: (j, 0)),
                  pl.BlockSpec((L, 2 * HY_WIDTH), lambda j: (0, 0)),
                  pl.BlockSpec((tf, 1), lambda j: (j, 0)),
                  pl.BlockSpec((tf, 1), lambda j: (j, 0))],
        out_specs=pl.BlockSpec((tf, HY_WIDTH), lambda j: (j, 0)),
        out_shape=jax.ShapeDtypeStruct((n, HY_WIDTH), F32),
        compiler_params=_cparams("parallel"),
        name="hyena_ktab",
    )(jnp.asarray(fm, dtype=BF16), filt, jnp.asarray(scale), jnp.asarray(sign))


def _hyena_conv_kernel(z_ref, x0_ref, f_ref, ft_ref, k_ref, skip_ref, o_ref, zb_scr, acc_scr, *, tf):
    j = pl.program_id(1)
    half = tf // 2

    @pl.when(j == 0)
    def _():
        zb_scr[...] = z_ref[...].astype(BF16)
        acc_scr[...] = jnp.zeros_like(acc_scr)

    zf = _dot(f_ref[...], zb_scr[...])
    re, im = zf[:half], zf[half:]
    kre, kim = k_ref[:half, :], k_ref[half:, :]
    ridx = lax.broadcasted_iota(jnp.int32, re.shape, 0)
    mix = jnp.where(jnp.logical_and(j == 0, ridx == 0), 0.0, 1.0)
    yre = re * kre - mix * (im * kim)
    yim = mix * (re * kim) + im * jnp.where(mix == 0.0, kim, kre)
    y = jnp.concatenate([yre, yim], axis=0).astype(BF16)
    acc_scr[...] += _dot(ft_ref[...], y)

    @pl.when(j == pl.num_programs(1) - 1)
    def _():
        o_ref[...] = x0_ref[...] * (acc_scr[...] + z_ref[...] * skip_ref[...])


def _hyena_conv(z, x0c, ktab, skip, batch, L, blk0):
    fm, fmt, _, _ = _dft_tables(L)
    n = 2 * L
    tf = _freq_tile(L)
    hw = HY_WIDTH
    return pl.pallas_call(
        functools.partial(_hyena_conv_kernel, tf=tf),
        grid=(batch, n // tf),
        in_specs=[pl.BlockSpec((L, hw), lambda b, j: (blk0 + b, 0)),
                  pl.BlockSpec((L, hw), lambda b, j: (blk0 + b, 0)),
                  pl.BlockSpec((tf, L), lambda b, j: (j, 0)),
                  pl.BlockSpec((L, tf), lambda b, j: (0, j)),
                  pl.BlockSpec((tf, hw), lambda b, j: (j, 0)),
                  pl.BlockSpec((1, hw), lambda b, j: (0, 0))],
        out_specs=pl.BlockSpec((L, hw), lambda b, j: (b, 0)),
        out_shape=jax.ShapeDtypeStruct((batch * L, hw), F32),
        scratch_shapes=[pltpu.VMEM((L, hw), BF16), pltpu.VMEM((L, hw), F32)],
        compiler_params=_cparams("parallel", "arbitrary"),
        name="hyena_conv",
    )(z, x0c, jnp.asarray(fm, dtype=BF16), jnp.asarray(fmt, dtype=BF16), ktab, skip.reshape(1, hw))


def _mix_out_kernel(ac_ref, al_ref, bc_ref, bl_ref, wa_ref, wb_ref, h_ref, mod_ref, o_ref, *, nct):
    is_ctx = pl.program_id(0) < nct
    a = jnp.where(is_ctx, ac_ref[...], al_ref[...])
    b = jnp.where(is_ctx, bc_ref[...], bl_ref[...])
    y = _dot(a.astype(BF16), wa_ref[...]) + _dot(b.astype(BF16), wb_ref[...])
    o_ref[...] = h_ref[...] + mod_ref[2] * y


def _mix_out(a_ctx, a_lat, b_ctx, b_lat, w_out, h, mods_l, rows):
    d = h.shape[1]
    ka, kb = a_ctx.shape[1], b_ctx.shape[1]
    nct = rows.nct
    ctx_spec = lambda k: pl.BlockSpec((ROW_TILE, k), lambda i: (jnp.minimum(i, nct - 1), 0))
    lat_spec = lambda k: pl.BlockSpec((ROW_TILE, k), lambda i: (jnp.maximum(i - nct, 0), 0))
    return pl.pallas_call(
        functools.partial(_mix_out_kernel, nct=nct),
        grid=(rows.n_tiles,),
        in_specs=[ctx_spec(ka), lat_spec(ka), ctx_spec(kb), lat_spec(kb),
                  pl.BlockSpec((ka, d), lambda i: (0, 0)),
                  pl.BlockSpec((kb, d), lambda i: (ka // kb, 0)),
                  pl.BlockSpec((ROW_TILE, d), lambda i: (i, 0)),
                  pl.BlockSpec((None, 6, 1, d), lambda i: (rows.mod_row(i), 0, 0, 0))],
        out_specs=pl.BlockSpec((ROW_TILE, d), lambda i: (i, 0)),
        out_shape=jax.ShapeDtypeStruct((rows.n_tiles * ROW_TILE, d), F32),
        compiler_params=_cparams("parallel"),
        name="mix_out",
    )(a_ctx, a_lat, b_ctx, b_lat, w_out, w_out, h, mods_l)


def _shortconv_out_kernel(bg_ref, cg_ref, xi_ref, cgp_ref, cgn_ref, xip_ref, xin_ref, cw_ref, w_ref,
                          h_ref, mod_ref, o_ref, *, rows):
    first, last = rows.seq_edges(pl.program_id(0))
    keep_p = jnp.where(first, 0.0, 1.0)
    keep_n = jnp.where(last, 0.0, 1.0)
    m = cg_ref[...] * xi_ref[...]
    m_prev = cgp_ref[7:8, :] * xip_ref[7:8, :] * keep_p
    m_next = cgn_ref[0:1, :] * xin_ref[0:1, :] * keep_n
    y = bg_ref[...] * _conv3(m, m_prev, m_next, cw_ref)
    o_ref[...] = h_ref[...] + mod_ref[2] * _dot(y.astype(BF16), w_ref[...])


def _shortconv_out(p, conv_w, w_out, h, mods_l, rows):
    d = h.shape[1]
    t_rows = p.shape[0]
    in_specs = [pl.BlockSpec((ROW_TILE, d), lambda i, j=j: (i, j)) for j in range(3)]
    in_specs += [*_halo_specs(d, 1, t_rows), *_halo_specs(d, 2, t_rows),
                 pl.BlockSpec((3, d), lambda i: (0, 0)),
                 pl.BlockSpec((d, d), lambda i: (0, 0)),
                 pl.BlockSpec((ROW_TILE, d), lambda i: (i, 0)),
                 pl.BlockSpec((None, 6, 1, d), lambda i: (rows.mod_row(i), 0, 0, 0))]
    return pl.pallas_call(
        functools.partial(_shortconv_out_kernel, rows=rows),
        grid=(rows.n_tiles,),
        in_specs=in_specs,
        out_specs=pl.BlockSpec((ROW_TILE, d), lambda i: (i, 0)),
        out_shape=jax.ShapeDtypeStruct((rows.n_tiles * ROW_TILE, d), F32),
        compiler_params=_cparams("parallel"),
        name="shortconv_out",
    )(p, p, p, p, p, p, p, conv_w, w_out, h, mods_l)


META_E, META_RANK, META_GATE = 0, 2, 4


def _lane_min_index(mask, lane_f):
    return jnp.min(jnp.where(mask, lane_f, float(LANE)), axis=1, keepdims=True)


def _router_kernel(h_ref, g_ref, mod_ref, w_ref, b_ref, v_ref, meta_ref, cnt_ref, carry_scr):
    @pl.when(pl.program_id(0) == 0)
    def _():
        carry_scr[...] = jnp.zeros_like(carry_scr)

    v = _norm_mod(h_ref[...], g_ref[...], mod_ref[3], mod_ref[4])
    v_ref[...] = v
    lg = _dot3(v, w_ref[...]) + b_ref[...]
    tm = lg.shape[0]
    lane = lax.broadcasted_iota(jnp.int32, lg.shape, 1)
    lane_f = lane.astype(F32)
    neg = -jnp.inf

    is_grp = lane < N_GROUPS
    lgm = jnp.where(is_grp, lg, neg)
    m_g = jnp.max(lgm, axis=1, keepdims=True)
    s_g = jnp.sum(jnp.where(is_grp, jnp.exp(lg - m_g), 0.0), axis=1, keepdims=True)
    p_g = 1.0 / s_g
    grp = _lane_min_index(lgm == m_g, lane_f)

    ex_lane = lane - N_GROUPS
    in_grp = jnp.logical_and(jnp.logical_and(ex_lane >= 0, ex_lane < N_EXPERTS),
                             (ex_lane // EXP_PER_GROUP).astype(F32) == grp)
    m_e = jnp.max(jnp.where(in_grp, lg, neg), axis=1, keepdims=True)
    ee = jnp.where(in_grp, jnp.exp(lg - m_e), 0.0)
    pe = ee / jnp.sum(ee, axis=1, keepdims=True)
    pe1 = jnp.where(in_grp, pe, -1.0)
    p1 = jnp.max(pe1, axis=1, keepdims=True)
    i1 = _lane_min_index(pe1 == p1, lane_f)
    pe2 = jnp.where(lane_f == i1, -1.0, pe1)
    p2 = jnp.max(pe2, axis=1, keepdims=True)
    i2 = _lane_min_index(pe2 == p2, lane_f)
    denom = p1 + p2
    g1 = p_g * p1 / denom
    g2 = p_g * p2 / denom
    e1 = i1 - float(N_GROUPS)
    e2 = i2 - float(N_GROUPS)

    oh1 = lane_f == e1
    oh2 = lane_f == e2
    row = lax.broadcasted_iota(jnp.int32, (tm, tm), 0)
    col = lax.broadcasted_iota(jnp.int32, (tm, tm), 1)
    earlier = jnp.where(row > col, 1.0, 0.0).astype(BF16)
    c1 = _dot(earlier, jnp.where(oh1, 1.0, 0.0).astype(BF16))
    c2 = _dot(earlier, jnp.where(oh2, 1.0, 0.0).astype(BF16))
    tot1 = jnp.sum(jnp.where(oh1, 1.0, 0.0), axis=0, keepdims=True)
    tot2 = jnp.sum(jnp.where(oh2, 1.0, 0.0), axis=0, keepdims=True)
    carry = carry_scr[...]
    r1 = jnp.sum(jnp.where(oh1, carry + c1, 0.0), axis=1, keepdims=True)
    r2 = jnp.sum(jnp.where(oh2, carry + tot1 + c2, 0.0), axis=1, keepdims=True)
    carry = carry + tot1 + tot2
    carry_scr[...] = carry
    cnt_ref[...] = carry

    meta = jnp.zeros_like(lg)
    for idx, val in ((META_E, e1), (META_E + 1, e2), (META_RANK, r1), (META_RANK + 1, r2),
                     (META_GATE, g1), (META_GATE + 1, g2)):
        meta = jnp.where(lane == idx, val, meta)
    meta_ref[...] = meta


def _router(h, g, mods_l, w_rt, b_rt, rows):
    d = h.shape[1]
    nt = rows.n_tiles * ROW_TILE
    return pl.pallas_call(
        _router_kernel,
        grid=(rows.n_tiles,),
        in_specs=[pl.BlockSpec((ROW_TILE, d), lambda i: (i, 0)),
                  pl.BlockSpec((1, d), lambda i: (0, 0)),
                  pl.BlockSpec((None, 6, 1, d), lambda i: (rows.mod_row(i), 0, 0, 0)),
                  pl.BlockSpec((d, LANE), lambda i: (0, 0)),
                  pl.BlockSpec((1, LANE), lambda i: (0, 0))],
        out_specs=[pl.BlockSpec((ROW_TILE, d), lambda i: (i, 0)),
                   pl.BlockSpec((ROW_TILE, LANE), lambda i: (i, 0)),
                   pl.BlockSpec((1, LANE), lambda i: (0, 0))],
        out_shape=[jax.ShapeDtypeStruct((nt, d), F32), jax.ShapeDtypeStruct((nt, LANE), F32),
                   jax.ShapeDtypeStruct((1, LANE), F32)],
        scratch_shapes=[pltpu.VMEM((1, LANE), F32)],
        compiler_params=_cparams("arbitrary"),
        name="router",
    )(h, g.reshape(1, d), mods_l, w_rt, b_rt)


def _dispatch_plan(meta, counts):
    bm = MOE_ROWS
    t = meta.shape[0]
    n_blocks = -(-(t * TOP_K + N_EXPERTS * (bm - 1)) // bm)
    counts = counts[0, :N_EXPERTS].astype(jnp.int32)
    nblk = (counts + bm - 1) // bm
    blk_end = jnp.cumsum(nblk).astype(jnp.int32)
    slot0 = (blk_end - nblk) * bm
    expert = meta[:, META_E:META_E + TOP_K].astype(jnp.int32)
    rank = meta[:, META_RANK:META_RANK + TOP_K].astype(jnp.int32)
    onehot = expert[:, :, None] == jnp.arange(N_EXPERTS, dtype=jnp.int32)
    slot = rank + jnp.sum(jnp.where(onehot, slot0, 0), axis=-1)
    pad_lo = jnp.concatenate([slot0 + counts, blk_end[-1:] * bm])
    pad_hi = jnp.concatenate([blk_end * bm, jnp.full((1,), n_blocks * bm, jnp.int32)])
    src = _slot_tokens(jnp.concatenate([pad_lo, pad_hi]).astype(jnp.int32), _slot_tiles(slot), n_blocks)
    blk = jnp.arange(n_blocks, dtype=jnp.int32)
    blk_e = jnp.minimum(jnp.sum(blk[:, None] >= blk_end[None, :], axis=1), N_EXPERTS - 1).astype(jnp.int32)
    blk_used = (blk < blk_end[-1]).astype(jnp.int32)
    return slot, src.reshape(n_blocks, 1, bm), blk_e, blk_used


def _slot_tiles(slot):
    return slot.reshape(slot.shape[0] // ROW_TILE, ROW_TILE, TOP_K).transpose(0, 2, 1)


def _slot_tokens_kernel(pad_ref, slot_ref, src_ref):
    i = pl.program_id(0)
    tm = slot_ref.shape[1]
    n_ranges = pad_ref.shape[0] // 2

    @pl.when(i == 0)
    def _():
        def clear(j, c):
            src_ref[j] = 0
            return c

        for q in range(n_ranges):
            lax.fori_loop(pad_ref[q], pad_ref[n_ranges + q], clear, 0)

    base = i * tm
    for r in range(tm):
        for k in range(TOP_K):
            src_ref[slot_ref[k, r]] = base + r


def _slot_tokens(pad_ranges, slot_tiles, n_blocks):
    n = n_blocks * MOE_ROWS
    grid_spec = pltpu.PrefetchScalarGridSpec(
        num_scalar_prefetch=1,
        grid=(slot_tiles.shape[0],),
        in_specs=[pl.BlockSpec((None, TOP_K, ROW_TILE), lambda i, pad: (i, 0, 0), memory_space=pltpu.SMEM)],
        out_specs=pl.BlockSpec((n,), lambda i, pad: (0,), memory_space=pltpu.SMEM),
    )
    return pl.pallas_call(
        _slot_tokens_kernel,
        grid_spec=grid_spec,
        out_shape=jax.ShapeDtypeStruct((n,), jnp.int32),
        compiler_params=_cparams("arbitrary"),
        name="slot_tokens",
    )(pad_ranges, slot_tiles)


def _row_copies_start(src_hbm, idx_ref, k, dst, sem):
    for r in range(dst.shape[0]):
        pltpu.async_copy(src_hbm.at[pl.ds(idx_ref[k, r], 1), :], dst.at[pl.ds(r, 1), :], sem,
                         priority=r % DMA_THREADS)


def _row_copies_wait(src_hbm, dst, sem):
    pltpu.make_async_copy(src_hbm.at[pl.ds(0, dst.shape[0]), :], dst, sem).wait()


def _expert_kernel(blk_e_ref, blk_used_ref, src_ref, src_next_ref, v_hbm, w1_ref, w3_ref, w2_ref, y_ref,
                   xbuf, w1_scr, w3_scr, w2_scr, sem):
    j = pl.program_id(0)
    nb = pl.num_programs(0)
    used = blk_used_ref[j] > 0
    new_expert = jnp.logical_or(j == 0, blk_e_ref[j] != blk_e_ref[jnp.maximum(j - 1, 0)])

    @pl.when(j == 0)
    def _():
        _row_copies_start(v_hbm, src_ref, 0, xbuf.at[0], sem.at[0])

    @pl.when(jnp.logical_and(used, new_expert))
    def _():
        w1_scr[...] = w1_ref[...].astype(BF16)
        w3_scr[...] = w3_ref[...].astype(BF16)
        w2_scr[...] = w2_ref[...].astype(BF16)

    @pl.when(used)
    def _():
        b = j % 2
        _row_copies_wait(v_hbm, xbuf.at[b], sem.at[b])
        x = xbuf[b].astype(BF16)
        _row_copies_start(v_hbm, src_next_ref, 0, xbuf.at[1 - b], sem.at[1 - b])
        hid = _silu(_dot(x, w1_scr[...])) * _dot(x, w3_scr[...])
        y_ref[...] = _dot(hid.astype(BF16), w2_scr[...])

        @pl.when(jnp.logical_or(j == nb - 1, blk_used_ref[jnp.minimum(j + 1, nb - 1)] == 0))
        def _():
            _row_copies_wait(v_hbm, xbuf.at[1 - b], sem.at[1 - b])

    @pl.when(jnp.logical_not(used))
    def _():
        y_ref[...] = jnp.zeros_like(y_ref)


def _experts(v, src, blk_e, blk_used, w1, w3, w2):
    t, d = v.shape
    n_blocks = src.shape[0]
    bm = MOE_ROWS
    de = w1.shape[2]

    def next_block(j, e, used):
        nxt = jnp.minimum(j + 1, n_blocks - 1)
        return jnp.where(used[nxt] > 0, nxt, j)

    grid_spec = pltpu.PrefetchScalarGridSpec(
        num_scalar_prefetch=2,
        grid=(n_blocks,),
        in_specs=[pl.BlockSpec((None, 1, bm), lambda j, e, u: (j, 0, 0), memory_space=pltpu.SMEM),
                  pl.BlockSpec((None, 1, bm), lambda j, e, u: (next_block(j, e, u), 0, 0), memory_space=pltpu.SMEM),
                  pl.BlockSpec(memory_space=pl.ANY),
                  pl.BlockSpec((None, d, de), lambda j, e, u: (e[j], 0, 0)),
                  pl.BlockSpec((None, d, de), lambda j, e, u: (e[j], 0, 0)),
                  pl.BlockSpec((None, de, d), lambda j, e, u: (e[j], 0, 0))],
        out_specs=pl.BlockSpec((bm, d), lambda j, e, u: (j, 0)),
        scratch_shapes=[pltpu.VMEM((2, bm, d), F32), pltpu.VMEM((d, de), BF16), pltpu.VMEM((d, de), BF16),
                        pltpu.VMEM((de, d), BF16), pltpu.SemaphoreType.DMA((2,))],
    )
    return pl.pallas_call(
        _expert_kernel,
        grid_spec=grid_spec,
        out_shape=jax.ShapeDtypeStruct((n_blocks * bm, d), F32),
        compiler_params=_cparams("arbitrary"),
        name="experts",
    )(blk_e, blk_used, src, src, v, w1, w3, w2)


def _combined_tile(slot_ref, slot_next_ref, h_ref, meta_ref, modp_ref, y_hbm, buf, sem, overlap_with):
    i = pl.program_id(0)
    n = pl.num_programs(0)

    @pl.when(i == 0)
    def _():
        for k in range(TOP_K):
            _row_copies_start(y_hbm, slot_ref, k, buf.at[0, k], sem.at[0])

    b = i % 2
    for k in range(TOP_K):
        _row_copies_wait(y_hbm, buf.at[b, k], sem.at[b])
    meta = meta_ref[...]
    f = meta[:, META_GATE:META_GATE + 1] * buf[b, 0] + meta[:, META_GATE + 1:META_GATE + 2] * buf[b, 1]
    hn = h_ref[...] + modp_ref[5] * f
    for k in range(TOP_K):
        _row_copies_start(y_hbm, slot_next_ref, k, buf.at[1 - b, k], sem.at[1 - b])
    overlap_with(hn)

    @pl.when(i == n - 1)
    def _():
        for k in range(TOP_K):
            _row_copies_wait(y_hbm, buf.at[1 - b, k], sem.at[1 - b])


def _combine_mm_kernel(slot_ref, slot_next_ref, h_ref, meta_ref, modp_ref, g_ref, mod_ref, w_ref, y_hbm,
                       hn_ref, o_ref, buf, sem):
    def project(hn):
        hn_ref[...] = hn
        u = _norm_mod(hn, g_ref[...], mod_ref[0], mod_ref[1])
        o_ref[...] = _dot(u.astype(BF16), w_ref[...])

    _combined_tile(slot_ref, slot_next_ref, h_ref, meta_ref, modp_ref, y_hbm, buf, sem, project)


def _combine_norm_kernel(slot_ref, slot_next_ref, h_ref, meta_ref, modp_ref, g_ref, y_hbm, o_ref, buf, sem):
    def finish(hn):
        o_ref[...] = hn * lax.rsqrt(jnp.mean(hn * hn, axis=-1, keepdims=True) + EPS) * g_ref[...]

    _combined_tile(slot_ref, slot_next_ref, h_ref, meta_ref, modp_ref, y_hbm, buf, sem, finish)


def _combine_then(h, ys, slot_tiles, meta, mods_prev, rows, g, mods_l=None, w=None, name="combine_norm", tile0=0):
    d = h.shape[1]
    n_tiles = rows.n_tiles
    slot_spec = lambda fn: pl.BlockSpec((None, TOP_K, ROW_TILE), lambda i: (tile0 + fn(i), 0, 0),
                                        memory_space=pltpu.SMEM)
    mod_spec = pl.BlockSpec((None, 6, 1, d), lambda i: (rows.mod_row(i), 0, 0, 0))
    tile_in = lambda width: pl.BlockSpec((ROW_TILE, width), lambda i: (tile0 + i, 0))
    tile = lambda width: pl.BlockSpec((ROW_TILE, width), lambda i: (i, 0))
    rows_shape = lambda width: jax.ShapeDtypeStruct((n_tiles * ROW_TILE, width), F32)
    in_specs = [slot_spec(lambda i: i), slot_spec(lambda i: jnp.minimum(i + 1, n_tiles - 1)),
                tile_in(d), tile_in(LANE), mod_spec, pl.BlockSpec((1, d), lambda i: (0, 0))]
    args = [slot_tiles, slot_tiles, h, meta, mods_prev, g.reshape(1, d)]
    if w is None:
        kern, out_specs, out_shape = _combine_norm_kernel, tile(d), rows_shape(d)
    else:
        nw = w.shape[1]
        in_specs += [mod_spec, pl.BlockSpec((d, nw), lambda i: (0, 0))]
        args += [mods_l, w]
        kern, out_specs, out_shape = _combine_mm_kernel, [tile(d), tile(nw)], [rows_shape(d), rows_shape(nw)]
    return pl.pallas_call(
        kern,
        grid=(n_tiles,),
        in_specs=in_specs + [pl.BlockSpec(memory_space=pl.ANY)],
        out_specs=out_specs,
        out_shape=out_shape,
        scratch_shapes=[pltpu.VMEM((2, TOP_K, ROW_TILE, d), F32), pltpu.SemaphoreType.DMA((2,))],
        compiler_params=_cparams("arbitrary"),
        name=name,
    )(*args, ys)


def _even_w_in(w):
    d = w.shape[0]
    n_main = 2 * GLA_QK + 2 * GLA_V
    ranks = w[:, n_main:n_main + 2 * GLA_RANK]
    hy = w[:, n_main + 2 * GLA_RANK:]
    pad = jnp.zeros((d, R_COLS - 2 * GLA_RANK), w.dtype)
    return jnp.concatenate([w[:, :n_main], hy, ranks, pad], axis=1).astype(BF16)


def _rank_proj(wa, first_row):
    return jnp.zeros((R_COLS, GLA_QK), F32).at[first_row:first_row + GLA_RANK].set(wa)


def _grid_transpose(h_lat, batch, a, b):
    d = h_lat.shape[1]
    return h_lat.reshape(batch, a, b, d).transpose(0, 2, 1, 3).reshape(-1, d)


def kernel(x, c, ctx, c_ctx, mod_w, mod_b, norm_mix, norm_ffn, norm_final, ev_w_in, ev_w_out, gla_wa_f, gla_ba_f, gla_wa_b, gla_ba_b, gla_norm, hy_conv_w, hy_conv_b, hy_w1, hy_b1, hy_f1, hy_w2, hy_b2, hy_f2, hy_w3, hy_skip, od_w_in, od_conv_w, od_w_out, rt_w_grp, rt_b_grp, rt_w_exp, rt_b_exp, ex_w1, ex_w3, ex_w2):
    batch, s, d = x.shape
    lc = ctx.shape[1]
    depth = mod_w.shape[0]
    tc, tl = batch * lc, batch * s
    assert lc % ROW_TILE == 0 and s % ROW_TILE == 0 and tc % s == 0 and s % GRID_W == 0
    assert lc % GLA_CHUNK == 0 and s % GLA_CHUNK == 0
    assert depth % 2 == 0
    grid_rows = s // GRID_W

    n_cond = -(-(batch + 1) // 8) * 8
    cond = jnp.concatenate([c, c_ctx[None], jnp.zeros((n_cond - batch - 1, d), F32)], axis=0)
    mods = _mods(cond, mod_w, mod_b).reshape(depth, n_cond, 6, 1, d)

    h = jnp.concatenate([ctx.reshape(tc, d), x.reshape(tl, d)], axis=0)
    col_major_now = False
    moe = None

    def lat_part(a, fn):
        return fn(a) if a.shape[0] == tl else jnp.concatenate([a[:tc], fn(a[tc:])], axis=0)

    for l in range(depth):
        i = l // 2
        even = l % 2 == 0
        ctx_out = l < depth - 1
        col_major = i % 2 == 1
        per_token = [h] if moe is None else [h, moe[1], moe[2]]
        if col_major != col_major_now:
            perm = ((lambda a: _grid_transpose(a, batch, grid_rows, GRID_W)) if col_major
                    else (lambda a: _grid_transpose(a, batch, GRID_W, grid_rows)))
            per_token = [lat_part(a, perm) for a in per_token]
            col_major_now = col_major
        h = per_token[0]
        arr_rows = _Rows(batch, lc, s, not ctx_out)
        tile0 = (h.shape[0] - arr_rows.n_tiles * ROW_TILE) // ROW_TILE
        mods_l = mods[l]
        w_in = _even_w_in(ev_w_in[i]) if even else od_w_in[i].astype(BF16)
        name = "even_in" if even else "odd_in"
        if moe is None:
            p = _normmod_mm(h, norm_mix[l], mods_l, w_in, arr_rows, 0, 1, name)
        else:
            h, p = _combine_then(h, moe[0], _slot_tiles(per_token[1]), per_token[2], moe[3], arr_rows,
                                 norm_mix[l], mods_l, w_in, name, tile0)

        if even:
            waf = _rank_proj(gla_wa_f[i], 0)
            wab = _rank_proj(gla_wa_b[i], GLA_RANK)
            baf, bab = gla_ba_f[i].reshape(1, -1), gla_ba_b[i].reshape(1, -1)
            gain = gla_norm[i].reshape(1, -1)
            zeros = jnp.zeros((batch, 2, 2 * GLA_DV, 2 * GLA_DK), F32)
            filt_args = (hy_w1[i], hy_b1[i], hy_f1[i], hy_w2[i], hy_b2[i], hy_f2[i], hy_w3[i])
            x0c, z = _hyena_pre(p, hy_conv_w[i], hy_conv_b[i], arr_rows)
            gla_c, sc_f, sc_b = _gla(p, waf, wab, baf, bab, gain, zeros, zeros, batch, lc, 0)
            gla_l, _, _ = _gla(p, waf, wab, baf, bab, gain, sc_f, sc_b, batch, s, tc // s)
            kt_c = _hyena_ktab(_hyena_filters(lc, *filt_args), lc)
            kt_l = _hyena_ktab(_hyena_filters(s, *filt_args), s)
            hy_c = _hyena_conv(z, x0c, kt_c, hy_skip[i], batch, lc, 0)
            hy_l = _hyena_conv(z, x0c, kt_l, hy_skip[i], batch, s, tc // s)
            h = _mix_out(gla_c, gla_l, hy_c, hy_l, ev_w_out[i].astype(BF16), h, mods_l, arr_rows)
        else:
            h = _shortconv_out(p, od_conv_w[i], od_w_out[i].astype(BF16), h, mods_l, arr_rows)

        w_rt = jnp.concatenate([rt_w_grp[l], rt_w_exp[l],
                                jnp.zeros((d, LANE - N_GROUPS - N_EXPERTS), F32)], axis=1)
        b_rt = jnp.concatenate([rt_b_grp[l], rt_b_exp[l],
                                jnp.zeros((LANE - N_GROUPS - N_EXPERTS,), F32)]).reshape(1, LANE)
        v, meta, counts = _router(h, norm_ffn[l], mods_l, w_rt, b_rt, arr_rows)
        slot, src, blk_e, blk_used = _dispatch_plan(meta, counts)
        ys = _experts(v, src, blk_e, blk_used, ex_w1[l], ex_w3[l], ex_w2[l])
        moe = (ys, slot, meta, mods_l)

    out = _combine_then(h, moe[0], _slot_tiles(moe[1]), moe[2], moe[3], arr_rows, norm_final)
    if col_major_now:
        out = _grid_transpose(out, batch, GRID_W, grid_rows)
    return out.reshape(batch, s, d)
```

```python
import functools
import math

import numpy as np
import jax
import jax.numpy as jnp
from jax import lax
from jax.experimental import pallas as pl
from jax.experimental.pallas import tpu as pltpu

F32 = jnp.float32
BF16 = jnp.bfloat16

EPS = 1e-6
GRID_W = 64

GLA_HEADS = 4
GLA_DK = 64
GLA_DV = 128
GLA_RANK = 16
GLA_TAU = 16.0
GLA_CHUNK = 64
GLA_SLAB = 256
GLA_STEP_UNROLL = 4
GLA_QK = GLA_HEADS * GLA_DK
GLA_V = GLA_HEADS * GLA_DV

HY_WIDTH = 512
HY_EMB = 33
HY_BANDS = (HY_EMB - 1) // 2
HY_HIDDEN = 64
HY_FAST_DECAY = 0.3
HY_SLOW_DECAY = 1.5
HY_TARGET = 1e-2

N_GROUPS = 4
EXP_PER_GROUP = 8
N_EXPERTS = N_GROUPS * EXP_PER_GROUP
TOP_K = 2

LANE = 128
ROW_TILE = 256
MOE_ROWS = 256
ROUTER_TILE = 512
R_COLS = LANE
VMEM_LIMIT = 56 * 1024 * 1024
DMA_THREADS = 2


def _cparams(*sem):
    return pltpu.CompilerParams(dimension_semantics=sem, vmem_limit_bytes=VMEM_LIMIT)


def _split_bf16(a):
    hi = a.astype(BF16)
    lo = (a - hi.astype(F32)).astype(BF16)
    return hi, lo


def _dot(a, b):
    return jnp.dot(a, b, preferred_element_type=F32)


def _dot_nt(a, b):
    return lax.dot_general(a, b, (((1,), (1,)), ((), ())), preferred_element_type=F32)


def _dot_tn(a, b):
    return lax.dot_general(a, b, (((0,), (0,)), ((), ())), preferred_element_type=F32)


def _dot3(a, b):
    ah, al = _split_bf16(a)
    bh, bl = _split_bf16(b)
    return _dot(ah, bh) + _dot(ah, bl) + _dot(al, bh)


def _silu(x):
    return x / (1.0 + jnp.exp(-x))


def _log_sigmoid(x):
    return jnp.minimum(x, 0.0) - jnp.log1p(jnp.exp(-jnp.abs(x)))


def _norm_mod(x, g, shift, scale):
    y = x * lax.rsqrt(jnp.mean(x * x, axis=-1, keepdims=True) + EPS)
    return (y * g) * (1.0 + scale) + shift


def _mods_kernel(s_ref, w_ref, b_ref, o_ref):
    s = s_ref[...]
    s = _silu(s)
    o_ref[...] = _dot3(s, w_ref[...]) + b_ref[...]


def _mods(cond, mod_w, mod_b):
    depth, d, n = mod_w.shape
    r = cond.shape[0]
    tn = 1024
    return pl.pallas_call(
        _mods_kernel,
        grid=(depth, n // tn),
        in_specs=[pl.BlockSpec((r, d), lambda l, j: (0, 0)),
                  pl.BlockSpec((None, d, tn), lambda l, j: (l, 0, j)),
                  pl.BlockSpec((None, 1, tn), lambda l, j: (l, 0, j))],
        out_specs=pl.BlockSpec((None, r, tn), lambda l, j: (l, 0, j)),
        out_shape=jax.ShapeDtypeStruct((depth, r, n), F32),
        compiler_params=_cparams("parallel", "parallel"),
        name="mods",
    )(cond, mod_w, mod_b.reshape(depth, 1, n))


class _Rows:
    def __init__(self, batch, lc, s, lat_only, tile=ROW_TILE):
        self.batch = batch
        self.tile = tile
        self.nct = 0 if lat_only else batch * lc // tile
        self.tps = s // tile
        self.tpc = lc // tile
        self.n_tiles = self.nct + batch * self.tps

    def mod_row(self, i):
        return jnp.where(i < self.nct, self.batch, (i - self.nct) // self.tps)

    def seq_edges(self, i):
        pos_c = i % self.tpc
        pos_l = (i - self.nct) % self.tps
        is_c = i < self.nct
        first = jnp.where(is_c, pos_c == 0, pos_l == 0)
        last = jnp.where(is_c, pos_c == self.tpc - 1, pos_l == self.tps - 1)
        return first, last


def _stream_specs(stream, rows, width):
    _, _, lat0 = stream
    nct = rows.nct
    ctx_spec = pl.BlockSpec((ROW_TILE, width), lambda i: (jnp.clip(i, 0, max(nct - 1, 0)), 0))
    lat_spec = pl.BlockSpec((ROW_TILE, width), lambda i: (jnp.maximum(i - nct, 0) + lat0, 0))
    return [ctx_spec, lat_spec]


def _stream_tile(c_ref, l_ref, nct):
    if nct == 0:
        return l_ref[...]
    return jnp.where(pl.program_id(0) < nct, c_ref[...], l_ref[...])


def _as_stream(a, rows):
    return (a, a, rows.nct)


def _normmod_mm_kernel(hc_ref, hl_ref, g_ref, mod_ref, w_ref, o_ref, *, shift_i, scale_i, nct):
    u = _norm_mod(_stream_tile(hc_ref, hl_ref, nct), g_ref[...], mod_ref[shift_i], mod_ref[scale_i])
    o_ref[...] = _dot(u.astype(BF16), w_ref[...])


def _normmod_mm(h, g, mods_l, w, rows, shift_i, scale_i, name):
    d = w.shape[0]
    n = w.shape[1]
    return pl.pallas_call(
        functools.partial(_normmod_mm_kernel, shift_i=shift_i, scale_i=scale_i, nct=rows.nct),
        grid=(rows.n_tiles,),
        in_specs=[*_stream_specs(h, rows, d),
                  pl.BlockSpec((1, d), lambda i: (0, 0)),
                  pl.BlockSpec((None, 6, 1, d), lambda i: (rows.mod_row(i), 0, 0, 0)),
                  pl.BlockSpec((d, n), lambda i: (0, 0))],
        out_specs=pl.BlockSpec((ROW_TILE, n), lambda i: (i, 0)),
        out_shape=jax.ShapeDtypeStruct((rows.n_tiles * ROW_TILE, n), F32),
        compiler_params=_cparams("parallel"),
        name=name,
    )(h[0], h[1], g.reshape(1, d), mods_l, w)


def _gla_kernel(q_ref, k_ref, v_ref, g_ref, r_ref, waf_ref, wab_ref, baf_ref, bab_ref, gain_ref,
                s0f_ref, s0b_ref, o_ref, sf_ref, sb_ref, gl_scr, tot_scr, qd_scr, ds_scr, o_scr, st_scr,
                *, seq_len):
    C, SL = GLA_CHUNK, GLA_SLAB
    cps = SL // C
    n_chunks, n_slabs = seq_len // C, seq_len // SL
    head_of_lane = lax.broadcasted_iota(jnp.int32, (SL, 2 * GLA_DK), 1) // GLA_DK
    row = lax.broadcasted_iota(jnp.int32, (SL, SL), 0)
    col = lax.broadcasted_iota(jnp.int32, (SL, SL), 1)
    same_chunk = (row // C) == (col // C)
    srow = lax.broadcasted_iota(jnp.int32, (2 * GLA_DV, 2 * GLA_DK), 0) // GLA_DV
    scol = lax.broadcasted_iota(jnp.int32, (2 * GLA_DV, 2 * GLA_DK), 1) // GLA_DK
    same_head = srow == scol
    r = r_ref[...]

    def direction(wa_ref, ba_ref, s0_ref, s_out_ref, forward):
        mask = jnp.logical_and(same_chunk, (row >= col) if forward else (row <= col))
        tri = jnp.where(mask, 1.0, 0.0).astype(BF16)
        gl_scr[...] = _log_sigmoid(_dot3(r, wa_ref[...]) + ba_ref[...]) * (1.0 / GLA_TAU)

        def slab(s, carry):
            rows = pl.ds(pl.multiple_of(s * SL, SL), SL)
            gl = gl_scr[rows, :]
            g_hi = gl.astype(BF16)
            g_r1 = gl - g_hi.astype(F32)
            g_mid = g_r1.astype(BF16)
            g_lo = (g_r1 - g_mid.astype(F32)).astype(BF16)
            b = _dot(tri, g_hi) + _dot(tri, g_mid) + _dot(tri, g_lo)
            b3 = b.reshape(cps, C, 2 * GLA_DK)
            last = b3[:, C - 1:C, :] if forward else b3[:, 0:1, :]
            tot = jnp.broadcast_to(last, b3.shape).reshape(SL, 2 * GLA_DK)
            tot_scr[rows, :] = tot
            q = q_ref[rows, :] * (GLA_DK ** -0.5)
            k = k_ref[rows, :]
            v = v_ref[rows, :].astype(BF16)
            qd = q * jnp.exp(b)
            kd = (k * jnp.exp(-b)).astype(BF16)
            kr = (k * jnp.exp(tot - b)).astype(BF16)
            qd_scr[rows, :] = qd.astype(BF16)
            o_parts = []
            for h in range(2):
                qh = jnp.where(head_of_lane == h, qd, 0.0).astype(BF16)
                a = jnp.where(mask, _dot_nt(qh, kd), 0.0).astype(BF16)
                o_parts.append(_dot(a, v[:, h * GLA_DV:(h + 1) * GLA_DV]))
            o = jnp.concatenate(o_parts, axis=1)
            if forward:
                o_scr[rows, :] = o
            else:
                o_scr[rows, :] += o
            for c in range(cps):
                ds = _dot_tn(v[c * C:(c + 1) * C], kr[c * C:(c + 1) * C])
                ds_scr[s * cps + c] = jnp.where(same_head, ds, 0.0)
            return carry

        lax.fori_loop(0, n_slabs, slab, 0, unroll=2)
        st_scr[...] = s0_ref[...]

        def step(i, carry):
            c = i if forward else n_chunks - 1 - i
            first = pl.multiple_of(c * C, C)
            rows = pl.ds(first, C)
            st = st_scr[...]
            o_scr[rows, :] += _dot_nt(qd_scr[rows, :], st.astype(BF16))
            st_scr[...] = st * jnp.exp(tot_scr[pl.ds(first, 1), :]) + ds_scr[c]
            return carry

        lax.fori_loop(0, n_chunks, step, 0, unroll=GLA_STEP_UNROLL)
        s_out_ref[...] = st_scr[...]

    direction(waf_ref, baf_ref, s0f_ref, sf_ref, True)
    direction(wab_ref, bab_ref, s0b_ref, sb_ref, False)
    gain = gain_ref[...]

    def readout(s, carry):
        rows = pl.ds(pl.multiple_of(s * SL, SL), SL)
        o = o_scr[rows, :]
        outs = []
        for h in range(2):
            oh = o[:, h * GLA_DV:(h + 1) * GLA_DV]
            outs.append(oh * lax.rsqrt(jnp.mean(oh * oh, axis=-1, keepdims=True) + EPS) * gain)
        o_ref[rows, :] = jnp.concatenate(outs, axis=1) * _silu(g_ref[rows, :])
        return carry

    lax.fori_loop(0, n_slabs, readout, 0)


def _gla(p, waf, wab, baf, bab, gain, s0f, s0b, batch, seq_len, blk0):
    dk2, dv2 = 2 * GLA_DK, 2 * GLA_DV
    seq = lambda width, cb: pl.BlockSpec((seq_len, width), lambda b, hp: (blk0 + b, cb(hp)))
    state_spec = pl.BlockSpec((None, None, dv2, dk2), lambda b, hp: (b, hp, 0, 0))
    in_specs = [seq(dk2, lambda hp: hp),
                seq(dk2, lambda hp: GLA_QK // dk2 + hp),
                seq(dv2, lambda hp: 2 * GLA_QK // dv2 + hp),
                seq(dv2, lambda hp: (2 * GLA_QK + GLA_V) // dv2 + hp),
                seq(R_COLS, lambda hp: (2 * GLA_QK + 2 * GLA_V + 3 * HY_WIDTH) // R_COLS),
                pl.BlockSpec((R_COLS, dk2), lambda b, hp: (0, hp)),
                pl.BlockSpec((R_COLS, dk2), lambda b, hp: (0, hp)),
                pl.BlockSpec((1, dk2), lambda b, hp: (0, hp)),
                pl.BlockSpec((1, dk2), lambda b, hp: (0, hp)),
                pl.BlockSpec((1, GLA_DV), lambda b, hp: (0, 0)),
                state_spec, state_spec]
    state_shape = jax.ShapeDtypeStruct((batch, 2, dv2, dk2), F32)
    return pl.pallas_call(
        functools.partial(_gla_kernel, seq_len=seq_len),
        grid=(batch, 2),
        in_specs=in_specs,
        out_specs=[pl.BlockSpec((seq_len, dv2), lambda b, hp: (b, hp)), state_spec, state_spec],
        out_shape=[jax.ShapeDtypeStruct((batch * seq_len, GLA_V), F32), state_shape, state_shape],
        scratch_shapes=[pltpu.VMEM((seq_len, dk2), F32),
                        pltpu.VMEM((seq_len, dk2), F32),
                        pltpu.VMEM((seq_len, dk2), BF16),
                        pltpu.VMEM((seq_len // GLA_CHUNK, dv2, dk2), F32),
                        pltpu.VMEM((seq_len, dv2), F32),
                        pltpu.VMEM((dv2, dk2), F32)],
        compiler_params=_cparams("parallel", "parallel"),
        name="gla",
    )(p, p, p, p, p, waf, wab, baf, bab, gain, s0f, s0b)


def _conv3(m, prev_row, next_row, w_ref):
    n = m.shape[0]
    ridx = lax.broadcasted_iota(jnp.int32, m.shape, 0)
    m_prev = jnp.where(ridx == 0, prev_row, pltpu.roll(m, 1, 0))
    m_next = jnp.where(ridx == n - 1, next_row, pltpu.roll(m, n - 1, 0))
    return w_ref[0:1, :] * m_prev + w_ref[1:2, :] * m + w_ref[2:3, :] * m_next


def _halo_specs(width, col_block, t_rows):
    g = ROW_TILE // 8
    last = t_rows // 8 - 1
    prev = pl.BlockSpec((8, width), lambda i: (jnp.maximum(i * g - 1, 0), col_block))
    nxt = pl.BlockSpec((8, width), lambda i: (jnp.minimum((i + 1) * g, last), col_block))
    return prev, nxt


def _hyena_pre_kernel(x0_ref, x1_ref, v_ref, x0p_ref, x0n_ref, x1p_ref, x1n_ref, vp_ref, vn_ref,
                      w_ref, b_ref, x0c_ref, z_ref, *, rows):
    first, last = rows.seq_edges(pl.program_id(0))
    keep_p = jnp.where(first, 0.0, 1.0)
    keep_n = jnp.where(last, 0.0, 1.0)
    hw = HY_WIDTH

    def conv(ref, p_ref, n_ref, j):
        w = w_ref.at[:, j * hw:(j + 1) * hw]
        y = _conv3(ref[...], p_ref[7:8, :] * keep_p, n_ref[0:1, :] * keep_n, w)
        return y + b_ref[:, j * hw:(j + 1) * hw]

    x0c_ref[...] = conv(x0_ref, x0p_ref, x0n_ref, 0)
    z_ref[...] = conv(v_ref, vp_ref, vn_ref, 2) * conv(x1_ref, x1p_ref, x1n_ref, 1)


def _hyena_pre(p, conv_w, conv_b, rows):
    t_rows = p.shape[0]
    hw = HY_WIDTH
    cb0 = (2 * GLA_QK + 2 * GLA_V) // hw
    in_specs = [pl.BlockSpec((ROW_TILE, hw), lambda i, j=j: (i, cb0 + j)) for j in range(3)]
    for j in range(3):
        in_specs.extend(_halo_specs(hw, cb0 + j, t_rows))
    in_specs += [pl.BlockSpec((3, 3 * hw), lambda i: (0, 0)), pl.BlockSpec((1, 3 * hw), lambda i: (0, 0))]
    out_spec = pl.BlockSpec((ROW_TILE, hw), lambda i: (i, 0))
    shape = jax.ShapeDtypeStruct((rows.n_tiles * ROW_TILE, hw), F32)
    return pl.pallas_call(
        functools.partial(_hyena_pre_kernel, rows=rows),
        grid=(rows.n_tiles,),
        in_specs=in_specs,
        out_specs=[out_spec, out_spec],
        out_shape=[shape, shape],
        compiler_params=_cparams("parallel"),
        name="hyena_pre",
    )(p, p, p, p, p, p, p, p, p, conv_w, conv_b.reshape(1, 3 * hw))


def _filter_kernel(z_ref, w1_ref, b1_ref, f1_ref, w2_ref, b2_ref, f2_ref, w3_ref, win_ref, o_ref):
    hh = jnp.sin(f1_ref[...] * (_dot3(z_ref[...], w1_ref[...]) + b1_ref[...]))
    hh = jnp.sin(f2_ref[...] * (_dot3(hh, w2_ref[...]) + b2_ref[...]))
    win = win_ref[...]
    o_ref[...] = _dot3(hh, w3_ref[...]) * jnp.concatenate([win, win], axis=1)


@functools.lru_cache(maxsize=None)
def _filter_features(L):
    t = np.linspace(0.0, 1.0, L, dtype=np.float32)[:, None]
    pos = np.arange(L, dtype=np.float32)[:, None]
    bands = np.linspace(1e-4, HY_BANDS - 1, HY_BANDS, dtype=np.float32)[None]
    ang = (np.float32(2.0 * math.pi / L) * pos * bands).astype(np.float32)
    z = np.concatenate([t, np.cos(ang), np.sin(ang)], axis=-1).astype(np.float32)
    z = np.pad(z, ((0, 0), (0, LANE - HY_EMB)))
    max_decay = math.log(HY_TARGET) / HY_FAST_DECAY
    min_decay = math.log(HY_TARGET) / HY_SLOW_DECAY
    deltas = np.linspace(min_decay, max_decay, HY_WIDTH, dtype=np.float32)
    window = np.exp(-t * np.abs(deltas)[None]).astype(np.float32)
    return z, window


def _hyena_filters(L, w1, b1, f1, w2, b2, f2, w3):
    z, window = _filter_features(L)
    w1p = jnp.pad(w1, ((0, LANE - HY_EMB), (0, 0)))
    tl = min(L, 512)
    full = lambda a: pl.BlockSpec(a.shape, lambda i: (0,) * a.ndim)
    row = lambda a: a.reshape(1, -1)
    ops = [w1p, row(b1), row(f1), w2, row(b2), row(f2), w3]
    return pl.pallas_call(
        _filter_kernel,
        grid=(L // tl,),
        in_specs=[pl.BlockSpec((tl, LANE), lambda i: (i, 0))] + [full(a) for a in ops]
                 + [pl.BlockSpec((tl, HY_WIDTH), lambda i: (i, 0))],
        out_specs=pl.BlockSpec((tl, 2 * HY_WIDTH), lambda i: (i, 0)),
        out_shape=jax.ShapeDtypeStruct((L, 2 * HY_WIDTH), F32),
        compiler_params=_cparams("parallel"),
        name="hyena_filters",
    )(jnp.asarray(z), *ops, jnp.asarray(window))


def _freq_tile(L):
    return min(2 * L, 512)


@functools.lru_cache(maxsize=None)
def _dft_tables(L):
    n = 2 * L
    tf = _freq_tile(L)
    half = tf // 2
    t = np.arange(L, dtype=np.int64)[None, :]
    fm = np.zeros((n, L), np.float64)
    scale = np.zeros((n, 1), np.float64)
    sign = np.zeros((n, 1), np.float64)
    for j in range(n // tf):
        k = (np.arange(half, dtype=np.int64) + j * half)[:, None]
        ang = 2.0 * np.pi * ((k * t) % n).astype(np.float64) / n
        fm[j * tf:j * tf + half] = np.cos(ang)
        fm[j * tf + half:(j + 1) * tf] = -np.sin(ang)
        scale[j * tf:(j + 1) * tf] = 2.0 / n
        sign[j * tf:j * tf + half] = 1.0
        sign[j * tf + half:(j + 1) * tf] = -1.0
    fm[half] = np.cos(np.pi * t[0])
    scale[0] = 1.0 / n
    scale[half] = 1.0 / n
    sign[half] = 1.0
    return (fm.astype(np.float32), np.ascontiguousarray(fm.T).astype(np.float32),
            scale.astype(np.float32), sign.astype(np.float32))


def _ktab_kernel(f_ref, h_ref, scale_ref, sign_ref, o_ref):
    hw = HY_WIDTH
    hh, hl = _split_bf16(h_ref[...])
    f = f_ref[...]
    kk = _dot(f, hh) + _dot(f, hl)
    o_ref[...] = scale_ref[...] * (kk[:, :hw] + sign_ref[...] * kk[:, hw:])


def _hyena_ktab(filt, L):
    fm, _, scale, sign = _dft_tables(L)
    n = 2 * L
    tf = _freq_tile(L)
    return pl.pallas_call(
        _ktab_kernel,
        grid=(n // tf,),
        in_specs=[pl.BlockSpec((tf, L), lambda j: (j, 0)),
                  pl.BlockSpec((L, 2 * HY_WIDTH), lambda j: (0, 0)),
                  pl.BlockSpec((tf, 1), lambda j: (j, 0)),
                  pl.BlockSpec((tf, 1), lambda j: (j, 0))],
        out_specs=pl.BlockSpec((tf, HY_WIDTH), lambda j: (j, 0)),
        out_shape=jax.ShapeDtypeStruct((n, HY_WIDTH), F32),
        compiler_params=_cparams("parallel"),
        name="hyena_ktab",
    )(jnp.asarray(fm, dtype=BF16), filt, jnp.asarray(scale), jnp.asarray(sign))


def _hyena_conv_kernel(z_ref, x0_ref, f_ref, ft_ref, k_ref, skip_ref, o_ref, zb_scr, acc_scr, *, tf):
    j = pl.program_id(1)
    half = tf // 2

    @pl.when(j == 0)
    def _():
        zb_scr[...] = z_ref[...].astype(BF16)
        acc_scr[...] = jnp.zeros_like(acc_scr)

    zf = _dot(f_ref[...], zb_scr[...])
    re, im = zf[:half], zf[half:]
    kre, kim = k_ref[:half, :], k_ref[half:, :]
    ridx = lax.broadcasted_iota(jnp.int32, re.shape, 0)
    mix = jnp.where(jnp.logical_and(j == 0, ridx == 0), 0.0, 1.0)
    yre = re * kre - mix * (im * kim)
    yim = mix * (re * kim) + im * jnp.where(mix == 0.0, kim, kre)
    y = jnp.concatenate([yre, yim], axis=0).astype(BF16)
    acc_scr[...] += _dot(ft_ref[...], y)

    @pl.when(j == pl.num_programs(1) - 1)
    def _():
        o_ref[...] = x0_ref[...] * (acc_scr[...] + z_ref[...] * skip_ref[...])


def _hyena_conv(z, x0c, ktab, skip, batch, L, blk0):
    fm, fmt, _, _ = _dft_tables(L)
    n = 2 * L
    tf = _freq_tile(L)
    hw = HY_WIDTH
    return pl.pallas_call(
        functools.partial(_hyena_conv_kernel, tf=tf),
        grid=(batch, n // tf),
        in_specs=[pl.BlockSpec((L, hw), lambda b, j: (blk0 + b, 0)),
                  pl.BlockSpec((L, hw), lambda b, j: (blk0 + b, 0)),
                  pl.BlockSpec((tf, L), lambda b, j: (j, 0)),
                  pl.BlockSpec((L, tf), lambda b, j: (0, j)),
                  pl.BlockSpec((tf, hw), lambda b, j: (j, 0)),
                  pl.BlockSpec((1, hw), lambda b, j: (0, 0))],
        out_specs=pl.BlockSpec((L, hw), lambda b, j: (b, 0)),
        out_shape=jax.ShapeDtypeStruct((batch * L, hw), F32),
        scratch_shapes=[pltpu.VMEM((L, hw), BF16), pltpu.VMEM((L, hw), F32)],
        compiler_params=_cparams("parallel", "arbitrary"),
        name="hyena_conv",
    )(z, x0c, jnp.asarray(fm, dtype=BF16), jnp.asarray(fmt, dtype=BF16), ktab, skip.reshape(1, hw))


def _mix_out_kernel(ac_ref, al_ref, bc_ref, bl_ref, wa_ref, wb_ref, hc_ref, hl_ref, mod_ref, o_ref, *, nct):
    a = _stream_tile(ac_ref, al_ref, nct)
    b = _stream_tile(bc_ref, bl_ref, nct)
    y = _dot(a.astype(BF16), wa_ref[...]) + _dot(b.astype(BF16), wb_ref[...])
    o_ref[...] = _stream_tile(hc_ref, hl_ref, nct) + mod_ref[2] * y


def _mix_out(a_ctx, a_lat, b_ctx, b_lat, w_out, h, mods_l, rows):
    d = w_out.shape[1]
    ka, kb = a_ctx.shape[1], b_ctx.shape[1]
    return pl.pallas_call(
        functools.partial(_mix_out_kernel, nct=rows.nct),
        grid=(rows.n_tiles,),
        in_specs=[*_stream_specs((a_ctx, a_lat, 0), rows, ka), *_stream_specs((b_ctx, b_lat, 0), rows, kb),
                  pl.BlockSpec((ka, d), lambda i: (0, 0)),
                  pl.BlockSpec((kb, d), lambda i: (ka // kb, 0)),
                  *_stream_specs(h, rows, d),
                  pl.BlockSpec((None, 6, 1, d), lambda i: (rows.mod_row(i), 0, 0, 0))],
        out_specs=pl.BlockSpec((ROW_TILE, d), lambda i: (i, 0)),
        out_shape=jax.ShapeDtypeStruct((rows.n_tiles * ROW_TILE, d), F32),
        compiler_params=_cparams("parallel"),
        name="mix_out",
    )(a_ctx, a_lat, b_ctx, b_lat, w_out, w_out, h[0], h[1], mods_l)


def _shortconv_out_kernel(bg_ref, cg_ref, xi_ref, cgp_ref, cgn_ref, xip_ref, xin_ref, cw_ref, w_ref,
                          h_ref, mod_ref, o_ref, *, rows):
    first, last = rows.seq_edges(pl.program_id(0))
    keep_p = jnp.where(first, 0.0, 1.0)
    keep_n = jnp.where(last, 0.0, 1.0)
    m = cg_ref[...] * xi_ref[...]
    m_prev = cgp_ref[7:8, :] * xip_ref[7:8, :] * keep_p
    m_next = cgn_ref[0:1, :] * xin_ref[0:1, :] * keep_n
    y = bg_ref[...] * _conv3(m, m_prev, m_next, cw_ref)
    o_ref[...] = h_ref[...] + mod_ref[2] * _dot(y.astype(BF16), w_ref[...])


def _shortconv_out(p, conv_w, w_out, h, mods_l, rows):
    d = h.shape[1]
    t_rows = p.shape[0]
    in_specs = [pl.BlockSpec((ROW_TILE, d), lambda i, j=j: (i, j)) for j in range(3)]
    in_specs += [*_halo_specs(d, 1, t_rows), *_halo_specs(d, 2, t_rows),
                 pl.BlockSpec((3, d), lambda i: (0, 0)),
                 pl.BlockSpec((d, d), lambda i: (0, 0)),
                 pl.BlockSpec((ROW_TILE, d), lambda i: (i, 0)),
                 pl.BlockSpec((None, 6, 1, d), lambda i: (rows.mod_row(i), 0, 0, 0))]
    return pl.pallas_call(
        functools.partial(_shortconv_out_kernel, rows=rows),
        grid=(rows.n_tiles,),
        in_specs=in_specs,
        out_specs=pl.BlockSpec((ROW_TILE, d), lambda i: (i, 0)),
        out_shape=jax.ShapeDtypeStruct((rows.n_tiles * ROW_TILE, d), F32),
        compiler_params=_cparams("parallel"),
        name="shortconv_out",
    )(p, p, p, p, p, p, p, conv_w, w_out, h, mods_l)


META_E, META_RANK, META_GATE = 0, 2, 4


def _lane_min_index(mask, lane_f):
    return jnp.min(jnp.where(mask, lane_f, float(LANE)), axis=1, keepdims=True)


def _router_kernel(h_ref, g_ref, mod_ref, w_ref, b_ref, v_ref, meta_ref, cnt_ref, carry_scr):
    @pl.when(pl.program_id(0) == 0)
    def _():
        carry_scr[...] = jnp.zeros_like(carry_scr)

    v = _norm_mod(h_ref[...], g_ref[...], mod_ref[3], mod_ref[4])
    v_ref[...] = v
    lg = _dot3(v, w_ref[...]) + b_ref[...]
    tm = lg.shape[0]
    lane = lax.broadcasted_iota(jnp.int32, lg.shape, 1)
    lane_f = lane.astype(F32)
    neg = -jnp.inf

    is_grp = lane < N_GROUPS
    lgm = jnp.where(is_grp, lg, neg)
    m_g = jnp.max(lgm, axis=1, keepdims=True)
    s_g = jnp.sum(jnp.where(is_grp, jnp.exp(lg - m_g), 0.0), axis=1, keepdims=True)
    p_g = 1.0 / s_g
    grp = _lane_min_index(lgm == m_g, lane_f)

    ex_lane = lane - N_GROUPS
    in_grp = jnp.logical_and(jnp.logical_and(ex_lane >= 0, ex_lane < N_EXPERTS),
                             (ex_lane // EXP_PER_GROUP).astype(F32) == grp)
    m_e = jnp.max(jnp.where(in_grp, lg, neg), axis=1, keepdims=True)
    ee = jnp.where(in_grp, jnp.exp(lg - m_e), 0.0)
    pe = ee / jnp.sum(ee, axis=1, keepdims=True)
    pe1 = jnp.where(in_grp, pe, -1.0)
    p1 = jnp.max(pe1, axis=1, keepdims=True)
    i1 = _lane_min_index(pe1 == p1, lane_f)
    pe2 = jnp.where(lane_f == i1, -1.0, pe1)
    p2 = jnp.max(pe2, axis=1, keepdims=True)
    i2 = _lane_min_index(pe2 == p2, lane_f)
    denom = p1 + p2
    g1 = p_g * p1 / denom
    g2 = p_g * p2 / denom
    e1 = i1 - float(N_GROUPS)
    e2 = i2 - float(N_GROUPS)

    oh1 = lane_f == e1
    oh2 = lane_f == e2
    row = lax.broadcasted_iota(jnp.int32, (tm, tm), 0)
    col = lax.broadcasted_iota(jnp.int32, (tm, tm), 1)
    earlier = jnp.where(row > col, 1.0, 0.0).astype(BF16)
    c1 = _dot(earlier, jnp.where(oh1, 1.0, 0.0).astype(BF16))
    c2 = _dot(earlier, jnp.where(oh2, 1.0, 0.0).astype(BF16))
    tot1 = jnp.sum(jnp.where(oh1, 1.0, 0.0), axis=0, keepdims=True)
    tot2 = jnp.sum(jnp.where(oh2, 1.0, 0.0), axis=0, keepdims=True)
    carry = carry_scr[...]
    r1 = jnp.sum(jnp.where(oh1, carry + c1, 0.0), axis=1, keepdims=True)
    r2 = jnp.sum(jnp.where(oh2, carry + tot1 + c2, 0.0), axis=1, keepdims=True)
    carry = carry + tot1 + tot2
    carry_scr[...] = carry
    cnt_ref[...] = carry

    meta = jnp.zeros_like(lg)
    for idx, val in ((META_E, e1), (META_E + 1, e2), (META_RANK, r1), (META_RANK + 1, r2),
                     (META_GATE, g1), (META_GATE + 1, g2)):
        meta = jnp.where(lane == idx, val, meta)
    meta_ref[...] = meta


def _router(h, g, mods_l, w_rt, b_rt, rows):
    d = h.shape[1]
    tm = rows.tile
    nt = rows.n_tiles * tm
    return pl.pallas_call(
        _router_kernel,
        grid=(rows.n_tiles,),
        in_specs=[pl.BlockSpec((tm, d), lambda i: (i, 0)),
                  pl.BlockSpec((1, d), lambda i: (0, 0)),
                  pl.BlockSpec((None, 6, 1, d), lambda i: (rows.mod_row(i), 0, 0, 0)),
                  pl.BlockSpec((d, LANE), lambda i: (0, 0)),
                  pl.BlockSpec((1, LANE), lambda i: (0, 0))],
        out_specs=[pl.BlockSpec((tm, d), lambda i: (i, 0)),
                   pl.BlockSpec((tm, LANE), lambda i: (i, 0)),
                   pl.BlockSpec((1, LANE), lambda i: (0, 0))],
        out_shape=[jax.ShapeDtypeStruct((nt, d), F32), jax.ShapeDtypeStruct((nt, LANE), F32),
                   jax.ShapeDtypeStruct((1, LANE), F32)],
        scratch_shapes=[pltpu.VMEM((1, LANE), F32)],
        compiler_params=_cparams("arbitrary"),
        name="router",
    )(h, g.reshape(1, d), mods_l, w_rt, b_rt)


def _dispatch_plan(meta, counts):
    bm = MOE_ROWS
    t = meta.shape[0]
    n_blocks = -(-(t * TOP_K + N_EXPERTS * (bm - 1)) // bm)
    counts = counts[0, :N_EXPERTS].astype(jnp.int32)
    nblk = (counts + bm - 1) // bm
    blk_end = jnp.cumsum(nblk).astype(jnp.int32)
    slot0 = (blk_end - nblk) * bm
    expert = meta[:, META_E:META_E + TOP_K].astype(jnp.int32)
    rank = meta[:, META_RANK:META_RANK + TOP_K].astype(jnp.int32)
    onehot = expert[:, :, None] == jnp.arange(N_EXPERTS, dtype=jnp.int32)
    slot = rank + jnp.sum(jnp.where(onehot, slot0, 0), axis=-1)
    pad_lo = jnp.concatenate([slot0 + counts, blk_end[-1:] * bm])
    pad_hi = jnp.concatenate([blk_end * bm, jnp.full((1,), n_blocks * bm, jnp.int32)])
    src = _slot_tokens(jnp.concatenate([pad_lo, pad_hi]).astype(jnp.int32), _slot_tiles(slot), n_blocks)
    blk = jnp.arange(n_blocks, dtype=jnp.int32)
    blk_e = jnp.minimum(jnp.sum(blk[:, None] >= blk_end[None, :], axis=1), N_EXPERTS - 1).astype(jnp.int32)
    blk_used = (blk < blk_end[-1]).astype(jnp.int32)
    return slot, src.reshape(n_blocks, 1, bm), blk_e, blk_used


def _slot_tiles(slot):
    return slot.reshape(slot.shape[0] // ROW_TILE, ROW_TILE, TOP_K).transpose(0, 2, 1)


def _slot_tokens_kernel(pad_ref, slot_ref, src_ref):
    i = pl.program_id(0)
    tm = slot_ref.shape[1]
    n_ranges = pad_ref.shape[0] // 2

    @pl.when(i == 0)
    def _():
        def clear(j, c):
            src_ref[j] = 0
            return c

        for q in range(n_ranges):
            lax.fori_loop(pad_ref[q], pad_ref[n_ranges + q], clear, 0)

    base = i * tm
    for r in range(tm):
        for k in range(TOP_K):
            src_ref[slot_ref[k, r]] = base + r


def _slot_tokens(pad_ranges, slot_tiles, n_blocks):
    n = n_blocks * MOE_ROWS
    grid_spec = pltpu.PrefetchScalarGridSpec(
        num_scalar_prefetch=1,
        grid=(slot_tiles.shape[0],),
        in_specs=[pl.BlockSpec((None, TOP_K, ROW_TILE), lambda i, pad: (i, 0, 0), memory_space=pltpu.SMEM)],
        out_specs=pl.BlockSpec((n,), lambda i, pad: (0,), memory_space=pltpu.SMEM),
    )
    return pl.pallas_call(
        _slot_tokens_kernel,
        grid_spec=grid_spec,
        out_shape=jax.ShapeDtypeStruct((n,), jnp.int32),
        compiler_params=_cparams("arbitrary"),
        name="slot_tokens",
    )(pad_ranges, slot_tiles)


def _row_copies_start(src_hbm, idx_ref, k, dst, sem):
    for r in range(dst.shape[0]):
        pltpu.async_copy(src_hbm.at[pl.ds(idx_ref[k, r], 1), :], dst.at[pl.ds(r, 1), :], sem,
                         priority=r % DMA_THREADS)


def _row_copies_wait(src_hbm, dst, sem):
    pltpu.make_async_copy(src_hbm.at[pl.ds(0, dst.shape[0]), :], dst, sem).wait()


def _expert_kernel(blk_e_ref, blk_used_ref, src_ref, src_next_ref, v_hbm, w1_ref, w3_ref, w2_ref, y_ref,
                   xbuf, w1_scr, w3_scr, w2_scr, sem):
    j = pl.program_id(0)
    nb = pl.num_programs(0)
    used = blk_used_ref[j] > 0
    new_expert = jnp.logical_or(j == 0, blk_e_ref[j] != blk_e_ref[jnp.maximum(j - 1, 0)])

    @pl.when(j == 0)
    def _():
        _row_copies_start(v_hbm, src_ref, 0, xbuf.at[0], sem.at[0])

    @pl.when(jnp.logical_and(used, new_expert))
    def _():
        w1_scr[...] = w1_ref[...].astype(BF16)
        w3_scr[...] = w3_ref[...].astype(BF16)
        w2_scr[...] = w2_ref[...].astype(BF16)

    @pl.when(used)
    def _():
        b = j % 2
        _row_copies_wait(v_hbm, xbuf.at[b], sem.at[b])
        x = xbuf[b].astype(BF16)
        _row_copies_start(v_hbm, src_next_ref, 0, xbuf.at[1 - b], sem.at[1 - b])
        hid = _silu(_dot(x, w1_scr[...])) * _dot(x, w3_scr[...])
        y_ref[...] = _dot(hid.astype(BF16), w2_scr[...])

        @pl.when(jnp.logical_or(j == nb - 1, blk_used_ref[jnp.minimum(j + 1, nb - 1)] == 0))
        def _():
            _row_copies_wait(v_hbm, xbuf.at[1 - b], sem.at[1 - b])

    @pl.when(jnp.logical_not(used))
    def _():
        y_ref[...] = jnp.zeros_like(y_ref)


def _experts(v, src, blk_e, blk_used, w1, w3, w2):
    t, d = v.shape
    n_blocks = src.shape[0]
    bm = MOE_ROWS
    de = w1.shape[2]

    def next_block(j, e, used):
        nxt = jnp.minimum(j + 1, n_blocks - 1)
        return jnp.where(used[nxt] > 0, nxt, j)

    grid_spec = pltpu.PrefetchScalarGridSpec(
        num_scalar_prefetch=2,
        grid=(n_blocks,),
        in_specs=[pl.BlockSpec((None, 1, bm), lambda j, e, u: (j, 0, 0), memory_space=pltpu.SMEM),
                  pl.BlockSpec((None, 1, bm), lambda j, e, u: (next_block(j, e, u), 0, 0), memory_space=pltpu.SMEM),
                  pl.BlockSpec(memory_space=pl.ANY),
                  pl.BlockSpec((None, d, de), lambda j, e, u: (e[j], 0, 0)),
                  pl.BlockSpec((None, d, de), lambda j, e, u: (e[j], 0, 0)),
                  pl.BlockSpec((None, de, d), lambda j, e, u: (e[j], 0, 0))],
        out_specs=pl.BlockSpec((bm, d), lambda j, e, u: (j, 0)),
        scratch_shapes=[pltpu.VMEM((2, bm, d), F32), pltpu.VMEM((d, de), BF16), pltpu.VMEM((d, de), BF16),
                        pltpu.VMEM((de, d), BF16), pltpu.SemaphoreType.DMA((2,))],
    )
    return pl.pallas_call(
        _expert_kernel,
        grid_spec=grid_spec,
        out_shape=jax.ShapeDtypeStruct((n_blocks * bm, d), F32),
        compiler_params=_cparams("arbitrary"),
        name="experts",
    )(blk_e, blk_used, src, src, v, w1, w3, w2)


def _combined_tile(slot_ref, slot_next_ref, hc_ref, hl_ref, nct, meta_ref, modp_ref, y_hbm, buf, sem, overlap_with):
    i = pl.program_id(0)
    n = pl.num_programs(0)

    @pl.when(i == 0)
    def _():
        for k in range(TOP_K):
            _row_copies_start(y_hbm, slot_ref, k, buf.at[0, k], sem.at[0])

    b = i % 2
    for k in range(TOP_K):
        _row_copies_wait(y_hbm, buf.at[b, k], sem.at[b])
    meta = meta_ref[...]
    f = meta[:, META_GATE:META_GATE + 1] * buf[b, 0] + meta[:, META_GATE + 1:META_GATE + 2] * buf[b, 1]
    hn = _stream_tile(hc_ref, hl_ref, nct) + modp_ref[5] * f
    for k in range(TOP_K):
        _row_copies_start(y_hbm, slot_next_ref, k, buf.at[1 - b, k], sem.at[1 - b])
    overlap_with(hn)

    @pl.when(i == n - 1)
    def _():
        for k in range(TOP_K):
            _row_copies_wait(y_hbm, buf.at[1 - b, k], sem.at[1 - b])


def _combine_mm_kernel(slot_ref, slot_next_ref, hc_ref, hl_ref, meta_ref, modp_ref, g_ref, mod_ref, w_ref, y_hbm,
                       hn_ref, o_ref, buf, sem, *, nct):
    def project(hn):
        hn_ref[...] = hn
        u = _norm_mod(hn, g_ref[...], mod_ref[0], mod_ref[1])
        o_ref[...] = _dot(u.astype(BF16), w_ref[...])

    _combined_tile(slot_ref, slot_next_ref, hc_ref, hl_ref, nct, meta_ref, modp_ref, y_hbm, buf, sem, project)


def _combine_norm_kernel(slot_ref, slot_next_ref, hc_ref, hl_ref, meta_ref, modp_ref, g_ref, y_hbm, o_ref, buf, sem,
                         *, nct):
    def finish(hn):
        o_ref[...] = hn * lax.rsqrt(jnp.mean(hn * hn, axis=-1, keepdims=True) + EPS) * g_ref[...]

    _combined_tile(slot_ref, slot_next_ref, hc_ref, hl_ref, nct, meta_ref, modp_ref, y_hbm, buf, sem, finish)


def _combine_then(h, ys, slot_tiles, meta, mods_prev, rows, g, mods_l=None, w=None, name="combine_norm", tile0=0):
    d = g.shape[0]
    n_tiles = rows.n_tiles
    slot_spec = lambda fn: pl.BlockSpec((None, TOP_K, ROW_TILE), lambda i: (tile0 + fn(i), 0, 0),
                                        memory_space=pltpu.SMEM)
    mod_spec = pl.BlockSpec((None, 6, 1, d), lambda i: (rows.mod_row(i), 0, 0, 0))
    tile_in = lambda width: pl.BlockSpec((ROW_TILE, width), lambda i: (tile0 + i, 0))
    tile = lambda width: pl.BlockSpec((ROW_TILE, width), lambda i: (i, 0))
    rows_shape = lambda width: jax.ShapeDtypeStruct((n_tiles * ROW_TILE, width), F32)
    in_specs = [slot_spec(lambda i: i), slot_spec(lambda i: jnp.minimum(i + 1, n_tiles - 1)),
                *_stream_specs(h, rows, d), tile_in(LANE), mod_spec, pl.BlockSpec((1, d), lambda i: (0, 0))]
    args = [slot_tiles, slot_tiles, h[0], h[1], meta, mods_prev, g.reshape(1, d)]
    if w is None:
        kern, out_specs, out_shape = _combine_norm_kernel, tile(d), rows_shape(d)
    else:
        nw = w.shape[1]
        in_specs += [mod_spec, pl.BlockSpec((d, nw), lambda i: (0, 0))]
        args += [mods_l, w]
        kern, out_specs, out_shape = _combine_mm_kernel, [tile(d), tile(nw)], [rows_shape(d), rows_shape(nw)]
    return pl.pallas_call(
        functools.partial(kern, nct=rows.nct),
        grid=(n_tiles,),
        in_specs=in_specs + [pl.BlockSpec(memory_space=pl.ANY)],
        out_specs=out_specs,
        out_shape=out_shape,
        scratch_shapes=[pltpu.VMEM((2, TOP_K, ROW_TILE, d), F32), pltpu.SemaphoreType.DMA((2,))],
        compiler_params=_cparams("arbitrary"),
        name=name,
    )(*args, ys)


def _even_w_in(w):
    d = w.shape[0]
    n_main = 2 * GLA_QK + 2 * GLA_V
    ranks = w[:, n_main:n_main + 2 * GLA_RANK]
    hy = w[:, n_main + 2 * GLA_RANK:]
    pad = jnp.zeros((d, R_COLS - 2 * GLA_RANK), w.dtype)
    return jnp.concatenate([w[:, :n_main], hy, ranks, pad], axis=1).astype(BF16)


def _rank_proj(wa, first_row):
    return jnp.zeros((R_COLS, GLA_QK), F32).at[first_row:first_row + GLA_RANK].set(wa)


def _grid_transpose(h_lat, batch, a, b):
    d = h_lat.shape[1]
    return h_lat.reshape(batch, a, b, d).transpose(0, 2, 1, 3).reshape(-1, d)


def kernel(x, c, ctx, c_ctx, mod_w, mod_b, norm_mix, norm_ffn, norm_final, ev_w_in, ev_w_out, gla_wa_f, gla_ba_f, gla_wa_b, gla_ba_b, gla_norm, hy_conv_w, hy_conv_b, hy_w1, hy_b1, hy_f1, hy_w2, hy_b2, hy_f2, hy_w3, hy_skip, od_w_in, od_conv_w, od_w_out, rt_w_grp, rt_b_grp, rt_w_exp, rt_b_exp, ex_w1, ex_w3, ex_w2):
    batch, s, d = x.shape
    lc = ctx.shape[1]
    depth = mod_w.shape[0]
    tc, tl = batch * lc, batch * s
    assert lc % ROW_TILE == 0 and s % ROW_TILE == 0 and tc % s == 0 and s % GRID_W == 0
    assert lc % GLA_CHUNK == 0 and s % GLA_CHUNK == 0
    assert s % ROUTER_TILE == 0 and tc % ROUTER_TILE == 0
    assert depth % 2 == 0
    grid_rows = s // GRID_W

    n_cond = -(-(batch + 1) // 8) * 8
    cond = jnp.concatenate([c, c_ctx[None], jnp.zeros((n_cond - batch - 1, d), F32)], axis=0)
    mods = _mods(cond, mod_w, mod_b).reshape(depth, n_cond, 6, 1, d)

    hs = (ctx.reshape(tc, d), x.reshape(tl, d), 0)
    col_major_now = False
    moe = None

    def lat_part(a, fn):
        return fn(a) if a.shape[0] == tl else jnp.concatenate([a[:tc], fn(a[tc:])], axis=0)

    for l in range(depth):
        i = l // 2
        even = l % 2 == 0
        ctx_out = l < depth - 1
        col_major = i % 2 == 1
        routing = [] if moe is None else [moe[1], moe[2]]
        if col_major != col_major_now:
            perm = ((lambda a: _grid_transpose(a, batch, grid_rows, GRID_W)) if col_major
                    else (lambda a: _grid_transpose(a, batch, GRID_W, grid_rows)))
            lat = hs[1][hs[2] * ROW_TILE:hs[2] * ROW_TILE + tl]
            hs = (hs[0], perm(lat), 0)
            routing = [lat_part(a, perm) for a in routing]
            col_major_now = col_major
        arr_rows = _Rows(batch, lc, s, not ctx_out)
        if not ctx_out:
            hs = (hs[1], hs[1], hs[2])
        mods_l = mods[l]
        w_in = _even_w_in(ev_w_in[i]) if even else od_w_in[i].astype(BF16)
        name = "even_in" if even else "odd_in"
        if moe is None:
            p = _normmod_mm(hs, norm_mix[l], mods_l, w_in, arr_rows, 0, 1, name)
        else:
            tile0 = routing[0].shape[0] // ROW_TILE - arr_rows.n_tiles
            h, p = _combine_then(hs, moe[0], _slot_tiles(routing[0]), routing[1], moe[3], arr_rows,
                                 norm_mix[l], mods_l, w_in, name, tile0)
            hs = _as_stream(h, arr_rows)

        if even:
            waf = _rank_proj(gla_wa_f[i], 0)
            wab = _rank_proj(gla_wa_b[i], GLA_RANK)
            baf, bab = gla_ba_f[i].reshape(1, -1), gla_ba_b[i].reshape(1, -1)
            gain = gla_norm[i].reshape(1, -1)
            zeros = jnp.zeros((batch, 2, 2 * GLA_DV, 2 * GLA_DK), F32)
            filt_args = (hy_w1[i], hy_b1[i], hy_f1[i], hy_w2[i], hy_b2[i], hy_f2[i], hy_w3[i])
            x0c, z = _hyena_pre(p, hy_conv_w[i], hy_conv_b[i], arr_rows)
            gla_c, sc_f, sc_b = _gla(p, waf, wab, baf, bab, gain, zeros, zeros, batch, lc, 0)
            gla_l, _, _ = _gla(p, waf, wab, baf, bab, gain, sc_f, sc_b, batch, s, tc // s)
            kt_c = _hyena_ktab(_hyena_filters(lc, *filt_args), lc)
            kt_l = _hyena_ktab(_hyena_filters(s, *filt_args), s)
            hy_c = _hyena_conv(z, x0c, kt_c, hy_skip[i], batch, lc, 0)
            hy_l = _hyena_conv(z, x0c, kt_l, hy_skip[i], batch, s, tc // s)
            h = _mix_out(gla_c, gla_l, hy_c, hy_l, ev_w_out[i].astype(BF16), hs, mods_l, arr_rows)
        else:
            h = _shortconv_out(p, od_conv_w[i], od_w_out[i].astype(BF16), h, mods_l, arr_rows)

        w_rt = jnp.concatenate([rt_w_grp[l], rt_w_exp[l],
                                jnp.zeros((d, LANE - N_GROUPS - N_EXPERTS), F32)], axis=1)
        b_rt = jnp.concatenate([rt_b_grp[l], rt_b_exp[l],
                                jnp.zeros((LANE - N_GROUPS - N_EXPERTS,), F32)]).reshape(1, LANE)
        v, meta, counts = _router(h, norm_ffn[l], mods_l, w_rt, b_rt,
                                  _Rows(batch, lc, s, not ctx_out, ROUTER_TILE))
        slot, src, blk_e, blk_used = _dispatch_plan(meta, counts)
        ys = _experts(v, src, blk_e, blk_used, ex_w1[l], ex_w3[l], ex_w2[l])
        moe = (ys, slot, meta, mods_l)
        hs = _as_stream(h, arr_rows)

    out = _combine_then(hs, moe[0], _slot_tiles(moe[1]), moe[2], moe[3], arr_rows, norm_final)
    if col_major_now:
        out = _grid_transpose(out, batch, GRID_W, grid_rows)
    return out.reshape(batch, s, d)
```

```python
import functools
import math

import numpy as np
import jax
import jax.numpy as jnp
from jax import lax
from jax.experimental import pallas as pl
from jax.experimental.pallas import tpu as pltpu

F32 = jnp.float32
BF16 = jnp.bfloat16

EPS = 1e-6
GRID_W = 64

GLA_HEADS = 4
GLA_DK = 64
GLA_DV = 128
GLA_RANK = 16
GLA_TAU = 16.0
GLA_CHUNK = 64
GLA_SLAB = 256
GLA_STEP_UNROLL = 4
GLA_QK = GLA_HEADS * GLA_DK
GLA_V = GLA_HEADS * GLA_DV

HY_WIDTH = 512
HY_EMB = 33
HY_BANDS = (HY_EMB - 1) // 2
HY_HIDDEN = 64
HY_FAST_DECAY = 0.3
HY_SLOW_DECAY = 1.5
HY_TARGET = 1e-2

N_GROUPS = 4
EXP_PER_GROUP = 8
N_EXPERTS = N_GROUPS * EXP_PER_GROUP
TOP_K = 2

LANE = 128
ROW_TILE = 256
MOE_ROWS = 256
ROUTER_TILE = 512
R_COLS = LANE
VMEM_LIMIT = 56 * 1024 * 1024
DMA_THREADS = 2


def _cparams(*sem):
    return pltpu.CompilerParams(dimension_semantics=sem, vmem_limit_bytes=VMEM_LIMIT)


def _split_bf16(a):
    hi = a.astype(BF16)
    lo = (a - hi.astype(F32)).astype(BF16)
    return hi, lo


def _dot(a, b):
    return jnp.dot(a, b, preferred_element_type=F32)


def _dot_nt(a, b):
    return lax.dot_general(a, b, (((1,), (1,)), ((), ())), preferred_element_type=F32)


def _dot_tn(a, b):
    return lax.dot_general(a, b, (((0,), (0,)), ((), ())), preferred_element_type=F32)


def _dot3(a, b):
    ah, al = _split_bf16(a)
    bh, bl = _split_bf16(b)
    return _dot(ah, bh) + _dot(ah, bl) + _dot(al, bh)


def _silu(x):
    return x / (1.0 + jnp.exp(-x))


def _log_sigmoid(x):
    return jnp.minimum(x, 0.0) - jnp.log1p(jnp.exp(-jnp.abs(x)))


def _norm_mod(x, g, shift, scale):
    y = x * lax.rsqrt(jnp.mean(x * x, axis=-1, keepdims=True) + EPS)
    return (y * g) * (1.0 + scale) + shift


def _mods_kernel(s_ref, w_ref, b_ref, o_ref):
    s = s_ref[...]
    s = _silu(s)
    o_ref[...] = _dot3(s, w_ref[...]) + b_ref[...]


def _mods(cond, mod_w, mod_b):
    depth, d, n = mod_w.shape
    r = cond.shape[0]
    tn = 1024
    return pl.pallas_call(
        _mods_kernel,
        grid=(depth, n // tn),
        in_specs=[pl.BlockSpec((r, d), lambda l, j: (0, 0)),
                  pl.BlockSpec((None, d, tn), lambda l, j: (l, 0, j)),
                  pl.BlockSpec((None, 1, tn), lambda l, j: (l, 0, j))],
        out_specs=pl.BlockSpec((None, r, tn), lambda l, j: (l, 0, j)),
        out_shape=jax.ShapeDtypeStruct((depth, r, n), F32),
        compiler_params=_cparams("parallel", "parallel"),
        name="mods",
    )(cond, mod_w, mod_b.reshape(depth, 1, n))


class _Rows:
    def __init__(self, batch, lc, s, lat_only, tile=ROW_TILE):
        self.batch = batch
        self.tile = tile
        self.nct = 0 if lat_only else batch * lc // tile
        self.tps = s // tile
        self.tpc = lc // tile
        self.n_tiles = self.nct + batch * self.tps

    def mod_row(self, i):
        return jnp.where(i < self.nct, self.batch, (i - self.nct) // self.tps)

    def seq_edges(self, i):
        pos_c = i % self.tpc
        pos_l = (i - self.nct) % self.tps
        is_c = i < self.nct
        first = jnp.where(is_c, pos_c == 0, pos_l == 0)
        last = jnp.where(is_c, pos_c == self.tpc - 1, pos_l == self.tps - 1)
        return first, last


def _stream_specs(stream, rows, width):
    _, _, lat0 = stream
    nct = rows.nct
    ctx_spec = pl.BlockSpec((ROW_TILE, width), lambda i: (jnp.clip(i, 0, max(nct - 1, 0)), 0))
    lat_spec = pl.BlockSpec((ROW_TILE, width), lambda i: (jnp.maximum(i - nct, 0) + lat0, 0))
    return [ctx_spec, lat_spec]


def _stream_tile(c_ref, l_ref, nct):
    if nct == 0:
        return l_ref[...]
    return jnp.where(pl.program_id(0) < nct, c_ref[...], l_ref[...])


def _as_stream(a, rows):
    return (a, a, rows.nct)


def _normmod_mm_kernel(hc_ref, hl_ref, g_ref, mod_ref, w_ref, o_ref, *, shift_i, scale_i, nct):
    u = _norm_mod(_stream_tile(hc_ref, hl_ref, nct), g_ref[...], mod_ref[shift_i], mod_ref[scale_i])
    o_ref[...] = _dot(u.astype(BF16), w_ref[...]).astype(o_ref.dtype)


def _normmod_mm(h, g, mods_l, w, rows, shift_i, scale_i, name):
    d = w.shape[0]
    n = w.shape[1]
    return pl.pallas_call(
        functools.partial(_normmod_mm_kernel, shift_i=shift_i, scale_i=scale_i, nct=rows.nct),
        grid=(rows.n_tiles,),
        in_specs=[*_stream_specs(h, rows, d),
                  pl.BlockSpec((1, d), lambda i: (0, 0)),
                  pl.BlockSpec((None, 6, 1, d), lambda i: (rows.mod_row(i), 0, 0, 0)),
                  pl.BlockSpec((d, n), lambda i: (0, 0))],
        out_specs=pl.BlockSpec((ROW_TILE, n), lambda i: (i, 0)),
        out_shape=jax.ShapeDtypeStruct((rows.n_tiles * ROW_TILE, n), BF16),
        compiler_params=_cparams("parallel"),
        name=name,
    )(h[0], h[1], g.reshape(1, d), mods_l, w)


def _gla_kernel(q_ref, k_ref, v_ref, g_ref, r_ref, waf_ref, wab_ref, baf_ref, bab_ref, gain_ref,
                s0f_ref, s0b_ref, o_ref, sf_ref, sb_ref, gl_scr, tot_scr, qd_scr, ds_scr, o_scr, st_scr,
                *, seq_len):
    C, SL = GLA_CHUNK, GLA_SLAB
    cps = SL // C
    n_chunks, n_slabs = seq_len // C, seq_len // SL
    head_of_lane = lax.broadcasted_iota(jnp.int32, (SL, 2 * GLA_DK), 1) // GLA_DK
    row = lax.broadcasted_iota(jnp.int32, (SL, SL), 0)
    col = lax.broadcasted_iota(jnp.int32, (SL, SL), 1)
    same_chunk = (row // C) == (col // C)
    srow = lax.broadcasted_iota(jnp.int32, (2 * GLA_DV, 2 * GLA_DK), 0) // GLA_DV
    scol = lax.broadcasted_iota(jnp.int32, (2 * GLA_DV, 2 * GLA_DK), 1) // GLA_DK
    same_head = srow == scol
    r = r_ref[...].astype(F32)

    def direction(wa_ref, ba_ref, s0_ref, s_out_ref, forward):
        mask = jnp.logical_and(same_chunk, (row >= col) if forward else (row <= col))
        tri = jnp.where(mask, 1.0, 0.0).astype(BF16)
        gl_scr[...] = _log_sigmoid(_dot3(r, wa_ref[...]) + ba_ref[...]) * (1.0 / GLA_TAU)

        def slab(s, carry):
            rows = pl.ds(pl.multiple_of(s * SL, SL), SL)
            gl = gl_scr[rows, :]
            g_hi, g_lo = _split_bf16(gl)
            b = _dot(tri, g_hi) + _dot(tri, g_lo)
            b3 = b.reshape(cps, C, 2 * GLA_DK)
            last = b3[:, C - 1:C, :] if forward else b3[:, 0:1, :]
            tot = jnp.broadcast_to(last, b3.shape).reshape(SL, 2 * GLA_DK)
            tot_scr[rows, :] = tot
            q = q_ref[rows, :].astype(F32) * (GLA_DK ** -0.5)
            k = k_ref[rows, :].astype(F32)
            v = v_ref[rows, :].astype(BF16)
            qd = q * jnp.exp(b)
            kd = (k * jnp.exp(-b)).astype(BF16)
            kr = (k * jnp.exp(tot - b)).astype(BF16)
            qd_scr[rows, :] = qd.astype(BF16)
            o_parts = []
            for h in range(2):
                qh = jnp.where(head_of_lane == h, qd, 0.0).astype(BF16)
                a = jnp.where(mask, _dot_nt(qh, kd), 0.0).astype(BF16)
                o_parts.append(_dot(a, v[:, h * GLA_DV:(h + 1) * GLA_DV]))
            o = jnp.concatenate(o_parts, axis=1)
            if forward:
                o_scr[rows, :] = o
            else:
                o_scr[rows, :] += o
            for c in range(cps):
                ds = _dot_tn(v[c * C:(c + 1) * C], kr[c * C:(c + 1) * C])
                ds_scr[s * cps + c] = jnp.where(same_head, ds, 0.0)
            return carry

        lax.fori_loop(0, n_slabs, slab, 0, unroll=2)
        st_scr[...] = s0_ref[...]

        def step(i, carry):
            c = i if forward else n_chunks - 1 - i
            first = pl.multiple_of(c * C, C)
            rows = pl.ds(first, C)
            st = st_scr[...]
            o_scr[rows, :] += _dot_nt(qd_scr[rows, :], st.astype(BF16))
            st_scr[...] = st * jnp.exp(tot_scr[pl.ds(first, 1), :]) + ds_scr[c]
            return carry

        lax.fori_loop(0, n_chunks, step, 0, unroll=GLA_STEP_UNROLL)
        s_out_ref[...] = st_scr[...]

    direction(waf_ref, baf_ref, s0f_ref, sf_ref, True)
    direction(wab_ref, bab_ref, s0b_ref, sb_ref, False)
    gain = gain_ref[...]

    def readout(s, carry):
        rows = pl.ds(pl.multiple_of(s * SL, SL), SL)
        o = o_scr[rows, :]
        outs = []
        for h in range(2):
            oh = o[:, h * GLA_DV:(h + 1) * GLA_DV]
            outs.append(oh * lax.rsqrt(jnp.mean(oh * oh, axis=-1, keepdims=True) + EPS) * gain)
        o_ref[rows, :] = (jnp.concatenate(outs, axis=1) * _silu(g_ref[rows, :].astype(F32))).astype(o_ref.dtype)
        return carry

    lax.fori_loop(0, n_slabs, readout, 0)


def _gla(p, waf, wab, baf, bab, gain, s0f, s0b, batch, seq_len, blk0):
    dk2, dv2 = 2 * GLA_DK, 2 * GLA_DV
    seq = lambda width, cb: pl.BlockSpec((seq_len, width), lambda b, hp: (blk0 + b, cb(hp)))
    state_spec = pl.BlockSpec((None, None, dv2, dk2), lambda b, hp: (b, hp, 0, 0))
    in_specs = [seq(dk2, lambda hp: hp),
                seq(dk2, lambda hp: GLA_QK // dk2 + hp),
                seq(dv2, lambda hp: 2 * GLA_QK // dv2 + hp),
                seq(dv2, lambda hp: (2 * GLA_QK + GLA_V) // dv2 + hp),
                seq(R_COLS, lambda hp: (2 * GLA_QK + 2 * GLA_V + 3 * HY_WIDTH) // R_COLS),
                pl.BlockSpec((R_COLS, dk2), lambda b, hp: (0, hp)),
                pl.BlockSpec((R_COLS, dk2), lambda b, hp: (0, hp)),
                pl.BlockSpec((1, dk2), lambda b, hp: (0, hp)),
                pl.BlockSpec((1, dk2), lambda b, hp: (0, hp)),
                pl.BlockSpec((1, GLA_DV), lambda b, hp: (0, 0)),
                state_spec, state_spec]
    state_shape = jax.ShapeDtypeStruct((batch, 2, dv2, dk2), F32)
    return pl.pallas_call(
        functools.partial(_gla_kernel, seq_len=seq_len),
        grid=(batch, 2),
        in_specs=in_specs,
        out_specs=[pl.BlockSpec((seq_len, dv2), lambda b, hp: (b, hp)), state_spec, state_spec],
        out_shape=[jax.ShapeDtypeStruct((batch * seq_len, GLA_V), BF16), state_shape, state_shape],
        scratch_shapes=[pltpu.VMEM((seq_len, dk2), F32),
                        pltpu.VMEM((seq_len, dk2), F32),
                        pltpu.VMEM((seq_len, dk2), BF16),
                        pltpu.VMEM((seq_len // GLA_CHUNK, dv2, dk2), F32),
                        pltpu.VMEM((seq_len, dv2), F32),
                        pltpu.VMEM((dv2, dk2), F32)],
        compiler_params=_cparams("parallel", "parallel"),
        name="gla",
    )(p, p, p, p, p, waf, wab, baf, bab, gain, s0f, s0b)


def _conv3(m, prev_row, next_row, w_ref):
    n = m.shape[0]
    ridx = lax.broadcasted_iota(jnp.int32, m.shape, 0)
    m_prev = jnp.where(ridx == 0, prev_row, pltpu.roll(m, 1, 0))
    m_next = jnp.where(ridx == n - 1, next_row, pltpu.roll(m, n - 1, 0))
    return w_ref[0:1, :] * m_prev + w_ref[1:2, :] * m + w_ref[2:3, :] * m_next


HALO = 16


def _halo_specs(width, col_block, t_rows):
    g = ROW_TILE // HALO
    last = t_rows // HALO - 1
    prev = pl.BlockSpec((HALO, width), lambda i: (jnp.maximum(i * g - 1, 0), col_block))
    nxt = pl.BlockSpec((HALO, width), lambda i: (jnp.minimum((i + 1) * g, last), col_block))
    return prev, nxt


def _hyena_pre_kernel(x0_ref, x1_ref, v_ref, x0p_ref, x0n_ref, x1p_ref, x1n_ref, vp_ref, vn_ref,
                      w_ref, b_ref, x0c_ref, z_ref, *, rows):
    first, last = rows.seq_edges(pl.program_id(0))
    keep_p = jnp.where(first, 0.0, 1.0)
    keep_n = jnp.where(last, 0.0, 1.0)
    hw = HY_WIDTH

    def conv(ref, p_ref, n_ref, j):
        w = w_ref.at[:, j * hw:(j + 1) * hw]
        y = _conv3(ref[...].astype(F32), p_ref[HALO - 1:HALO, :].astype(F32) * keep_p,
                   n_ref[0:1, :].astype(F32) * keep_n, w)
        return y + b_ref[:, j * hw:(j + 1) * hw]

    x0c_ref[...] = conv(x0_ref, x0p_ref, x0n_ref, 0)
    z_ref[...] = conv(v_ref, vp_ref, vn_ref, 2) * conv(x1_ref, x1p_ref, x1n_ref, 1)


def _hyena_pre(p, conv_w, conv_b, rows):
    t_rows = p.shape[0]
    hw = HY_WIDTH
    cb0 = (2 * GLA_QK + 2 * GLA_V) // hw
    in_specs = [pl.BlockSpec((ROW_TILE, hw), lambda i, j=j: (i, cb0 + j)) for j in range(3)]
    for j in range(3):
        in_specs.extend(_halo_specs(hw, cb0 + j, t_rows))
    in_specs += [pl.BlockSpec((3, 3 * hw), lambda i: (0, 0)), pl.BlockSpec((1, 3 * hw), lambda i: (0, 0))]
    out_spec = pl.BlockSpec((ROW_TILE, hw), lambda i: (i, 0))
    shape = jax.ShapeDtypeStruct((rows.n_tiles * ROW_TILE, hw), F32)
    return pl.pallas_call(
        functools.partial(_hyena_pre_kernel, rows=rows),
        grid=(rows.n_tiles,),
        in_specs=in_specs,
        out_specs=[out_spec, out_spec],
        out_shape=[shape, shape],
        compiler_params=_cparams("parallel"),
        name="hyena_pre",
    )(p, p, p, p, p, p, p, p, p, conv_w, conv_b.reshape(1, 3 * hw))


def _filter_kernel(z_ref, w1_ref, b1_ref, f1_ref, w2_ref, b2_ref, f2_ref, w3_ref, win_ref, o_ref):
    hh = jnp.sin(f1_ref[...] * (_dot3(z_ref[...], w1_ref[...]) + b1_ref[...]))
    hh = jnp.sin(f2_ref[...] * (_dot3(hh, w2_ref[...]) + b2_ref[...]))
    win = win_ref[...]
    o_ref[...] = _dot3(hh, w3_ref[...]) * jnp.concatenate([win, win], axis=1)


@functools.lru_cache(maxsize=None)
def _filter_features(L):
    t = np.linspace(0.0, 1.0, L, dtype=np.float32)[:, None]
    pos = np.arange(L, dtype=np.float32)[:, None]
    bands = np.linspace(1e-4, HY_BANDS - 1, HY_BANDS, dtype=np.float32)[None]
    ang = (np.float32(2.0 * math.pi / L) * pos * bands).astype(np.float32)
    z = np.concatenate([t, np.cos(ang), np.sin(ang)], axis=-1).astype(np.float32)
    z = np.pad(z, ((0, 0), (0, LANE - HY_EMB)))
    max_decay = math.log(HY_TARGET) / HY_FAST_DECAY
    min_decay = math.log(HY_TARGET) / HY_SLOW_DECAY
    deltas = np.linspace(min_decay, max_decay, HY_WIDTH, dtype=np.float32)
    window = np.exp(-t * np.abs(deltas)[None]).astype(np.float32)
    return z, window


def _hyena_filters(L, w1, b1, f1, w2, b2, f2, w3):
    z, window = _filter_features(L)
    w1p = jnp.pad(w1, ((0, LANE - HY_EMB), (0, 0)))
    tl = min(L, 512)
    full = lambda a: pl.BlockSpec(a.shape, lambda i: (0,) * a.ndim)
    row = lambda a: a.reshape(1, -1)
    ops = [w1p, row(b1), row(f1), w2, row(b2), row(f2), w3]
    return pl.pallas_call(
        _filter_kernel,
        grid=(L // tl,),
        in_specs=[pl.BlockSpec((tl, LANE), lambda i: (i, 0))] + [full(a) for a in ops]
                 + [pl.BlockSpec((tl, HY_WIDTH), lambda i: (i, 0))],
        out_specs=pl.BlockSpec((tl, 2 * HY_WIDTH), lambda i: (i, 0)),
        out_shape=jax.ShapeDtypeStruct((L, 2 * HY_WIDTH), F32),
        compiler_params=_cparams("parallel"),
        name="hyena_filters",
    )(jnp.asarray(z), *ops, jnp.asarray(window))


def _freq_tile(L):
    return min(2 * L, 1024)


@functools.lru_cache(maxsize=None)
def _dft_tables(L):
    n = 2 * L
    tf = _freq_tile(L)
    half = tf // 2
    t = np.arange(L, dtype=np.int64)[None, :]
    fm = np.zeros((n, L), np.float64)
    scale = np.zeros((n, 1), np.float64)
    sign = np.zeros((n, 1), np.float64)
    for j in range(n // tf):
        k = (np.arange(half, dtype=np.int64) + j * half)[:, None]
        ang = 2.0 * np.pi * ((k * t) % n).astype(np.float64) / n
        fm[j * tf:j * tf + half] = np.cos(ang)
        fm[j * tf + half:(j + 1) * tf] = -np.sin(ang)
        scale[j * tf:(j + 1) * tf] = 2.0 / n
        sign[j * tf:j * tf + half] = 1.0
        sign[j * tf + half:(j + 1) * tf] = -1.0
    fm[half] = np.cos(np.pi * t[0])
    scale[0] = 1.0 / n
    scale[half] = 1.0 / n
    sign[half] = 1.0
    return (fm.astype(np.float32), np.ascontiguousarray(fm.T).astype(np.float32),
            scale.astype(np.float32), sign.astype(np.float32))


def _ktab_kernel(f_ref, h_ref, scale_ref, sign_ref, o_ref):
    hw = HY_WIDTH
    hh, hl = _split_bf16(h_ref[...])
    f = f_ref[...]
    kk = _dot(f, hh) + _dot(f, hl)
    o_ref[...] = scale_ref[...] * (kk[:, :hw] + sign_ref[...] * kk[:, hw:])


def _hyena_ktab(filt, L):
    fm, _, scale, sign = _dft_tables(L)
    n = 2 * L
    tf = _freq_tile(L)
    return pl.pallas_call(
        _ktab_kernel,
        grid=(n // tf,),
        in_specs=[pl.BlockSpec((tf, L), lambda j: (j, 0)),
                  pl.BlockSpec((L, 2 * HY_WIDTH), lambda j: (0, 0)),
                  pl.BlockSpec((tf, 1), lambda j: (j, 0)),
                  pl.BlockSpec((tf, 1), lambda j: (j, 0))],
        out_specs=pl.BlockSpec((tf, HY_WIDTH), lambda j: (j, 0)),
        out_shape=jax.ShapeDtypeStruct((n, HY_WIDTH), F32),
        compiler_params=_cparams("parallel"),
        name="hyena_ktab",
    )(jnp.asarray(fm, dtype=BF16), filt, jnp.asarray(scale), jnp.asarray(sign))


def _hyena_conv_kernel(z_ref, x0_ref, f_ref, ft_ref, k_ref, skip_ref, o_ref, zb_scr, acc_scr, *, tf):
    j = pl.program_id(1)
    half = tf // 2

    @pl.when(j == 0)
    def _():
        zb_scr[...] = z_ref[...].astype(BF16)
        acc_scr[...] = jnp.zeros_like(acc_scr)

    zf = _dot(f_ref[...], zb_scr[...])
    re, im = zf[:half], zf[half:]
    kre, kim = k_ref[:half, :], k_ref[half:, :]
    ridx = lax.broadcasted_iota(jnp.int32, re.shape, 0)
    mix = jnp.where(jnp.logical_and(j == 0, ridx == 0), 0.0, 1.0)
    yre = re * kre - mix * (im * kim)
    yim = mix * (re * kim) + im * jnp.where(mix == 0.0, kim, kre)
    y = jnp.concatenate([yre, yim], axis=0).astype(BF16)
    acc_scr[...] += _dot(ft_ref[...], y)

    @pl.when(j == pl.num_programs(1) - 1)
    def _():
        o_ref[...] = (x0_ref[...] * (acc_scr[...] + z_ref[...] * skip_ref[...])).astype(o_ref.dtype)


def _hyena_conv(z, x0c, ktab, skip, batch, L, blk0):
    fm, fmt, _, _ = _dft_tables(L)
    n = 2 * L
    tf = _freq_tile(L)
    hw = HY_WIDTH
    return pl.pallas_call(
        functools.partial(_hyena_conv_kernel, tf=tf),
        grid=(batch, n // tf),
        in_specs=[pl.BlockSpec((L, hw), lambda b, j: (blk0 + b, 0)),
                  pl.BlockSpec((L, hw), lambda b, j: (blk0 + b, 0)),
                  pl.BlockSpec((tf, L), lambda b, j: (j, 0)),
                  pl.BlockSpec((L, tf), lambda b, j: (0, j)),
                  pl.BlockSpec((tf, hw), lambda b, j: (j, 0)),
                  pl.BlockSpec((1, hw), lambda b, j: (0, 0))],
        out_specs=pl.BlockSpec((L, hw), lambda b, j: (b, 0)),
        out_shape=jax.ShapeDtypeStruct((batch * L, hw), BF16),
        scratch_shapes=[pltpu.VMEM((L, hw), BF16), pltpu.VMEM((L, hw), F32)],
        compiler_params=_cparams("parallel", "arbitrary"),
        name="hyena_conv",
    )(z, x0c, jnp.asarray(fm, dtype=BF16), jnp.asarray(fmt, dtype=BF16), ktab, skip.reshape(1, hw))


def _mix_out_kernel(ac_ref, al_ref, bc_ref, bl_ref, wa_ref, wb_ref, hc_ref, hl_ref, mod_ref, o_ref, *, nct):
    a = _stream_tile(ac_ref, al_ref, nct)
    b = _stream_tile(bc_ref, bl_ref, nct)
    y = _dot(a, wa_ref[...]) + _dot(b, wb_ref[...])
    o_ref[...] = _stream_tile(hc_ref, hl_ref, nct) + mod_ref[2] * y


def _mix_out(a_ctx, a_lat, b_ctx, b_lat, w_out, h, mods_l, rows):
    d = w_out.shape[1]
    ka, kb = a_ctx.shape[1], b_ctx.shape[1]
    return pl.pallas_call(
        functools.partial(_mix_out_kernel, nct=rows.nct),
        grid=(rows.n_tiles,),
        in_specs=[*_stream_specs((a_ctx, a_lat, 0), rows, ka), *_stream_specs((b_ctx, b_lat, 0), rows, kb),
                  pl.BlockSpec((ka, d), lambda i: (0, 0)),
                  pl.BlockSpec((kb, d), lambda i: (ka // kb, 0)),
                  *_stream_specs(h, rows, d),
                  pl.BlockSpec((None, 6, 1, d), lambda i: (rows.mod_row(i), 0, 0, 0))],
        out_specs=pl.BlockSpec((ROW_TILE, d), lambda i: (i, 0)),
        out_shape=jax.ShapeDtypeStruct((rows.n_tiles * ROW_TILE, d), F32),
        compiler_params=_cparams("parallel"),
        name="mix_out",
    )(a_ctx, a_lat, b_ctx, b_lat, w_out, w_out, h[0], h[1], mods_l)


def _shortconv_out_kernel(bg_ref, cg_ref, xi_ref, cgp_ref, cgn_ref, xip_ref, xin_ref, cw_ref, w_ref,
                          h_ref, mod_ref, o_ref, *, rows):
    first, last = rows.seq_edges(pl.program_id(0))
    keep_p = jnp.where(first, 0.0, 1.0)
    keep_n = jnp.where(last, 0.0, 1.0)
    f32 = lambda a: a.astype(F32)
    m = f32(cg_ref[...]) * f32(xi_ref[...])
    m_prev = f32(cgp_ref[HALO - 1:HALO, :]) * f32(xip_ref[HALO - 1:HALO, :]) * keep_p
    m_next = f32(cgn_ref[0:1, :]) * f32(xin_ref[0:1, :]) * keep_n
    y = f32(bg_ref[...]) * _conv3(m, m_prev, m_next, cw_ref)
    o_ref[...] = h_ref[...] + mod_ref[2] * _dot(y.astype(BF16), w_ref[...])


def _shortconv_out(p, conv_w, w_out, h, mods_l, rows):
    d = h.shape[1]
    t_rows = p.shape[0]
    in_specs = [pl.BlockSpec((ROW_TILE, d), lambda i, j=j: (i, j)) for j in range(3)]
    in_specs += [*_halo_specs(d, 1, t_rows), *_halo_specs(d, 2, t_rows),
                 pl.BlockSpec((3, d), lambda i: (0, 0)),
                 pl.BlockSpec((d, d), lambda i: (0, 0)),
                 pl.BlockSpec((ROW_TILE, d), lambda i: (i, 0)),
                 pl.BlockSpec((None, 6, 1, d), lambda i: (rows.mod_row(i), 0, 0, 0))]
    return pl.pallas_call(
        functools.partial(_shortconv_out_kernel, rows=rows),
        grid=(rows.n_tiles,),
        in_specs=in_specs,
        out_specs=pl.BlockSpec((ROW_TILE, d), lambda i: (i, 0)),
        out_shape=jax.ShapeDtypeStruct((rows.n_tiles * ROW_TILE, d), F32),
        compiler_params=_cparams("parallel"),
        name="shortconv_out",
    )(p, p, p, p, p, p, p, conv_w, w_out, h, mods_l)


META_E, META_RANK, META_GATE = 0, 2, 4


def _lane_min_index(mask, lane_f):
    return jnp.min(jnp.where(mask, lane_f, float(LANE)), axis=1, keepdims=True)


def _router_kernel(h_ref, g_ref, mod_ref, w_ref, b_ref, v_ref, meta_ref, cnt_ref, carry_scr):
    @pl.when(pl.program_id(0) == 0)
    def _():
        carry_scr[...] = jnp.zeros_like(carry_scr)

    v = _norm_mod(h_ref[...], g_ref[...], mod_ref[3], mod_ref[4])
    v_ref[...] = v
    lg = _dot3(v, w_ref[...]) + b_ref[...]
    tm = lg.shape[0]
    lane = lax.broadcasted_iota(jnp.int32, lg.shape, 1)
    lane_f = lane.astype(F32)
    neg = -jnp.inf

    is_grp = lane < N_GROUPS
    lgm = jnp.where(is_grp, lg, neg)
    m_g = jnp.max(lgm, axis=1, keepdims=True)
    s_g = jnp.sum(jnp.where(is_grp, jnp.exp(lg - m_g), 0.0), axis=1, keepdims=True)
    p_g = 1.0 / s_g
    grp = _lane_min_index(lgm == m_g, lane_f)

    ex_lane = lane - N_GROUPS
    in_grp = jnp.logical_and(jnp.logical_and(ex_lane >= 0, ex_lane < N_EXPERTS),
                             (ex_lane // EXP_PER_GROUP).astype(F32) == grp)
    m_e = jnp.max(jnp.where(in_grp, lg, neg), axis=1, keepdims=True)
    ee = jnp.where(in_grp, jnp.exp(lg - m_e), 0.0)
    pe = ee / jnp.sum(ee, axis=1, keepdims=True)
    pe1 = jnp.where(in_grp, pe, -1.0)
    p1 = jnp.max(pe1, axis=1, keepdims=True)
    i1 = _lane_min_index(pe1 == p1, lane_f)
    pe2 = jnp.where(lane_f == i1, -1.0, pe1)
    p2 = jnp.max(pe2, axis=1, keepdims=True)
    i2 = _lane_min_index(pe2 == p2, lane_f)
    denom = p1 + p2
    g1 = p_g * p1 / denom
    g2 = p_g * p2 / denom
    e1 = i1 - float(N_GROUPS)
    e2 = i2 - float(N_GROUPS)

    oh1 = lane_f == e1
    oh2 = lane_f == e2
    row = lax.broadcasted_iota(jnp.int32, (tm, tm), 0)
    col = lax.broadcasted_iota(jnp.int32, (tm, tm), 1)
    earlier = jnp.where(row > col, 1.0, 0.0).astype(BF16)
    c1 = _dot(earlier, jnp.where(oh1, 1.0, 0.0).astype(BF16))
    c2 = _dot(earlier, jnp.where(oh2, 1.0, 0.0).astype(BF16))
    tot1 = jnp.sum(jnp.where(oh1, 1.0, 0.0), axis=0, keepdims=True)
    tot2 = jnp.sum(jnp.where(oh2, 1.0, 0.0), axis=0, keepdims=True)
    carry = carry_scr[...]
    r1 = jnp.sum(jnp.where(oh1, carry + c1, 0.0), axis=1, keepdims=True)
    r2 = jnp.sum(jnp.where(oh2, carry + tot1 + c2, 0.0), axis=1, keepdims=True)
    carry = carry + tot1 + tot2
    carry_scr[...] = carry
    cnt_ref[...] = carry

    meta = jnp.zeros_like(lg)
    for idx, val in ((META_E, e1), (META_E + 1, e2), (META_RANK, r1), (META_RANK + 1, r2),
                     (META_GATE, g1), (META_GATE + 1, g2)):
        meta = jnp.where(lane == idx, val, meta)
    meta_ref[...] = meta


def _router(h, g, mods_l, w_rt, b_rt, rows):
    d = h.shape[1]
    tm = rows.tile
    nt = rows.n_tiles * tm
    return pl.pallas_call(
        _router_kernel,
        grid=(rows.n_tiles,),
        in_specs=[pl.BlockSpec((tm, d), lambda i: (i, 0)),
                  pl.BlockSpec((1, d), lambda i: (0, 0)),
                  pl.BlockSpec((None, 6, 1, d), lambda i: (rows.mod_row(i), 0, 0, 0)),
                  pl.BlockSpec((d, LANE), lambda i: (0, 0)),
                  pl.BlockSpec((1, LANE), lambda i: (0, 0))],
        out_specs=[pl.BlockSpec((tm, d), lambda i: (i, 0)),
                   pl.BlockSpec((tm, LANE), lambda i: (i, 0)),
                   pl.BlockSpec((1, LANE), lambda i: (0, 0))],
        out_shape=[jax.ShapeDtypeStruct((nt, d), F32), jax.ShapeDtypeStruct((nt, LANE), F32),
                   jax.ShapeDtypeStruct((1, LANE), F32)],
        scratch_shapes=[pltpu.VMEM((1, LANE), F32)],
        compiler_params=_cparams("arbitrary"),
        name="router",
    )(h, g.reshape(1, d), mods_l, w_rt, b_rt)


def _dispatch_plan(meta, counts):
    bm = MOE_ROWS
    t = meta.shape[0]
    n_blocks = -(-(t * TOP_K + N_EXPERTS * (bm - 1)) // bm)
    counts = counts[0, :N_EXPERTS].astype(jnp.int32)
    nblk = (counts + bm - 1) // bm
    blk_end = jnp.cumsum(nblk).astype(jnp.int32)
    slot0 = (blk_end - nblk) * bm
    expert = meta[:, META_E:META_E + TOP_K].astype(jnp.int32)
    rank = meta[:, META_RANK:META_RANK + TOP_K].astype(jnp.int32)
    onehot = expert[:, :, None] == jnp.arange(N_EXPERTS, dtype=jnp.int32)
    slot = rank + jnp.sum(jnp.where(onehot, slot0, 0), axis=-1)
    pad_lo = jnp.concatenate([slot0 + counts, blk_end[-1:] * bm])
    pad_hi = jnp.concatenate([blk_end * bm, jnp.full((1,), n_blocks * bm, jnp.int32)])
    src = _slot_tokens(jnp.concatenate([pad_lo, pad_hi]).astype(jnp.int32), _slot_tiles(slot), n_blocks)
    blk = jnp.arange(n_blocks, dtype=jnp.int32)
    blk_e = jnp.minimum(jnp.sum(blk[:, None] >= blk_end[None, :], axis=1), N_EXPERTS - 1).astype(jnp.int32)
    blk_used = (blk < blk_end[-1]).astype(jnp.int32)
    return slot, src.reshape(n_blocks, 1, bm), blk_e, blk_used


def _slot_tiles(slot):
    return slot.reshape(slot.shape[0] // ROW_TILE, ROW_TILE, TOP_K).transpose(0, 2, 1)


def _slot_tokens_kernel(pad_ref, slot_ref, src_ref):
    i = pl.program_id(0)
    tm = slot_ref.shape[1]
    n_ranges = pad_ref.shape[0] // 2

    @pl.when(i == 0)
    def _():
        def clear(j, c):
            src_ref[j] = 0
            return c

        for q in range(n_ranges):
            lax.fori_loop(pad_ref[q], pad_ref[n_ranges + q], clear, 0)

    base = i * tm
    for r in range(tm):
        for k in range(TOP_K):
            src_ref[slot_ref[k, r]] = base + r


def _slot_tokens(pad_ranges, slot_tiles, n_blocks):
    n = n_blocks * MOE_ROWS
    grid_spec = pltpu.PrefetchScalarGridSpec(
        num_scalar_prefetch=1,
        grid=(slot_tiles.shape[0],),
        in_specs=[pl.BlockSpec((None, TOP_K, ROW_TILE), lambda i, pad: (i, 0, 0), memory_space=pltpu.SMEM)],
        out_specs=pl.BlockSpec((n,), lambda i, pad: (0,), memory_space=pltpu.SMEM),
    )
    return pl.pallas_call(
        _slot_tokens_kernel,
        grid_spec=grid_spec,
        out_shape=jax.ShapeDtypeStruct((n,), jnp.int32),
        compiler_params=_cparams("arbitrary"),
        name="slot_tokens",
    )(pad_ranges, slot_tiles)


def _row_copies_start(src_hbm, idx_ref, k, dst, sem):
    for r in range(dst.shape[0]):
        pltpu.async_copy(src_hbm.at[pl.ds(idx_ref[k, r], 1), :], dst.at[pl.ds(r, 1), :], sem,
                         priority=r % DMA_THREADS)


def _row_copies_wait(src_hbm, dst, sem):
    pltpu.make_async_copy(src_hbm.at[pl.ds(0, dst.shape[0]), :], dst, sem).wait()


def _expert_kernel(blk_e_ref, blk_used_ref, src_ref, src_next_ref, v_hbm, w1_ref, w3_ref, w2_ref, y_ref,
                   xbuf, w1_scr, w3_scr, w2_scr, sem):
    j = pl.program_id(0)
    nb = pl.num_programs(0)
    used = blk_used_ref[j] > 0
    new_expert = jnp.logical_or(j == 0, blk_e_ref[j] != blk_e_ref[jnp.maximum(j - 1, 0)])

    @pl.when(j == 0)
    def _():
        _row_copies_start(v_hbm, src_ref, 0, xbuf.at[0], sem.at[0])

    @pl.when(jnp.logical_and(used, new_expert))
    def _():
        w1_scr[...] = w1_ref[...].astype(BF16)
        w3_scr[...] = w3_ref[...].astype(BF16)
        w2_scr[...] = w2_ref[...].astype(BF16)

    @pl.when(used)
    def _():
        b = j % 2
        _row_copies_wait(v_hbm, xbuf.at[b], sem.at[b])
        x = xbuf[b].astype(BF16)
        _row_copies_start(v_hbm, src_next_ref, 0, xbuf.at[1 - b], sem.at[1 - b])
        hid = _silu(_dot(x, w1_scr[...])) * _dot(x, w3_scr[...])
        y_ref[...] = _dot(hid.astype(BF16), w2_scr[...])

        @pl.when(jnp.logical_or(j == nb - 1, blk_used_ref[jnp.minimum(j + 1, nb - 1)] == 0))
        def _():
            _row_copies_wait(v_hbm, xbuf.at[1 - b], sem.at[1 - b])

    @pl.when(jnp.logical_not(used))
    def _():
        y_ref[...] = jnp.zeros_like(y_ref)


def _experts(v, src, blk_e, blk_used, w1, w3, w2):
    t, d = v.shape
    n_blocks = src.shape[0]
    bm = MOE_ROWS
    de = w1.shape[2]

    def next_block(j, e, used):
        nxt = jnp.minimum(j + 1, n_blocks - 1)
        return jnp.where(used[nxt] > 0, nxt, j)

    grid_spec = pltpu.PrefetchScalarGridSpec(
        num_scalar_prefetch=2,
        grid=(n_blocks,),
        in_specs=[pl.BlockSpec((None, 1, bm), lambda j, e, u: (j, 0, 0), memory_space=pltpu.SMEM),
                  pl.BlockSpec((None, 1, bm), lambda j, e, u: (next_block(j, e, u), 0, 0), memory_space=pltpu.SMEM),
                  pl.BlockSpec(memory_space=pl.ANY),
                  pl.BlockSpec((None, d, de), lambda j, e, u: (e[j], 0, 0)),
                  pl.BlockSpec((None, d, de), lambda j, e, u: (e[j], 0, 0)),
                  pl.BlockSpec((None, de, d), lambda j, e, u: (e[j], 0, 0))],
        out_specs=pl.BlockSpec((bm, d), lambda j, e, u: (j, 0)),
        scratch_shapes=[pltpu.VMEM((2, bm, d), F32), pltpu.VMEM((d, de), BF16), pltpu.VMEM((d, de), BF16),
                        pltpu.VMEM((de, d), BF16), pltpu.SemaphoreType.DMA((2,))],
    )
    return pl.pallas_call(
        _expert_kernel,
        grid_spec=grid_spec,
        out_shape=jax.ShapeDtypeStruct((n_blocks * bm, d), F32),
        compiler_params=_cparams("arbitrary"),
        name="experts",
    )(blk_e, blk_used, src, src, v, w1, w3, w2)


def _combined_tile(slot_ref, slot_next_ref, hc_ref, hl_ref, nct, meta_ref, modp_ref, y_hbm, buf, sem, overlap_with):
    i = pl.program_id(0)
    n = pl.num_programs(0)

    @pl.when(i == 0)
    def _():
        for k in range(TOP_K):
            _row_copies_start(y_hbm, slot_ref, k, buf.at[0, k], sem.at[0])

    b = i % 2
    for k in range(TOP_K):
        _row_copies_wait(y_hbm, buf.at[b, k], sem.at[b])
    meta = meta_ref[...]
    f = meta[:, META_GATE:META_GATE + 1] * buf[b, 0] + meta[:, META_GATE + 1:META_GATE + 2] * buf[b, 1]
    hn = _stream_tile(hc_ref, hl_ref, nct) + modp_ref[5] * f
    for k in range(TOP_K):
        _row_copies_start(y_hbm, slot_next_ref, k, buf.at[1 - b, k], sem.at[1 - b])
    overlap_with(hn)

    @pl.when(i == n - 1)
    def _():
        for k in range(TOP_K):
            _row_copies_wait(y_hbm, buf.at[1 - b, k], sem.at[1 - b])


def _combine_mm_kernel(slot_ref, slot_next_ref, hc_ref, hl_ref, meta_ref, modp_ref, g_ref, mod_ref, w_ref, y_hbm,
                       hn_ref, o_ref, buf, sem, *, nct):
    def project(hn):
        hn_ref[...] = hn
        u = _norm_mod(hn, g_ref[...], mod_ref[0], mod_ref[1])
        o_ref[...] = _dot(u.astype(BF16), w_ref[...]).astype(o_ref.dtype)

    _combined_tile(slot_ref, slot_next_ref, hc_ref, hl_ref, nct, meta_ref, modp_ref, y_hbm, buf, sem, project)


def _combine_norm_kernel(slot_ref, slot_next_ref, hc_ref, hl_ref, meta_ref, modp_ref, g_ref, y_hbm, o_ref, buf, sem,
                         *, nct):
    def finish(hn):
        o_ref[...] = hn * lax.rsqrt(jnp.mean(hn * hn, axis=-1, keepdims=True) + EPS) * g_ref[...]

    _combined_tile(slot_ref, slot_next_ref, hc_ref, hl_ref, nct, meta_ref, modp_ref, y_hbm, buf, sem, finish)


def _combine_then(h, ys, slot_tiles, meta, mods_prev, rows, g, mods_l=None, w=None, name="combine_norm", tile0=0):
    d = g.shape[0]
    n_tiles = rows.n_tiles
    slot_spec = lambda fn: pl.BlockSpec((None, TOP_K, ROW_TILE), lambda i: (tile0 + fn(i), 0, 0),
                                        memory_space=pltpu.SMEM)
    mod_spec = pl.BlockSpec((None, 6, 1, d), lambda i: (rows.mod_row(i), 0, 0, 0))
    tile_in = lambda width: pl.BlockSpec((ROW_TILE, width), lambda i: (tile0 + i, 0))
    tile = lambda width: pl.BlockSpec((ROW_TILE, width), lambda i: (i, 0))
    rows_shape = lambda width: jax.ShapeDtypeStruct((n_tiles * ROW_TILE, width), F32)
    in_specs = [slot_spec(lambda i: i), slot_spec(lambda i: jnp.minimum(i + 1, n_tiles - 1)),
                *_stream_specs(h, rows, d), tile_in(LANE), mod_spec, pl.BlockSpec((1, d), lambda i: (0, 0))]
    args = [slot_tiles, slot_tiles, h[0], h[1], meta, mods_prev, g.reshape(1, d)]
    if w is None:
        kern, out_specs, out_shape = _combine_norm_kernel, tile(d), rows_shape(d)
    else:
        nw = w.shape[1]
        in_specs += [mod_spec, pl.BlockSpec((d, nw), lambda i: (0, 0))]
        args += [mods_l, w]
        kern, out_specs = _combine_mm_kernel, [tile(d), tile(nw)]
        out_shape = [rows_shape(d), jax.ShapeDtypeStruct((n_tiles * ROW_TILE, nw), BF16)]
    return pl.pallas_call(
        functools.partial(kern, nct=rows.nct),
        grid=(n_tiles,),
        in_specs=in_specs + [pl.BlockSpec(memory_space=pl.ANY)],
        out_specs=out_specs,
        out_shape=out_shape,
        scratch_shapes=[pltpu.VMEM((2, TOP_K, ROW_TILE, d), F32), pltpu.SemaphoreType.DMA((2,))],
        compiler_params=_cparams("arbitrary"),
        name=name,
    )(*args, ys)


def _even_w_in(w):
    d = w.shape[0]
    n_main = 2 * GLA_QK + 2 * GLA_V
    ranks = w[:, n_main:n_main + 2 * GLA_RANK]
    hy = w[:, n_main + 2 * GLA_RANK:]
    pad = jnp.zeros((d, R_COLS - 2 * GLA_RANK), w.dtype)
    return jnp.concatenate([w[:, :n_main], hy, ranks, pad], axis=1).astype(BF16)


def _rank_proj(wa, first_row):
    return jnp.zeros((R_COLS, GLA_QK), F32).at[first_row:first_row + GLA_RANK].set(wa)


def _grid_transpose(h_lat, batch, a, b):
    d = h_lat.shape[1]
    return h_lat.reshape(batch, a, b, d).transpose(0, 2, 1, 3).reshape(-1, d)


def kernel(x, c, ctx, c_ctx, mod_w, mod_b, norm_mix, norm_ffn, norm_final, ev_w_in, ev_w_out, gla_wa_f, gla_ba_f, gla_wa_b, gla_ba_b, gla_norm, hy_conv_w, hy_conv_b, hy_w1, hy_b1, hy_f1, hy_w2, hy_b2, hy_f2, hy_w3, hy_skip, od_w_in, od_conv_w, od_w_out, rt_w_grp, rt_b_grp, rt_w_exp, rt_b_exp, ex_w1, ex_w3, ex_w2):
    batch, s, d = x.shape
    lc = ctx.shape[1]
    depth = mod_w.shape[0]
    tc, tl = batch * lc, batch * s
    assert lc % ROW_TILE == 0 and s % ROW_TILE == 0 and tc % s == 0 and s % GRID_W == 0
    assert lc % GLA_CHUNK == 0 and s % GLA_CHUNK == 0
    assert s % ROUTER_TILE == 0 and tc % ROUTER_TILE == 0
    assert depth % 2 == 0
    grid_rows = s // GRID_W

    n_cond = -(-(batch + 1) // 8) * 8
    cond = jnp.concatenate([c, c_ctx[None], jnp.zeros((n_cond - batch - 1, d), F32)], axis=0)
    mods = _mods(cond, mod_w, mod_b).reshape(depth, n_cond, 6, 1, d)

    hs = (ctx.reshape(tc, d), x.reshape(tl, d), 0)
    col_major_now = False
    moe = None

    def lat_part(a, fn):
        return fn(a) if a.shape[0] == tl else jnp.concatenate([a[:tc], fn(a[tc:])], axis=0)

    for l in range(depth):
        i = l // 2
        even = l % 2 == 0
        ctx_out = l < depth - 1
        col_major = i % 2 == 1
        routing = [] if moe is None else [moe[1], moe[2]]
        if col_major != col_major_now:
            perm = ((lambda a: _grid_transpose(a, batch, grid_rows, GRID_W)) if col_major
                    else (lambda a: _grid_transpose(a, batch, GRID_W, grid_rows)))
            lat = hs[1][hs[2] * ROW_TILE:hs[2] * ROW_TILE + tl]
            hs = (hs[0], perm(lat), 0)
            routing = [lat_part(a, perm) for a in routing]
            col_major_now = col_major
        arr_rows = _Rows(batch, lc, s, not ctx_out)
        if not ctx_out:
            hs = (hs[1], hs[1], hs[2])
        mods_l = mods[l]
        w_in = _even_w_in(ev_w_in[i]) if even else od_w_in[i].astype(BF16)
        name = "even_in" if even else "odd_in"
        if moe is None:
            p = _normmod_mm(hs, norm_mix[l], mods_l, w_in, arr_rows, 0, 1, name)
        else:
            tile0 = routing[0].shape[0] // ROW_TILE - arr_rows.n_tiles
            h, p = _combine_then(hs, moe[0], _slot_tiles(routing[0]), routing[1], moe[3], arr_rows,
                                 norm_mix[l], mods_l, w_in, name, tile0)
            hs = _as_stream(h, arr_rows)

        if even:
            waf = _rank_proj(gla_wa_f[i], 0)
            wab = _rank_proj(gla_wa_b[i], GLA_RANK)
            baf, bab = gla_ba_f[i].reshape(1, -1), gla_ba_b[i].reshape(1, -1)
            gain = gla_norm[i].reshape(1, -1)
            zeros = jnp.zeros((batch, 2, 2 * GLA_DV, 2 * GLA_DK), F32)
            filt_args = (hy_w1[i], hy_b1[i], hy_f1[i], hy_w2[i], hy_b2[i], hy_f2[i], hy_w3[i])
            x0c, z = _hyena_pre(p, hy_conv_w[i], hy_conv_b[i], arr_rows)
            gla_c, sc_f, sc_b = _gla(p, waf, wab, baf, bab, gain, zeros, zeros, batch, lc, 0)
            gla_l, _, _ = _gla(p, waf, wab, baf, bab, gain, sc_f, sc_b, batch, s, tc // s)
            kt_c = _hyena_ktab(_hyena_filters(lc, *filt_args), lc)
            kt_l = _hyena_ktab(_hyena_filters(s, *filt_args), s)
            hy_c = _hyena_conv(z, x0c, kt_c, hy_skip[i], batch, lc, 0)
            hy_l = _hyena_conv(z, x0c, kt_l, hy_skip[i], batch, s, tc // s)
            h = _mix_out(gla_c, gla_l, hy_c, hy_l, ev_w_out[i].astype(BF16), hs, mods_l, arr_rows)
        else:
            h = _shortconv_out(p, od_conv_w[i], od_w_out[i].astype(BF16), h, mods_l, arr_rows)

        w_rt = jnp.concatenate([rt_w_grp[l], rt_w_exp[l],
                                jnp.zeros((d, LANE - N_GROUPS - N_EXPERTS), F32)], axis=1)
        b_rt = jnp.concatenate([rt_b_grp[l], rt_b_exp[l],
                                jnp.zeros((LANE - N_GROUPS - N_EXPERTS,), F32)]).reshape(1, LANE)
        v, meta, counts = _router(h, norm_ffn[l], mods_l, w_rt, b_rt,
                                  _Rows(batch, lc, s, not ctx_out, ROUTER_TILE))
        slot, src, blk_e, blk_used = _dispatch_plan(meta, counts)
        ys = _experts(v, src, blk_e, blk_used, ex_w1[l], ex_w3[l], ex_w2[l])
        moe = (ys, slot, meta, mods_l)
        hs = _as_stream(h, arr_rows)

    out = _combine_then(hs, moe[0], _slot_tiles(moe[1]), moe[2], moe[3], arr_rows, norm_final)
    if col_major_now:
        out = _grid_transpose(out, batch, GRID_W, grid_rows)
    return out.reshape(batch, s, d)
```

```python
import functools
import math

import numpy as np
import jax
import jax.numpy as jnp
from jax import lax
from jax.experimental import pallas as pl
from jax.experimental.pallas import tpu as pltpu

F32 = jnp.float32
BF16 = jnp.bfloat16

EPS = 1e-6
GRID_W = 64

GLA_HEADS = 4
GLA_DK = 64
GLA_DV = 128
GLA_RANK = 16
GLA_TAU = 16.0
GLA_CHUNK = 64
GLA_SLAB = 256
GLA_STEP_UNROLL = 4
GLA_QK = GLA_HEADS * GLA_DK
GLA_V = GLA_HEADS * GLA_DV

HY_WIDTH = 512
HY_EMB = 33
HY_BANDS = (HY_EMB - 1) // 2
HY_HIDDEN = 64
HY_FAST_DECAY = 0.3
HY_SLOW_DECAY = 1.5
HY_TARGET = 1e-2

N_GROUPS = 4
EXP_PER_GROUP = 8
N_EXPERTS = N_GROUPS * EXP_PER_GROUP
TOP_K = 2

LANE = 128
ROW_TILE = 256
MOE_ROWS = 256
ROUTER_TILE = 512
R_COLS = LANE
VMEM_LIMIT = 56 * 1024 * 1024
DMA_THREADS = 2


def _cparams(*sem):
    return pltpu.CompilerParams(dimension_semantics=sem, vmem_limit_bytes=VMEM_LIMIT)


def _split_bf16(a):
    hi = a.astype(BF16)
    lo = (a - hi.astype(F32)).astype(BF16)
    return hi, lo


def _dot(a, b):
    return jnp.dot(a, b, preferred_element_type=F32)


def _dot_nt(a, b):
    return lax.dot_general(a, b, (((1,), (1,)), ((), ())), preferred_element_type=F32)


def _dot_tn(a, b):
    return lax.dot_general(a, b, (((0,), (0,)), ((), ())), preferred_element_type=F32)


def _dot3(a, b):
    ah, al = _split_bf16(a)
    bh, bl = _split_bf16(b)
    return _dot(ah, bh) + _dot(ah, bl) + _dot(al, bh)


def _silu(x):
    return x / (1.0 + jnp.exp(-x))


def _log_sigmoid(x):
    return jnp.minimum(x, 0.0) - jnp.log1p(jnp.exp(-jnp.abs(x)))


def _norm_mod(x, g, shift, scale):
    y = x * lax.rsqrt(jnp.mean(x * x, axis=-1, keepdims=True) + EPS)
    return (y * g) * (1.0 + scale) + shift


def _mods_kernel(s_ref, w_ref, b_ref, o_ref):
    s = s_ref[...]
    s = _silu(s)
    o_ref[...] = _dot3(s, w_ref[...]) + b_ref[...]


def _mods(cond, mod_w, mod_b):
    depth, d, n = mod_w.shape
    r = cond.shape[0]
    tn = 1024
    return pl.pallas_call(
        _mods_kernel,
        grid=(depth, n // tn),
        in_specs=[pl.BlockSpec((r, d), lambda l, j: (0, 0)),
                  pl.BlockSpec((None, d, tn), lambda l, j: (l, 0, j)),
                  pl.BlockSpec((None, 1, tn), lambda l, j: (l, 0, j))],
        out_specs=pl.BlockSpec((None, r, tn), lambda l, j: (l, 0, j)),
        out_shape=jax.ShapeDtypeStruct((depth, r, n), F32),
        compiler_params=_cparams("parallel", "parallel"),
        name="mods",
    )(cond, mod_w, mod_b.reshape(depth, 1, n))


class _Rows:
    def __init__(self, batch, lc, s, lat_only, tile=ROW_TILE):
        self.batch = batch
        self.tile = tile
        self.nct = 0 if lat_only else batch * lc // tile
        self.tps = s // tile
        self.tpc = lc // tile
        self.n_tiles = self.nct + batch * self.tps

    def mod_row(self, i):
        return jnp.where(i < self.nct, self.batch, (i - self.nct) // self.tps)

    def seq_edges(self, i):
        pos_c = i % self.tpc
        pos_l = (i - self.nct) % self.tps
        is_c = i < self.nct
        first = jnp.where(is_c, pos_c == 0, pos_l == 0)
        last = jnp.where(is_c, pos_c == self.tpc - 1, pos_l == self.tps - 1)
        return first, last


def _stream_specs(stream, rows, width):
    _, _, lat0 = stream
    nct = rows.nct
    ctx_spec = pl.BlockSpec((ROW_TILE, width), lambda i: (jnp.clip(i, 0, max(nct - 1, 0)), 0))
    lat_spec = pl.BlockSpec((ROW_TILE, width), lambda i: (jnp.maximum(i - nct, 0) + lat0, 0))
    return [ctx_spec, lat_spec]


def _grid_view(a, s):
    grid_rows = s // GRID_W
    cpt = ROW_TILE // grid_rows
    return a.reshape(a.shape[0] // s, grid_rows, GRID_W // cpt, cpt, a.shape[1])


def _grid_spec(s, d, first_seq):
    grid_rows = s // GRID_W
    cpt = ROW_TILE // grid_rows
    groups = GRID_W // cpt
    return lambda lat_tile: pl.BlockSpec((None, grid_rows, None, cpt, d),
                                         lambda i: (first_seq + lat_tile(i) // groups, 0, lat_tile(i) % groups, 0, 0))


def _lat_tile(l_ref):
    if len(l_ref.shape) == 2:
        return l_ref[...]
    return jnp.concatenate([l_ref[:, w, :] for w in range(l_ref.shape[1])], axis=0)


def _stream_tile(c_ref, l_ref, nct):
    if nct == 0:
        return _lat_tile(l_ref)
    return jnp.where(pl.program_id(0) < nct, c_ref[...], _lat_tile(l_ref))


def _as_stream(a, rows):
    return (a, a, rows.nct)


def _normmod_mm_kernel(hc_ref, hl_ref, g_ref, mod_ref, w_ref, o_ref, *, shift_i, scale_i, nct):
    u = _norm_mod(_stream_tile(hc_ref, hl_ref, nct), g_ref[...], mod_ref[shift_i], mod_ref[scale_i])
    o_ref[...] = _dot(u.astype(BF16), w_ref[...]).astype(o_ref.dtype)


def _normmod_mm(h, g, mods_l, w, rows, shift_i, scale_i, name):
    d = w.shape[0]
    n = w.shape[1]
    return pl.pallas_call(
        functools.partial(_normmod_mm_kernel, shift_i=shift_i, scale_i=scale_i, nct=rows.nct),
        grid=(rows.n_tiles,),
        in_specs=[*_stream_specs(h, rows, d),
                  pl.BlockSpec((1, d), lambda i: (0, 0)),
                  pl.BlockSpec((None, 6, 1, d), lambda i: (rows.mod_row(i), 0, 0, 0)),
                  pl.BlockSpec((d, n), lambda i: (0, 0))],
        out_specs=pl.BlockSpec((ROW_TILE, n), lambda i: (i, 0)),
        out_shape=jax.ShapeDtypeStruct((rows.n_tiles * ROW_TILE, n), BF16),
        compiler_params=_cparams("parallel"),
        name=name,
    )(h[0], h[1], g.reshape(1, d), mods_l, w)


def _gla_kernel(q_ref, k_ref, v_ref, g_ref, r_ref, waf_ref, wab_ref, baf_ref, bab_ref, gain_ref,
                s0f_ref, s0b_ref, o_ref, sf_ref, sb_ref, gl_scr, tot_scr, qd_scr, ds_scr, o_scr, st_scr,
                *, seq_len):
    C, SL = GLA_CHUNK, GLA_SLAB
    cps = SL // C
    n_chunks, n_slabs = seq_len // C, seq_len // SL
    head_of_lane = lax.broadcasted_iota(jnp.int32, (SL, 2 * GLA_DK), 1) // GLA_DK
    row = lax.broadcasted_iota(jnp.int32, (SL, SL), 0)
    col = lax.broadcasted_iota(jnp.int32, (SL, SL), 1)
    same_chunk = (row // C) == (col // C)
    srow = lax.broadcasted_iota(jnp.int32, (2 * GLA_DV, 2 * GLA_DK), 0) // GLA_DV
    scol = lax.broadcasted_iota(jnp.int32, (2 * GLA_DV, 2 * GLA_DK), 1) // GLA_DK
    same_head = srow == scol
    r = r_ref[...].astype(F32)

    def direction(wa_ref, ba_ref, s0_ref, s_out_ref, forward):
        mask = jnp.logical_and(same_chunk, (row >= col) if forward else (row <= col))
        tri = jnp.where(mask, 1.0, 0.0).astype(BF16)
        gl_scr[...] = _log_sigmoid(_dot3(r, wa_ref[...]) + ba_ref[...]) * (1.0 / GLA_TAU)

        def slab(s, carry):
            rows = pl.ds(pl.multiple_of(s * SL, SL), SL)
            gl = gl_scr[rows, :]
            g_hi, g_lo = _split_bf16(gl)
            b = _dot(tri, g_hi) + _dot(tri, g_lo)
            b3 = b.reshape(cps, C, 2 * GLA_DK)
            last = b3[:, C - 1:C, :] if forward else b3[:, 0:1, :]
            tot = jnp.broadcast_to(last, b3.shape).reshape(SL, 2 * GLA_DK)
            tot_scr[rows, :] = tot
            q = q_ref[rows, :].astype(F32) * (GLA_DK ** -0.5)
            k = k_ref[rows, :].astype(F32)
            v = v_ref[rows, :].astype(BF16)
            qd = q * jnp.exp(b)
            kd = (k * jnp.exp(-b)).astype(BF16)
            kr = (k * jnp.exp(tot - b)).astype(BF16)
            qd_scr[rows, :] = qd.astype(BF16)
            o_parts = []
            for h in range(2):
                qh = jnp.where(head_of_lane == h, qd, 0.0).astype(BF16)
                a = jnp.where(mask, _dot_nt(qh, kd), 0.0).astype(BF16)
                o_parts.append(_dot(a, v[:, h * GLA_DV:(h + 1) * GLA_DV]))
            o = jnp.concatenate(o_parts, axis=1)
            if forward:
                o_scr[rows, :] = o
            else:
                o_scr[rows, :] += o
            for c in range(cps):
                ds = _dot_tn(v[c * C:(c + 1) * C], kr[c * C:(c + 1) * C])
                ds_scr[s * cps + c] = jnp.where(same_head, ds, 0.0)
            return carry

        lax.fori_loop(0, n_slabs, slab, 0, unroll=2)
        st_scr[...] = s0_ref[...]

        def step(i, carry):
            c = i if forward else n_chunks - 1 - i
            first = pl.multiple_of(c * C, C)
            rows = pl.ds(first, C)
            st = st_scr[...]
            o_scr[rows, :] += _dot_nt(qd_scr[rows, :], st.astype(BF16))
            st_scr[...] = st * jnp.exp(tot_scr[pl.ds(first, 1), :]) + ds_scr[c]
            return carry

        lax.fori_loop(0, n_chunks, step, 0, unroll=GLA_STEP_UNROLL)
        s_out_ref[...] = st_scr[...]

    direction(waf_ref, baf_ref, s0f_ref, sf_ref, True)
    direction(wab_ref, bab_ref, s0b_ref, sb_ref, False)
    gain = gain_ref[...]

    def readout(s, carry):
        rows = pl.ds(pl.multiple_of(s * SL, SL), SL)
        o = o_scr[rows, :]
        outs = []
        for h in range(2):
            oh = o[:, h * GLA_DV:(h + 1) * GLA_DV]
            outs.append(oh * lax.rsqrt(jnp.mean(oh * oh, axis=-1, keepdims=True) + EPS) * gain)
        o_ref[rows, :] = (jnp.concatenate(outs, axis=1) * _silu(g_ref[rows, :].astype(F32))).astype(o_ref.dtype)
        return carry

    lax.fori_loop(0, n_slabs, readout, 0)


def _gla(p, waf, wab, baf, bab, gain, s0f, s0b, batch, seq_len, blk0):
    dk2, dv2 = 2 * GLA_DK, 2 * GLA_DV
    seq = lambda width, cb: pl.BlockSpec((seq_len, width), lambda b, hp: (blk0 + b, cb(hp)))
    state_spec = pl.BlockSpec((None, None, dv2, dk2), lambda b, hp: (b, hp, 0, 0))
    in_specs = [seq(dk2, lambda hp: hp),
                seq(dk2, lambda hp: GLA_QK // dk2 + hp),
                seq(dv2, lambda hp: 2 * GLA_QK // dv2 + hp),
                seq(dv2, lambda hp: (2 * GLA_QK + GLA_V) // dv2 + hp),
                seq(R_COLS, lambda hp: (2 * GLA_QK + 2 * GLA_V + 3 * HY_WIDTH) // R_COLS),
                pl.BlockSpec((R_COLS, dk2), lambda b, hp: (0, hp)),
                pl.BlockSpec((R_COLS, dk2), lambda b, hp: (0, hp)),
                pl.BlockSpec((1, dk2), lambda b, hp: (0, hp)),
                pl.BlockSpec((1, dk2), lambda b, hp: (0, hp)),
                pl.BlockSpec((1, GLA_DV), lambda b, hp: (0, 0)),
                state_spec, state_spec]
    state_shape = jax.ShapeDtypeStruct((batch, 2, dv2, dk2), F32)
    return pl.pallas_call(
        functools.partial(_gla_kernel, seq_len=seq_len),
        grid=(batch, 2),
        in_specs=in_specs,
        out_specs=[pl.BlockSpec((seq_len, dv2), lambda b, hp: (b, hp)), state_spec, state_spec],
        out_shape=[jax.ShapeDtypeStruct((batch * seq_len, GLA_V), BF16), state_shape, state_shape],
        scratch_shapes=[pltpu.VMEM((seq_len, dk2), F32),
                        pltpu.VMEM((seq_len, dk2), F32),
                        pltpu.VMEM((seq_len, dk2), BF16),
                        pltpu.VMEM((seq_len // GLA_CHUNK, dv2, dk2), F32),
                        pltpu.VMEM((seq_len, dv2), F32),
                        pltpu.VMEM((dv2, dk2), F32)],
        compiler_params=_cparams("parallel", "parallel"),
        name="gla",
    )(p, p, p, p, p, waf, wab, baf, bab, gain, s0f, s0b)


def _conv3(m, prev_row, next_row, w_ref):
    n = m.shape[0]
    ridx = lax.broadcasted_iota(jnp.int32, m.shape, 0)
    m_prev = jnp.where(ridx == 0, prev_row, pltpu.roll(m, 1, 0))
    m_next = jnp.where(ridx == n - 1, next_row, pltpu.roll(m, n - 1, 0))
    return w_ref[0:1, :] * m_prev + w_ref[1:2, :] * m + w_ref[2:3, :] * m_next


HALO = 16


def _halo_specs(width, col_block, t_rows):
    g = ROW_TILE // HALO
    last = t_rows // HALO - 1
    prev = pl.BlockSpec((HALO, width), lambda i: (jnp.maximum(i * g - 1, 0), col_block))
    nxt = pl.BlockSpec((HALO, width), lambda i: (jnp.minimum((i + 1) * g, last), col_block))
    return prev, nxt


def _hyena_pre_kernel(x0_ref, x1_ref, v_ref, x0p_ref, x0n_ref, x1p_ref, x1n_ref, vp_ref, vn_ref,
                      w_ref, b_ref, x0c_ref, z_ref, *, rows):
    first, last = rows.seq_edges(pl.program_id(0))
    keep_p = jnp.where(first, 0.0, 1.0)
    keep_n = jnp.where(last, 0.0, 1.0)
    hw = HY_WIDTH

    def conv(ref, p_ref, n_ref, j):
        w = w_ref.at[:, j * hw:(j + 1) * hw]
        y = _conv3(ref[...].astype(F32), p_ref[HALO - 1:HALO, :].astype(F32) * keep_p,
                   n_ref[0:1, :].astype(F32) * keep_n, w)
        return y + b_ref[:, j * hw:(j + 1) * hw]

    x0c_ref[...] = conv(x0_ref, x0p_ref, x0n_ref, 0)
    z_ref[...] = conv(v_ref, vp_ref, vn_ref, 2) * conv(x1_ref, x1p_ref, x1n_ref, 1)


def _hyena_pre(p, conv_w, conv_b, rows):
    t_rows = p.shape[0]
    hw = HY_WIDTH
    cb0 = (2 * GLA_QK + 2 * GLA_V) // hw
    in_specs = [pl.BlockSpec((ROW_TILE, hw), lambda i, j=j: (i, cb0 + j)) for j in range(3)]
    for j in range(3):
        in_specs.extend(_halo_specs(hw, cb0 + j, t_rows))
    in_specs += [pl.BlockSpec((3, 3 * hw), lambda i: (0, 0)), pl.BlockSpec((1, 3 * hw), lambda i: (0, 0))]
    out_spec = pl.BlockSpec((ROW_TILE, hw), lambda i: (i, 0))
    shape = jax.ShapeDtypeStruct((rows.n_tiles * ROW_TILE, hw), F32)
    return pl.pallas_call(
        functools.partial(_hyena_pre_kernel, rows=rows),
        grid=(rows.n_tiles,),
        in_specs=in_specs,
        out_specs=[out_spec, out_spec],
        out_shape=[shape, shape],
        compiler_params=_cparams("parallel"),
        name="hyena_pre",
    )(p, p, p, p, p, p, p, p, p, conv_w, conv_b.reshape(1, 3 * hw))


def _filter_kernel(z_ref, w1_ref, b1_ref, f1_ref, w2_ref, b2_ref, f2_ref, w3_ref, win_ref, o_ref):
    hh = jnp.sin(f1_ref[...] * (_dot3(z_ref[...], w1_ref[...]) + b1_ref[...]))
    hh = jnp.sin(f2_ref[...] * (_dot3(hh, w2_ref[...]) + b2_ref[...]))
    win = win_ref[...]
    o_ref[...] = _dot3(hh, w3_ref[...]) * jnp.concatenate([win, win], axis=1)


@functools.lru_cache(maxsize=None)
def _filter_features(L):
    t = np.linspace(0.0, 1.0, L, dtype=np.float32)[:, None]
    pos = np.arange(L, dtype=np.float32)[:, None]
    bands = np.linspace(1e-4, HY_BANDS - 1, HY_BANDS, dtype=np.float32)[None]
    ang = (np.float32(2.0 * math.pi / L) * pos * bands).astype(np.float32)
    z = np.concatenate([t, np.cos(ang), np.sin(ang)], axis=-1).astype(np.float32)
    z = np.pad(z, ((0, 0), (0, LANE - HY_EMB)))
    max_decay = math.log(HY_TARGET) / HY_FAST_DECAY
    min_decay = math.log(HY_TARGET) / HY_SLOW_DECAY
    deltas = np.linspace(min_decay, max_decay, HY_WIDTH, dtype=np.float32)
    window = np.exp(-t * np.abs(deltas)[None]).astype(np.float32)
    return z, window


def _hyena_filters(L, w1, b1, f1, w2, b2, f2, w3):
    z, window = _filter_features(L)
    w1p = jnp.pad(w1, ((0, LANE - HY_EMB), (0, 0)))
    tl = min(L, 512)
    full = lambda a: pl.BlockSpec(a.shape, lambda i: (0,) * a.ndim)
    row = lambda a: a.reshape(1, -1)
    ops = [w1p, row(b1), row(f1), w2, row(b2), row(f2), w3]
    return pl.pallas_call(
        _filter_kernel,
        grid=(L // tl,),
        in_specs=[pl.BlockSpec((tl, LANE), lambda i: (i, 0))] + [full(a) for a in ops]
                 + [pl.BlockSpec((tl, HY_WIDTH), lambda i: (i, 0))],
        out_specs=pl.BlockSpec((tl, 2 * HY_WIDTH), lambda i: (i, 0)),
        out_shape=jax.ShapeDtypeStruct((L, 2 * HY_WIDTH), F32),
        compiler_params=_cparams("parallel"),
        name="hyena_filters",
    )(jnp.asarray(z), *ops, jnp.asarray(window))


def _freq_tile(L):
    return min(2 * L, 1024)


@functools.lru_cache(maxsize=None)
def _dft_tables(L):
    n = 2 * L
    tf = _freq_tile(L)
    half = tf // 2
    t = np.arange(L, dtype=np.int64)[None, :]
    fm = np.zeros((n, L), np.float64)
    scale = np.zeros((n, 1), np.float64)
    sign = np.zeros((n, 1), np.float64)
    for j in range(n // tf):
        k = (np.arange(half, dtype=np.int64) + j * half)[:, None]
        ang = 2.0 * np.pi * ((k * t) % n).astype(np.float64) / n
        fm[j * tf:j * tf + half] = np.cos(ang)
        fm[j * tf + half:(j + 1) * tf] = -np.sin(ang)
        scale[j * tf:(j + 1) * tf] = 2.0 / n
        sign[j * tf:j * tf + half] = 1.0
        sign[j * tf + half:(j + 1) * tf] = -1.0
    fm[half] = np.cos(np.pi * t[0])
    scale[0] = 1.0 / n
    scale[half] = 1.0 / n
    sign[half] = 1.0
    return (fm.astype(np.float32), np.ascontiguousarray(fm.T).astype(np.float32),
            scale.astype(np.float32), sign.astype(np.float32))


def _ktab_kernel(f_ref, h_ref, scale_ref, sign_ref, o_ref):
    hw = HY_WIDTH
    hh, hl = _split_bf16(h_ref[...])
    f = f_ref[...]
    kk = _dot(f, hh) + _dot(f, hl)
    o_ref[...] = scale_ref[...] * (kk[:, :hw] + sign_ref[...] * kk[:, hw:])


def _hyena_ktab(filt, L):
    fm, _, scale, sign = _dft_tables(L)
    n = 2 * L
    tf = _freq_tile(L)
    return pl.pallas_call(
        _ktab_kernel,
        grid=(n // tf,),
        in_specs=[pl.BlockSpec((tf, L), lambda j: (j, 0)),
                  pl.BlockSpec((L, 2 * HY_WIDTH), lambda j: (0, 0)),
                  pl.BlockSpec((tf, 1), lambda j: (j, 0)),
                  pl.BlockSpec((tf, 1), lambda j: (j, 0))],
        out_specs=pl.BlockSpec((tf, HY_WIDTH), lambda j: (j, 0)),
        out_shape=jax.ShapeDtypeStruct((n, HY_WIDTH), F32),
        compiler_params=_cparams("parallel"),
        name="hyena_ktab",
    )(jnp.asarray(fm, dtype=BF16), filt, jnp.asarray(scale), jnp.asarray(sign))


def _hyena_conv_kernel(z_ref, x0_ref, f_ref, ft_ref, k_ref, skip_ref, o_ref, zb_scr, acc_scr, *, tf):
    j = pl.program_id(1)
    half = tf // 2

    @pl.when(j == 0)
    def _():
        zb_scr[...] = z_ref[...].astype(BF16)
        acc_scr[...] = jnp.zeros_like(acc_scr)

    zf = _dot(f_ref[...], zb_scr[...])
    re, im = zf[:half], zf[half:]
    kre, kim = k_ref[:half, :], k_ref[half:, :]
    ridx = lax.broadcasted_iota(jnp.int32, re.shape, 0)
    mix = jnp.where(jnp.logical_and(j == 0, ridx == 0), 0.0, 1.0)
    yre = re * kre - mix * (im * kim)
    yim = mix * (re * kim) + im * jnp.where(mix == 0.0, kim, kre)
    y = jnp.concatenate([yre, yim], axis=0).astype(BF16)
    acc_scr[...] += _dot(ft_ref[...], y)

    @pl.when(j == pl.num_programs(1) - 1)
    def _():
        o_ref[...] = (x0_ref[...] * (acc_scr[...] + z_ref[...] * skip_ref[...])).astype(o_ref.dtype)


def _hyena_conv(z, x0c, ktab, skip, batch, L, blk0):
    fm, fmt, _, _ = _dft_tables(L)
    n = 2 * L
    tf = _freq_tile(L)
    hw = HY_WIDTH
    return pl.pallas_call(
        functools.partial(_hyena_conv_kernel, tf=tf),
        grid=(batch, n // tf),
        in_specs=[pl.BlockSpec((L, hw), lambda b, j: (blk0 + b, 0)),
                  pl.BlockSpec((L, hw), lambda b, j: (blk0 + b, 0)),
                  pl.BlockSpec((tf, L), lambda b, j: (j, 0)),
                  pl.BlockSpec((L, tf), lambda b, j: (0, j)),
                  pl.BlockSpec((tf, hw), lambda b, j: (j, 0)),
                  pl.BlockSpec((1, hw), lambda b, j: (0, 0))],
        out_specs=pl.BlockSpec((L, hw), lambda b, j: (b, 0)),
        out_shape=jax.ShapeDtypeStruct((batch * L, hw), BF16),
        scratch_shapes=[pltpu.VMEM((L, hw), BF16), pltpu.VMEM((L, hw), F32)],
        compiler_params=_cparams("parallel", "arbitrary"),
        name="hyena_conv",
    )(z, x0c, jnp.asarray(fm, dtype=BF16), jnp.asarray(fmt, dtype=BF16), ktab, skip.reshape(1, hw))


def _mix_out_kernel(ac_ref, al_ref, bc_ref, bl_ref, wa_ref, wb_ref, hc_ref, hl_ref, mod_ref, o_ref, *, nct):
    a = _stream_tile(ac_ref, al_ref, nct)
    b = _stream_tile(bc_ref, bl_ref, nct)
    y = _dot(a, wa_ref[...]) + _dot(b, wb_ref[...])
    o_ref[...] = _stream_tile(hc_ref, hl_ref, nct) + mod_ref[2] * y


def _mix_out(a_ctx, a_lat, b_ctx, b_lat, w_out, h, mods_l, rows):
    d = w_out.shape[1]
    ka, kb = a_ctx.shape[1], b_ctx.shape[1]
    return pl.pallas_call(
        functools.partial(_mix_out_kernel, nct=rows.nct),
        grid=(rows.n_tiles,),
        in_specs=[*_stream_specs((a_ctx, a_lat, 0), rows, ka), *_stream_specs((b_ctx, b_lat, 0), rows, kb),
                  pl.BlockSpec((ka, d), lambda i: (0, 0)),
                  pl.BlockSpec((kb, d), lambda i: (ka // kb, 0)),
                  *_stream_specs(h, rows, d),
                  pl.BlockSpec((None, 6, 1, d), lambda i: (rows.mod_row(i), 0, 0, 0))],
        out_specs=pl.BlockSpec((ROW_TILE, d), lambda i: (i, 0)),
        out_shape=jax.ShapeDtypeStruct((rows.n_tiles * ROW_TILE, d), F32),
        compiler_params=_cparams("parallel"),
        name="mix_out",
    )(a_ctx, a_lat, b_ctx, b_lat, w_out, w_out, h[0], h[1], mods_l)


def _shortconv_out_kernel(bg_ref, cg_ref, xi_ref, cgp_ref, cgn_ref, xip_ref, xin_ref, cw_ref, w_ref,
                          h_ref, mod_ref, o_ref, *, rows):
    first, last = rows.seq_edges(pl.program_id(0))
    keep_p = jnp.where(first, 0.0, 1.0)
    keep_n = jnp.where(last, 0.0, 1.0)
    f32 = lambda a: a.astype(F32)
    m = f32(cg_ref[...]) * f32(xi_ref[...])
    m_prev = f32(cgp_ref[HALO - 1:HALO, :]) * f32(xip_ref[HALO - 1:HALO, :]) * keep_p
    m_next = f32(cgn_ref[0:1, :]) * f32(xin_ref[0:1, :]) * keep_n
    y = f32(bg_ref[...]) * _conv3(m, m_prev, m_next, cw_ref)
    o_ref[...] = h_ref[...] + mod_ref[2] * _dot(y.astype(BF16), w_ref[...])


def _shortconv_out(p, conv_w, w_out, h, mods_l, rows):
    d = h.shape[1]
    t_rows = p.shape[0]
    in_specs = [pl.BlockSpec((ROW_TILE, d), lambda i, j=j: (i, j)) for j in range(3)]
    in_specs += [*_halo_specs(d, 1, t_rows), *_halo_specs(d, 2, t_rows),
                 pl.BlockSpec((3, d), lambda i: (0, 0)),
                 pl.BlockSpec((d, d), lambda i: (0, 0)),
                 pl.BlockSpec((ROW_TILE, d), lambda i: (i, 0)),
                 pl.BlockSpec((None, 6, 1, d), lambda i: (rows.mod_row(i), 0, 0, 0))]
    return pl.pallas_call(
        functools.partial(_shortconv_out_kernel, rows=rows),
        grid=(rows.n_tiles,),
        in_specs=in_specs,
        out_specs=pl.BlockSpec((ROW_TILE, d), lambda i: (i, 0)),
        out_shape=jax.ShapeDtypeStruct((rows.n_tiles * ROW_TILE, d), F32),
        compiler_params=_cparams("parallel"),
        name="shortconv_out",
    )(p, p, p, p, p, p, p, conv_w, w_out, h, mods_l)


META_E, META_RANK, META_GATE = 0, 2, 4


def _lane_min_index(mask, lane_f):
    return jnp.min(jnp.where(mask, lane_f, float(LANE)), axis=1, keepdims=True)


def _router_kernel(h_ref, g_ref, mod_ref, w_ref, b_ref, v_ref, meta_ref, cnt_ref, carry_scr):
    @pl.when(pl.program_id(0) == 0)
    def _():
        carry_scr[...] = jnp.zeros_like(carry_scr)

    v = _norm_mod(h_ref[...], g_ref[...], mod_ref[3], mod_ref[4])
    v_ref[...] = v
    lg = _dot3(v, w_ref[...]) + b_ref[...]
    tm = lg.shape[0]
    lane = lax.broadcasted_iota(jnp.int32, lg.shape, 1)
    lane_f = lane.astype(F32)
    neg = -jnp.inf

    is_grp = lane < N_GROUPS
    lgm = jnp.where(is_grp, lg, neg)
    m_g = jnp.max(lgm, axis=1, keepdims=True)
    s_g = jnp.sum(jnp.where(is_grp, jnp.exp(lg - m_g), 0.0), axis=1, keepdims=True)
    p_g = 1.0 / s_g
    grp = _lane_min_index(lgm == m_g, lane_f)

    ex_lane = lane - N_GROUPS
    in_grp = jnp.logical_and(jnp.logical_and(ex_lane >= 0, ex_lane < N_EXPERTS),
                             (ex_lane // EXP_PER_GROUP).astype(F32) == grp)
    m_e = jnp.max(jnp.where(in_grp, lg, neg), axis=1, keepdims=True)
    ee = jnp.where(in_grp, jnp.exp(lg - m_e), 0.0)
    pe = ee / jnp.sum(ee, axis=1, keepdims=True)
    pe1 = jnp.where(in_grp, pe, -1.0)
    p1 = jnp.max(pe1, axis=1, keepdims=True)
    i1 = _lane_min_index(pe1 == p1, lane_f)
    pe2 = jnp.where(lane_f == i1, -1.0, pe1)
    p2 = jnp.max(pe2, axis=1, keepdims=True)
    i2 = _lane_min_index(pe2 == p2, lane_f)
    denom = p1 + p2
    g1 = p_g * p1 / denom
    g2 = p_g * p2 / denom
    e1 = i1 - float(N_GROUPS)
    e2 = i2 - float(N_GROUPS)

    oh1 = lane_f == e1
    oh2 = lane_f == e2
    row = lax.broadcasted_iota(jnp.int32, (tm, tm), 0)
    col = lax.broadcasted_iota(jnp.int32, (tm, tm), 1)
    earlier = jnp.where(row > col, 1.0, 0.0).astype(BF16)
    c1 = _dot(earlier, jnp.where(oh1, 1.0, 0.0).astype(BF16))
    c2 = _dot(earlier, jnp.where(oh2, 1.0, 0.0).astype(BF16))
    tot1 = jnp.sum(jnp.where(oh1, 1.0, 0.0), axis=0, keepdims=True)
    tot2 = jnp.sum(jnp.where(oh2, 1.0, 0.0), axis=0, keepdims=True)
    carry = carry_scr[...]
    r1 = jnp.sum(jnp.where(oh1, carry + c1, 0.0), axis=1, keepdims=True)
    r2 = jnp.sum(jnp.where(oh2, carry + tot1 + c2, 0.0), axis=1, keepdims=True)
    carry = carry + tot1 + tot2
    carry_scr[...] = carry
    cnt_ref[...] = carry

    meta = jnp.zeros_like(lg)
    for idx, val in ((META_E, e1), (META_E + 1, e2), (META_RANK, r1), (META_RANK + 1, r2),
                     (META_GATE, g1), (META_GATE + 1, g2)):
        meta = jnp.where(lane == idx, val, meta)
    meta_ref[...] = meta


def _router(h, g, mods_l, w_rt, b_rt, rows):
    d = h.shape[1]
    tm = rows.tile
    nt = rows.n_tiles * tm
    return pl.pallas_call(
        _router_kernel,
        grid=(rows.n_tiles,),
        in_specs=[pl.BlockSpec((tm, d), lambda i: (i, 0)),
                  pl.BlockSpec((1, d), lambda i: (0, 0)),
                  pl.BlockSpec((None, 6, 1, d), lambda i: (rows.mod_row(i), 0, 0, 0)),
                  pl.BlockSpec((d, LANE), lambda i: (0, 0)),
                  pl.BlockSpec((1, LANE), lambda i: (0, 0))],
        out_specs=[pl.BlockSpec((tm, d), lambda i: (i, 0)),
                   pl.BlockSpec((tm, LANE), lambda i: (i, 0)),
                   pl.BlockSpec((1, LANE), lambda i: (0, 0))],
        out_shape=[jax.ShapeDtypeStruct((nt, d), F32), jax.ShapeDtypeStruct((nt, LANE), F32),
                   jax.ShapeDtypeStruct((1, LANE), F32)],
        scratch_shapes=[pltpu.VMEM((1, LANE), F32)],
        compiler_params=_cparams("arbitrary"),
        name="router",
    )(h, g.reshape(1, d), mods_l, w_rt, b_rt)


def _dispatch_plan(meta, counts):
    bm = MOE_ROWS
    t = meta.shape[0]
    n_blocks = -(-(t * TOP_K + N_EXPERTS * (bm - 1)) // bm)
    counts = counts[0, :N_EXPERTS].astype(jnp.int32)
    nblk = (counts + bm - 1) // bm
    blk_end = jnp.cumsum(nblk).astype(jnp.int32)
    slot0 = (blk_end - nblk) * bm
    expert = meta[:, META_E:META_E + TOP_K].astype(jnp.int32)
    rank = meta[:, META_RANK:META_RANK + TOP_K].astype(jnp.int32)
    onehot = expert[:, :, None] == jnp.arange(N_EXPERTS, dtype=jnp.int32)
    slot = rank + jnp.sum(jnp.where(onehot, slot0, 0), axis=-1)
    pad_lo = jnp.concatenate([slot0 + counts, blk_end[-1:] * bm])
    pad_hi = jnp.concatenate([blk_end * bm, jnp.full((1,), n_blocks * bm, jnp.int32)])
    src = _slot_tokens(jnp.concatenate([pad_lo, pad_hi]).astype(jnp.int32), _slot_tiles(slot), n_blocks)
    blk = jnp.arange(n_blocks, dtype=jnp.int32)
    blk_e = jnp.minimum(jnp.sum(blk[:, None] >= blk_end[None, :], axis=1), N_EXPERTS - 1).astype(jnp.int32)
    blk_used = (blk < blk_end[-1]).astype(jnp.int32)
    return slot, src.reshape(n_blocks, 1, bm), blk_e, blk_used


def _slot_tiles(slot):
    return slot.reshape(slot.shape[0] // ROW_TILE, ROW_TILE, TOP_K).transpose(0, 2, 1)


def _slot_tokens_kernel(pad_ref, slot_ref, src_ref):
    i = pl.program_id(0)
    tm = slot_ref.shape[1]
    n_ranges = pad_ref.shape[0] // 2

    @pl.when(i == 0)
    def _():
        def clear(j, c):
            src_ref[j] = 0
            return c

        for q in range(n_ranges):
            lax.fori_loop(pad_ref[q], pad_ref[n_ranges + q], clear, 0)

    base = i * tm
    for r in range(tm):
        for k in range(TOP_K):
            src_ref[slot_ref[k, r]] = base + r


def _slot_tokens(pad_ranges, slot_tiles, n_blocks):
    n = n_blocks * MOE_ROWS
    grid_spec = pltpu.PrefetchScalarGridSpec(
        num_scalar_prefetch=1,
        grid=(slot_tiles.shape[0],),
        in_specs=[pl.BlockSpec((None, TOP_K, ROW_TILE), lambda i, pad: (i, 0, 0), memory_space=pltpu.SMEM)],
        out_specs=pl.BlockSpec((n,), lambda i, pad: (0,), memory_space=pltpu.SMEM),
    )
    return pl.pallas_call(
        _slot_tokens_kernel,
        grid_spec=grid_spec,
        out_shape=jax.ShapeDtypeStruct((n,), jnp.int32),
        compiler_params=_cparams("arbitrary"),
        name="slot_tokens",
    )(pad_ranges, slot_tiles)


def _row_copies_start(src_hbm, idx_ref, k, dst, sem):
    for r in range(dst.shape[0]):
        pltpu.async_copy(src_hbm.at[pl.ds(idx_ref[k, r], 1), :], dst.at[pl.ds(r, 1), :], sem,
                         priority=r % DMA_THREADS)


def _row_copies_wait(src_hbm, dst, sem):
    pltpu.make_async_copy(src_hbm.at[pl.ds(0, dst.shape[0]), :], dst, sem).wait()


def _expert_kernel(blk_e_ref, blk_used_ref, src_ref, src_next_ref, v_hbm, w1_ref, w3_ref, w2_ref, y_ref,
                   xbuf, w1_scr, w3_scr, w2_scr, sem):
    j = pl.program_id(0)
    nb = pl.num_programs(0)
    used = blk_used_ref[j] > 0
    new_expert = jnp.logical_or(j == 0, blk_e_ref[j] != blk_e_ref[jnp.maximum(j - 1, 0)])

    @pl.when(j == 0)
    def _():
        _row_copies_start(v_hbm, src_ref, 0, xbuf.at[0], sem.at[0])

    @pl.when(jnp.logical_and(used, new_expert))
    def _():
        w1_scr[...] = w1_ref[...].astype(BF16)
        w3_scr[...] = w3_ref[...].astype(BF16)
        w2_scr[...] = w2_ref[...].astype(BF16)

    @pl.when(used)
    def _():
        b = j % 2
        _row_copies_wait(v_hbm, xbuf.at[b], sem.at[b])
        x = xbuf[b].astype(BF16)
        _row_copies_start(v_hbm, src_next_ref, 0, xbuf.at[1 - b], sem.at[1 - b])
        hid = _silu(_dot(x, w1_scr[...])) * _dot(x, w3_scr[...])
        y_ref[...] = _dot(hid.astype(BF16), w2_scr[...])

        @pl.when(jnp.logical_or(j == nb - 1, blk_used_ref[jnp.minimum(j + 1, nb - 1)] == 0))
        def _():
            _row_copies_wait(v_hbm, xbuf.at[1 - b], sem.at[1 - b])

    @pl.when(jnp.logical_not(used))
    def _():
        y_ref[...] = jnp.zeros_like(y_ref)


def _experts(v, src, blk_e, blk_used, w1, w3, w2):
    t, d = v.shape
    n_blocks = src.shape[0]
    bm = MOE_ROWS
    de = w1.shape[2]

    def next_block(j, e, used):
        nxt = jnp.minimum(j + 1, n_blocks - 1)
        return jnp.where(used[nxt] > 0, nxt, j)

    grid_spec = pltpu.PrefetchScalarGridSpec(
        num_scalar_prefetch=2,
        grid=(n_blocks,),
        in_specs=[pl.BlockSpec((None, 1, bm), lambda j, e, u: (j, 0, 0), memory_space=pltpu.SMEM),
                  pl.BlockSpec((None, 1, bm), lambda j, e, u: (next_block(j, e, u), 0, 0), memory_space=pltpu.SMEM),
                  pl.BlockSpec(memory_space=pl.ANY),
                  pl.BlockSpec((None, d, de), lambda j, e, u: (e[j], 0, 0)),
                  pl.BlockSpec((None, d, de), lambda j, e, u: (e[j], 0, 0)),
                  pl.BlockSpec((None, de, d), lambda j, e, u: (e[j], 0, 0))],
        out_specs=pl.BlockSpec((bm, d), lambda j, e, u: (j, 0)),
        scratch_shapes=[pltpu.VMEM((2, bm, d), F32), pltpu.VMEM((d, de), BF16), pltpu.VMEM((d, de), BF16),
                        pltpu.VMEM((de, d), BF16), pltpu.SemaphoreType.DMA((2,))],
    )
    return pl.pallas_call(
        _expert_kernel,
        grid_spec=grid_spec,
        out_shape=jax.ShapeDtypeStruct((n_blocks * bm, d), F32),
        compiler_params=_cparams("arbitrary"),
        name="experts",
    )(blk_e, blk_used, src, src, v, w1, w3, w2)


def _combined_tile(slot_ref, slot_next_ref, hc_ref, hl_ref, nct, meta_ref, modp_ref, y_hbm, buf, sem, overlap_with):
    i = pl.program_id(0)
    n = pl.num_programs(0)

    @pl.when(i == 0)
    def _():
        for k in range(TOP_K):
            _row_copies_start(y_hbm, slot_ref, k, buf.at[0, k], sem.at[0])

    b = i % 2
    for k in range(TOP_K):
        _row_copies_wait(y_hbm, buf.at[b, k], sem.at[b])
    meta = meta_ref[...]
    f = meta[:, META_GATE:META_GATE + 1] * buf[b, 0] + meta[:, META_GATE + 1:META_GATE + 2] * buf[b, 1]
    hn = _stream_tile(hc_ref, hl_ref, nct) + modp_ref[5] * f
    for k in range(TOP_K):
        _row_copies_start(y_hbm, slot_next_ref, k, buf.at[1 - b, k], sem.at[1 - b])
    overlap_with(hn)

    @pl.when(i == n - 1)
    def _():
        for k in range(TOP_K):
            _row_copies_wait(y_hbm, buf.at[1 - b, k], sem.at[1 - b])


def _combine_mm_kernel(slot_ref, slot_next_ref, hc_ref, hl_ref, meta_ref, modp_ref, g_ref, mod_ref, w_ref, y_hbm,
                       hn_ref, o_ref, buf, sem, *, nct):
    def project(hn):
        hn_ref[...] = hn
        u = _norm_mod(hn, g_ref[...], mod_ref[0], mod_ref[1])
        o_ref[...] = _dot(u.astype(BF16), w_ref[...]).astype(o_ref.dtype)

    _combined_tile(slot_ref, slot_next_ref, hc_ref, hl_ref, nct, meta_ref, modp_ref, y_hbm, buf, sem, project)


def _combine_norm_kernel(slot_ref, slot_next_ref, hc_ref, hl_ref, meta_ref, modp_ref, g_ref, y_hbm, o_ref, buf, sem,
                         *, nct):
    def finish(hn):
        res = hn * lax.rsqrt(jnp.mean(hn * hn, axis=-1, keepdims=True) + EPS) * g_ref[...]
        if len(o_ref.shape) == 2:
            o_ref[...] = res
        else:
            n_r = o_ref.shape[0]
            for w in range(o_ref.shape[1]):
                o_ref[:, w, :] = res[w * n_r:(w + 1) * n_r, :]

    _combined_tile(slot_ref, slot_next_ref, hc_ref, hl_ref, nct, meta_ref, modp_ref, y_hbm, buf, sem, finish)


def _combine_then(h, ys, slot_tiles, meta, mods_prev, rows, g, mods_l=None, w=None, name="combine_norm", tile0=0,
                  grid_seq=None):
    d = g.shape[0]
    n_tiles = rows.n_tiles
    slot_spec = lambda fn: pl.BlockSpec((None, TOP_K, ROW_TILE), lambda i: (tile0 + fn(i), 0, 0),
                                        memory_space=pltpu.SMEM)
    mod_spec = pl.BlockSpec((None, 6, 1, d), lambda i: (rows.mod_row(i), 0, 0, 0))
    tile_in = lambda width: pl.BlockSpec((ROW_TILE, width), lambda i: (tile0 + i, 0))
    tile = lambda width: pl.BlockSpec((ROW_TILE, width), lambda i: (i, 0))
    rows_shape = lambda width: jax.ShapeDtypeStruct((n_tiles * ROW_TILE, width), F32)
    h_specs = _stream_specs(h, rows, d)
    h_lat = h[1]
    if grid_seq is not None and w is not None:
        nct = rows.nct
        h_specs[1] = _grid_spec(grid_seq, d, h[2] * ROW_TILE // grid_seq)(lambda i: jnp.maximum(i - nct, 0))
        h_lat = _grid_view(h_lat, grid_seq)
    in_specs = [slot_spec(lambda i: i), slot_spec(lambda i: jnp.minimum(i + 1, n_tiles - 1)),
                *h_specs, tile_in(LANE), mod_spec, pl.BlockSpec((1, d), lambda i: (0, 0))]
    args = [slot_tiles, slot_tiles, h[0], h_lat, meta, mods_prev, g.reshape(1, d)]
    if w is None and grid_seq is not None:
        assert rows.nct == 0
        kern, out_specs = _combine_norm_kernel, _grid_spec(grid_seq, d, 0)(lambda i: i)
        out_shape = jax.eval_shape(lambda: _grid_view(jnp.zeros((n_tiles * ROW_TILE, d), F32), grid_seq))
    elif w is None:
        kern, out_specs, out_shape = _combine_norm_kernel, tile(d), rows_shape(d)
    else:
        nw = w.shape[1]
        in_specs += [mod_spec, pl.BlockSpec((d, nw), lambda i: (0, 0))]
        args += [mods_l, w]
        kern, out_specs = _combine_mm_kernel, [tile(d), tile(nw)]
        out_shape = [rows_shape(d), jax.ShapeDtypeStruct((n_tiles * ROW_TILE, nw), BF16)]
    return pl.pallas_call(
        functools.partial(kern, nct=rows.nct),
        grid=(n_tiles,),
        in_specs=in_specs + [pl.BlockSpec(memory_space=pl.ANY)],
        out_specs=out_specs,
        out_shape=out_shape,
        scratch_shapes=[pltpu.VMEM((2, TOP_K, ROW_TILE, d), F32), pltpu.SemaphoreType.DMA((2,))],
        compiler_params=_cparams("arbitrary"),
        name=name,
    )(*args, ys)


def _even_w_in(w):
    d = w.shape[0]
    n_main = 2 * GLA_QK + 2 * GLA_V
    ranks = w[:, n_main:n_main + 2 * GLA_RANK]
    hy = w[:, n_main + 2 * GLA_RANK:]
    pad = jnp.zeros((d, R_COLS - 2 * GLA_RANK), w.dtype)
    return jnp.concatenate([w[:, :n_main], hy, ranks, pad], axis=1).astype(BF16)


def _rank_proj(wa, first_row):
    return jnp.zeros((R_COLS, GLA_QK), F32).at[first_row:first_row + GLA_RANK].set(wa)


def _grid_transpose(h_lat, batch, a, b):
    d = h_lat.shape[1]
    return h_lat.reshape(batch, a, b, d).transpose(0, 2, 1, 3).reshape(-1, d)


def kernel(x, c, ctx, c_ctx, mod_w, mod_b, norm_mix, norm_ffn, norm_final, ev_w_in, ev_w_out, gla_wa_f, gla_ba_f, gla_wa_b, gla_ba_b, gla_norm, hy_conv_w, hy_conv_b, hy_w1, hy_b1, hy_f1, hy_w2, hy_b2, hy_f2, hy_w3, hy_skip, od_w_in, od_conv_w, od_w_out, rt_w_grp, rt_b_grp, rt_w_exp, rt_b_exp, ex_w1, ex_w3, ex_w2):
    batch, s, d = x.shape
    lc = ctx.shape[1]
    depth = mod_w.shape[0]
    tc, tl = batch * lc, batch * s
    assert lc % ROW_TILE == 0 and s % ROW_TILE == 0 and tc % s == 0 and s % GRID_W == 0
    assert lc % GLA_CHUNK == 0 and s % GLA_CHUNK == 0
    assert s % ROUTER_TILE == 0 and tc % ROUTER_TILE == 0
    assert depth % 2 == 0
    grid_rows = s // GRID_W
    assert ROW_TILE % grid_rows == 0 and GRID_W % (ROW_TILE // grid_rows) == 0 and (ROW_TILE // grid_rows) % 8 == 0

    n_cond = -(-(batch + 1) // 8) * 8
    cond = jnp.concatenate([c, c_ctx[None], jnp.zeros((n_cond - batch - 1, d), F32)], axis=0)
    mods = _mods(cond, mod_w, mod_b).reshape(depth, n_cond, 6, 1, d)

    hs = (ctx.reshape(tc, d), x.reshape(tl, d), 0)
    col_major_now = False
    moe = None

    def lat_part(a, fn):
        return fn(a) if a.shape[0] == tl else jnp.concatenate([a[:tc], fn(a[tc:])], axis=0)

    for l in range(depth):
        i = l // 2
        even = l % 2 == 0
        ctx_out = l < depth - 1
        col_major = i % 2 == 1
        routing = [] if moe is None else [moe[1], moe[2]]
        if col_major != col_major_now:
            perm = ((lambda a: _grid_transpose(a, batch, grid_rows, GRID_W)) if col_major
                    else (lambda a: _grid_transpose(a, batch, GRID_W, grid_rows)))
            grid_read = moe is not None and col_major
            if not grid_read:
                lat = hs[1][hs[2] * ROW_TILE:hs[2] * ROW_TILE + tl]
                hs = (hs[0], perm(lat), 0)
            routing = [lat_part(a, perm) for a in routing]
            col_major_now = col_major
        else:
            grid_read = False
        arr_rows = _Rows(batch, lc, s, not ctx_out)
        if not ctx_out:
            hs = (hs[1], hs[1], hs[2])
        mods_l = mods[l]
        w_in = _even_w_in(ev_w_in[i]) if even else od_w_in[i].astype(BF16)
        name = "even_in" if even else "odd_in"
        if moe is None:
            p = _normmod_mm(hs, norm_mix[l], mods_l, w_in, arr_rows, 0, 1, name)
        else:
            tile0 = routing[0].shape[0] // ROW_TILE - arr_rows.n_tiles
            h, p = _combine_then(hs, moe[0], _slot_tiles(routing[0]), routing[1], moe[3], arr_rows,
                                 norm_mix[l], mods_l, w_in, name, tile0, s if grid_read else None)
            hs = _as_stream(h, arr_rows)

        if even:
            waf = _rank_proj(gla_wa_f[i], 0)
            wab = _rank_proj(gla_wa_b[i], GLA_RANK)
            baf, bab = gla_ba_f[i].reshape(1, -1), gla_ba_b[i].reshape(1, -1)
            gain = gla_norm[i].reshape(1, -1)
            zeros = jnp.zeros((batch, 2, 2 * GLA_DV, 2 * GLA_DK), F32)
            filt_args = (hy_w1[i], hy_b1[i], hy_f1[i], hy_w2[i], hy_b2[i], hy_f2[i], hy_w3[i])
            x0c, z = _hyena_pre(p, hy_conv_w[i], hy_conv_b[i], arr_rows)
            gla_c, sc_f, sc_b = _gla(p, waf, wab, baf, bab, gain, zeros, zeros, batch, lc, 0)
            gla_l, _, _ = _gla(p, waf, wab, baf, bab, gain, sc_f, sc_b, batch, s, tc // s)
            kt_c = _hyena_ktab(_hyena_filters(lc, *filt_args), lc)
            kt_l = _hyena_ktab(_hyena_filters(s, *filt_args), s)
            hy_c = _hyena_conv(z, x0c, kt_c, hy_skip[i], batch, lc, 0)
            hy_l = _hyena_conv(z, x0c, kt_l, hy_skip[i], batch, s, tc // s)
            h = _mix_out(gla_c, gla_l, hy_c, hy_l, ev_w_out[i].astype(BF16), hs, mods_l, arr_rows)
        else:
            h = _shortconv_out(p, od_conv_w[i], od_w_out[i].astype(BF16), h, mods_l, arr_rows)

        w_rt = jnp.concatenate([rt_w_grp[l], rt_w_exp[l],
                                jnp.zeros((d, LANE - N_GROUPS - N_EXPERTS), F32)], axis=1)
        b_rt = jnp.concatenate([rt_b_grp[l], rt_b_exp[l],
                                jnp.zeros((LANE - N_GROUPS - N_EXPERTS,), F32)]).reshape(1, LANE)
        v, meta, counts = _router(h, norm_ffn[l], mods_l, w_rt, b_rt,
                                  _Rows(batch, lc, s, not ctx_out, ROUTER_TILE))
        slot, src, blk_e, blk_used = _dispatch_plan(meta, counts)
        ys = _experts(v, src, blk_e, blk_used, ex_w1[l], ex_w3[l], ex_w2[l])
        moe = (ys, slot, meta, mods_l)
        hs = _as_stream(h, arr_rows)

    out = _combine_then(hs, moe[0], _slot_tiles(moe[1]), moe[2], moe[3], arr_rows, norm_final,
                        grid_seq=s if col_major_now else None)
    return out.reshape(batch, s, d)
```

```python
import functools
import math

import numpy as np
import jax
import jax.numpy as jnp
from jax import lax
from jax.experimental import pallas as pl
from jax.experimental.pallas import tpu as pltpu

F32 = jnp.float32
BF16 = jnp.bfloat16

EPS = 1e-6
GRID_W = 64

GLA_HEADS = 4
GLA_DK = 64
GLA_DV = 128
GLA_RANK = 16
GLA_TAU = 16.0
GLA_CHUNK = 64
GLA_SLAB = 256
GLA_STEP_UNROLL = 4
GLA_QK = GLA_HEADS * GLA_DK
GLA_V = GLA_HEADS * GLA_DV

HY_WIDTH = 512
HY_EMB = 33
HY_BANDS = (HY_EMB - 1) // 2
HY_HIDDEN = 64
HY_FAST_DECAY = 0.3
HY_SLOW_DECAY = 1.5
HY_TARGET = 1e-2

N_GROUPS = 4
EXP_PER_GROUP = 8
N_EXPERTS = N_GROUPS * EXP_PER_GROUP
TOP_K = 2

LANE = 128
ROW_TILE = 256
MOE_ROWS = 256
ROUTER_TILE = 512
R_COLS = LANE
VMEM_LIMIT = 56 * 1024 * 1024
DMA_THREADS = 2


def _cparams(*sem):
    return pltpu.CompilerParams(dimension_semantics=sem, vmem_limit_bytes=VMEM_LIMIT)


def _split_bf16(a):
    hi = a.astype(BF16)
    lo = (a - hi.astype(F32)).astype(BF16)
    return hi, lo


def _dot(a, b):
    return jnp.dot(a, b, preferred_element_type=F32)


def _dot_nt(a, b):
    return lax.dot_general(a, b, (((1,), (1,)), ((), ())), preferred_element_type=F32)


def _dot_tn(a, b):
    return lax.dot_general(a, b, (((0,), (0,)), ((), ())), preferred_element_type=F32)


def _dot3(a, b):
    ah, al = _split_bf16(a)
    bh, bl = _split_bf16(b)
    return _dot(ah, bh) + _dot(ah, bl) + _dot(al, bh)


def _silu(x):
    return x / (1.0 + jnp.exp(-x))


def _log_sigmoid(x):
    return jnp.minimum(x, 0.0) - jnp.log1p(jnp.exp(-jnp.abs(x)))


def _norm_mod(x, g, shift, scale):
    y = x * lax.rsqrt(jnp.mean(x * x, axis=-1, keepdims=True) + EPS)
    return (y * g) * (1.0 + scale) + shift


def _mods_kernel(s_ref, w_ref, b_ref, o_ref):
    s = s_ref[...]
    s = _silu(s)
    o_ref[...] = _dot3(s, w_ref[...]) + b_ref[...]


def _mods(cond, mod_w, mod_b):
    depth, d, n = mod_w.shape
    r = cond.shape[0]
    tn = 1024
    return pl.pallas_call(
        _mods_kernel,
        grid=(depth, n // tn),
        in_specs=[pl.BlockSpec((r, d), lambda l, j: (0, 0)),
                  pl.BlockSpec((None, d, tn), lambda l, j: (l, 0, j)),
                  pl.BlockSpec((None, 1, tn), lambda l, j: (l, 0, j))],
        out_specs=pl.BlockSpec((None, r, tn), lambda l, j: (l, 0, j)),
        out_shape=jax.ShapeDtypeStruct((depth, r, n), F32),
        compiler_params=_cparams("parallel", "parallel"),
        name="mods",
    )(cond, mod_w, mod_b.reshape(depth, 1, n))


class _Rows:
    def __init__(self, batch, lc, s, lat_only, tile=ROW_TILE):
        self.batch = batch
        self.tile = tile
        self.nct = 0 if lat_only else batch * lc // tile
        self.tps = s // tile
        self.tpc = lc // tile
        self.n_tiles = self.nct + batch * self.tps

    def mod_row(self, i):
        return jnp.where(i < self.nct, self.batch, (i - self.nct) // self.tps)

    def seq_edges(self, i):
        pos_c = i % self.tpc
        pos_l = (i - self.nct) % self.tps
        is_c = i < self.nct
        first = jnp.where(is_c, pos_c == 0, pos_l == 0)
        last = jnp.where(is_c, pos_c == self.tpc - 1, pos_l == self.tps - 1)
        return first, last


def _stream_specs(stream, rows, width):
    _, _, lat0 = stream
    nct = rows.nct
    ctx_spec = pl.BlockSpec((ROW_TILE, width), lambda i: (jnp.clip(i, 0, max(nct - 1, 0)), 0))
    lat_spec = pl.BlockSpec((ROW_TILE, width), lambda i: (jnp.maximum(i - nct, 0) + lat0, 0))
    return [ctx_spec, lat_spec]


def _grid_view(a, s):
    grid_rows = s // GRID_W
    cpt = ROW_TILE // grid_rows
    return a.reshape(a.shape[0] // s, grid_rows, GRID_W // cpt, cpt, a.shape[1])


def _grid_spec(s, d, first_seq):
    grid_rows = s // GRID_W
    cpt = ROW_TILE // grid_rows
    groups = GRID_W // cpt
    return lambda lat_tile: pl.BlockSpec((None, grid_rows, None, cpt, d),
                                         lambda i: (first_seq + lat_tile(i) // groups, 0, lat_tile(i) % groups, 0, 0))


def _lat_tile(l_ref):
    if len(l_ref.shape) == 2:
        return l_ref[...]
    return jnp.concatenate([l_ref[:, w, :] for w in range(l_ref.shape[1])], axis=0)


def _stream_tile(c_ref, l_ref, nct):
    if nct == 0:
        return _lat_tile(l_ref)
    return jnp.where(pl.program_id(0) < nct, c_ref[...], _lat_tile(l_ref))


def _as_stream(a, rows):
    return (a, a, rows.nct)


def _normmod_mm_kernel(hc_ref, hl_ref, g_ref, mod_ref, w_ref, o_ref, *, shift_i, scale_i, nct):
    u = _norm_mod(_stream_tile(hc_ref, hl_ref, nct), g_ref[...], mod_ref[shift_i], mod_ref[scale_i])
    o_ref[...] = _dot(u.astype(BF16), w_ref[...]).astype(o_ref.dtype)


def _normmod_mm(h, g, mods_l, w, rows, shift_i, scale_i, name):
    d = w.shape[0]
    n = w.shape[1]
    return pl.pallas_call(
        functools.partial(_normmod_mm_kernel, shift_i=shift_i, scale_i=scale_i, nct=rows.nct),
        grid=(rows.n_tiles,),
        in_specs=[*_stream_specs(h, rows, d),
                  pl.BlockSpec((1, d), lambda i: (0, 0)),
                  pl.BlockSpec((None, 6, 1, d), lambda i: (rows.mod_row(i), 0, 0, 0)),
                  pl.BlockSpec((d, n), lambda i: (0, 0))],
        out_specs=pl.BlockSpec((ROW_TILE, n), lambda i: (i, 0)),
        out_shape=jax.ShapeDtypeStruct((rows.n_tiles * ROW_TILE, n), BF16),
        compiler_params=_cparams("parallel"),
        name=name,
    )(h[0], h[1], g.reshape(1, d), mods_l, w)


def _gla_kernel(q_ref, k_ref, v_ref, g_ref, r_ref, waf_ref, wab_ref, baf_ref, bab_ref, gain_ref,
                s0f_ref, s0b_ref, o_ref, sf_ref, sb_ref, gl_scr, tot_scr, qd_scr, ds_scr, o_scr, st_scr,
                *, seq_len):
    C, SL = GLA_CHUNK, GLA_SLAB
    cps = SL // C
    n_chunks, n_slabs = seq_len // C, seq_len // SL
    head_of_lane = lax.broadcasted_iota(jnp.int32, (SL, 2 * GLA_DK), 1) // GLA_DK
    row = lax.broadcasted_iota(jnp.int32, (SL, SL), 0)
    col = lax.broadcasted_iota(jnp.int32, (SL, SL), 1)
    same_chunk = (row // C) == (col // C)
    srow = lax.broadcasted_iota(jnp.int32, (2 * GLA_DV, 2 * GLA_DK), 0) // GLA_DV
    scol = lax.broadcasted_iota(jnp.int32, (2 * GLA_DV, 2 * GLA_DK), 1) // GLA_DK
    same_head = srow == scol
    r = r_ref[...].astype(F32)

    def direction(wa_ref, ba_ref, s0_ref, s_out_ref, forward):
        mask = jnp.logical_and(same_chunk, (row >= col) if forward else (row <= col))
        tri = jnp.where(mask, 1.0, 0.0).astype(BF16)
        gl_scr[...] = _log_sigmoid(_dot3(r, wa_ref[...]) + ba_ref[...]) * (1.0 / GLA_TAU)

        def slab(s, carry):
            rows = pl.ds(pl.multiple_of(s * SL, SL), SL)
            gl = gl_scr[rows, :]
            g_hi, g_lo = _split_bf16(gl)
            b = _dot(tri, g_hi) + _dot(tri, g_lo)
            b3 = b.reshape(cps, C, 2 * GLA_DK)
            last = b3[:, C - 1:C, :] if forward else b3[:, 0:1, :]
            tot = jnp.broadcast_to(last, b3.shape).reshape(SL, 2 * GLA_DK)
            tot_scr[rows, :] = tot
            q = q_ref[rows, :].astype(F32) * (GLA_DK ** -0.5)
            k = k_ref[rows, :].astype(F32)
            v = v_ref[rows, :].astype(BF16)
            qd = q * jnp.exp(b)
            kd = (k * jnp.exp(-b)).astype(BF16)
            kr = (k * jnp.exp(tot - b)).astype(BF16)
            qd_scr[rows, :] = qd.astype(BF16)
            o_parts = []
            for h in range(2):
                qh = jnp.where(head_of_lane == h, qd, 0.0).astype(BF16)
                a = jnp.where(mask, _dot_nt(qh, kd), 0.0).astype(BF16)
                o_parts.append(_dot(a, v[:, h * GLA_DV:(h + 1) * GLA_DV]))
            o = jnp.concatenate(o_parts, axis=1)
            if forward:
                o_scr[rows, :] = o
            else:
                o_scr[rows, :] += o
            for c in range(cps):
                ds = _dot_tn(v[c * C:(c + 1) * C], kr[c * C:(c + 1) * C])
                ds_scr[s * cps + c] = jnp.where(same_head, ds, 0.0)
            return carry

        lax.fori_loop(0, n_slabs, slab, 0, unroll=2)
        st_scr[...] = s0_ref[...]

        def step(i, carry):
            c = i if forward else n_chunks - 1 - i
            first = pl.multiple_of(c * C, C)
            rows = pl.ds(first, C)
            st = st_scr[...]
            o_scr[rows, :] += _dot_nt(qd_scr[rows, :], st.astype(BF16))
            st_scr[...] = st * jnp.exp(tot_scr[pl.ds(first, 1), :]) + ds_scr[c]
            return carry

        lax.fori_loop(0, n_chunks, step, 0, unroll=GLA_STEP_UNROLL)
        s_out_ref[...] = st_scr[...]

    direction(waf_ref, baf_ref, s0f_ref, sf_ref, True)
    direction(wab_ref, bab_ref, s0b_ref, sb_ref, False)
    gain = gain_ref[...]

    def readout(s, carry):
        rows = pl.ds(pl.multiple_of(s * SL, SL), SL)
        o = o_scr[rows, :]
        outs = []
        for h in range(2):
            oh = o[:, h * GLA_DV:(h + 1) * GLA_DV]
            outs.append(oh * lax.rsqrt(jnp.mean(oh * oh, axis=-1, keepdims=True) + EPS) * gain)
        o_ref[rows, :] = (jnp.concatenate(outs, axis=1) * _silu(g_ref[rows, :].astype(F32))).astype(o_ref.dtype)
        return carry

    lax.fori_loop(0, n_slabs, readout, 0)


def _gla(p, waf, wab, baf, bab, gain, s0f, s0b, batch, seq_len, blk0):
    dk2, dv2 = 2 * GLA_DK, 2 * GLA_DV
    seq = lambda width, cb: pl.BlockSpec((seq_len, width), lambda b, hp: (blk0 + b, cb(hp)))
    state_spec = pl.BlockSpec((None, None, dv2, dk2), lambda b, hp: (b, hp, 0, 0))
    in_specs = [seq(dk2, lambda hp: hp),
                seq(dk2, lambda hp: GLA_QK // dk2 + hp),
                seq(dv2, lambda hp: 2 * GLA_QK // dv2 + hp),
                seq(dv2, lambda hp: (2 * GLA_QK + GLA_V) // dv2 + hp),
                seq(R_COLS, lambda hp: (2 * GLA_QK + 2 * GLA_V + 3 * HY_WIDTH) // R_COLS),
                pl.BlockSpec((R_COLS, dk2), lambda b, hp: (0, hp)),
                pl.BlockSpec((R_COLS, dk2), lambda b, hp: (0, hp)),
                pl.BlockSpec((1, dk2), lambda b, hp: (0, hp)),
                pl.BlockSpec((1, dk2), lambda b, hp: (0, hp)),
                pl.BlockSpec((1, GLA_DV), lambda b, hp: (0, 0)),
                state_spec, state_spec]
    state_shape = jax.ShapeDtypeStruct((batch, 2, dv2, dk2), F32)
    return pl.pallas_call(
        functools.partial(_gla_kernel, seq_len=seq_len),
        grid=(batch, 2),
        in_specs=in_specs,
        out_specs=[pl.BlockSpec((seq_len, dv2), lambda b, hp: (b, hp)), state_spec, state_spec],
        out_shape=[jax.ShapeDtypeStruct((batch * seq_len, GLA_V), BF16), state_shape, state_shape],
        scratch_shapes=[pltpu.VMEM((seq_len, dk2), F32),
                        pltpu.VMEM((seq_len, dk2), F32),
                        pltpu.VMEM((seq_len, dk2), BF16),
                        pltpu.VMEM((seq_len // GLA_CHUNK, dv2, dk2), F32),
                        pltpu.VMEM((seq_len, dv2), F32),
                        pltpu.VMEM((dv2, dk2), F32)],
        compiler_params=_cparams("parallel", "parallel"),
        name="gla",
    )(p, p, p, p, p, waf, wab, baf, bab, gain, s0f, s0b)


def _conv3(m, prev_row, next_row, w_ref):
    n = m.shape[0]
    ridx = lax.broadcasted_iota(jnp.int32, m.shape, 0)
    m_prev = jnp.where(ridx == 0, prev_row, pltpu.roll(m, 1, 0))
    m_next = jnp.where(ridx == n - 1, next_row, pltpu.roll(m, n - 1, 0))
    return w_ref[0:1, :] * m_prev + w_ref[1:2, :] * m + w_ref[2:3, :] * m_next


HALO = 16


def _halo_specs(width, col_block, t_rows):
    g = ROW_TILE // HALO
    last = t_rows // HALO - 1
    prev = pl.BlockSpec((HALO, width), lambda i: (jnp.maximum(i * g - 1, 0), col_block))
    nxt = pl.BlockSpec((HALO, width), lambda i: (jnp.minimum((i + 1) * g, last), col_block))
    return prev, nxt


def _hyena_pre_kernel(x0_ref, x1_ref, v_ref, x0p_ref, x0n_ref, x1p_ref, x1n_ref, vp_ref, vn_ref,
                      w_ref, b_ref, x0c_ref, z_ref, *, rows):
    first, last = rows.seq_edges(pl.program_id(0))
    keep_p = jnp.where(first, 0.0, 1.0)
    keep_n = jnp.where(last, 0.0, 1.0)
    hw = HY_WIDTH

    def conv(ref, p_ref, n_ref, j):
        w = w_ref.at[:, j * hw:(j + 1) * hw]
        y = _conv3(ref[...].astype(F32), p_ref[HALO - 1:HALO, :].astype(F32) * keep_p,
                   n_ref[0:1, :].astype(F32) * keep_n, w)
        return y + b_ref[:, j * hw:(j + 1) * hw]

    x0c_ref[...] = conv(x0_ref, x0p_ref, x0n_ref, 0)
    z_ref[...] = conv(v_ref, vp_ref, vn_ref, 2) * conv(x1_ref, x1p_ref, x1n_ref, 1)


def _hyena_pre(p, conv_w, conv_b, rows):
    t_rows = p.shape[0]
    hw = HY_WIDTH
    cb0 = (2 * GLA_QK + 2 * GLA_V) // hw
    in_specs = [pl.BlockSpec((ROW_TILE, hw), lambda i, j=j: (i, cb0 + j)) for j in range(3)]
    for j in range(3):
        in_specs.extend(_halo_specs(hw, cb0 + j, t_rows))
    in_specs += [pl.BlockSpec((3, 3 * hw), lambda i: (0, 0)), pl.BlockSpec((1, 3 * hw), lambda i: (0, 0))]
    out_spec = pl.BlockSpec((ROW_TILE, hw), lambda i: (i, 0))
    shape = jax.ShapeDtypeStruct((rows.n_tiles * ROW_TILE, hw), F32)
    return pl.pallas_call(
        functools.partial(_hyena_pre_kernel, rows=rows),
        grid=(rows.n_tiles,),
        in_specs=in_specs,
        out_specs=[out_spec, out_spec],
        out_shape=[shape, shape],
        compiler_params=_cparams("parallel"),
        name="hyena_pre",
    )(p, p, p, p, p, p, p, p, p, conv_w, conv_b.reshape(1, 3 * hw))


def _filter_kernel(z_ref, w1_ref, b1_ref, f1_ref, w2_ref, b2_ref, f2_ref, w3_ref, win_ref, o_ref):
    hh = jnp.sin(f1_ref[...] * (_dot3(z_ref[...], w1_ref[...]) + b1_ref[...]))
    hh = jnp.sin(f2_ref[...] * (_dot3(hh, w2_ref[...]) + b2_ref[...]))
    win = win_ref[...]
    o_ref[...] = _dot3(hh, w3_ref[...]) * jnp.concatenate([win, win], axis=1)


@functools.lru_cache(maxsize=None)
def _filter_features(L):
    t = np.linspace(0.0, 1.0, L, dtype=np.float32)[:, None]
    pos = np.arange(L, dtype=np.float32)[:, None]
    bands = np.linspace(1e-4, HY_BANDS - 1, HY_BANDS, dtype=np.float32)[None]
    ang = (np.float32(2.0 * math.pi / L) * pos * bands).astype(np.float32)
    z = np.concatenate([t, np.cos(ang), np.sin(ang)], axis=-1).astype(np.float32)
    z = np.pad(z, ((0, 0), (0, LANE - HY_EMB)))
    max_decay = math.log(HY_TARGET) / HY_FAST_DECAY
    min_decay = math.log(HY_TARGET) / HY_SLOW_DECAY
    deltas = np.linspace(min_decay, max_decay, HY_WIDTH, dtype=np.float32)
    window = np.exp(-t * np.abs(deltas)[None]).astype(np.float32)
    return z, window


def _hyena_filters(L, w1, b1, f1, w2, b2, f2, w3):
    z, window = _filter_features(L)
    w1p = jnp.pad(w1, ((0, LANE - HY_EMB), (0, 0)))
    tl = min(L, 512)
    full = lambda a: pl.BlockSpec(a.shape, lambda i: (0,) * a.ndim)
    row = lambda a: a.reshape(1, -1)
    ops = [w1p, row(b1), row(f1), w2, row(b2), row(f2), w3]
    return pl.pallas_call(
        _filter_kernel,
        grid=(L // tl,),
        in_specs=[pl.BlockSpec((tl, LANE), lambda i: (i, 0))] + [full(a) for a in ops]
                 + [pl.BlockSpec((tl, HY_WIDTH), lambda i: (i, 0))],
        out_specs=pl.BlockSpec((tl, 2 * HY_WIDTH), lambda i: (i, 0)),
        out_shape=jax.ShapeDtypeStruct((L, 2 * HY_WIDTH), F32),
        compiler_params=_cparams("parallel"),
        name="hyena_filters",
    )(jnp.asarray(z), *ops, jnp.asarray(window))


def _freq_tile(L):
    return min(2 * L, 1024)


@functools.lru_cache(maxsize=None)
def _dft_tables(L):
    n = 2 * L
    tf = _freq_tile(L)
    half = tf // 2
    t = np.arange(L, dtype=np.int64)[None, :]
    fm = np.zeros((n, L), np.float64)
    scale = np.zeros((n, 1), np.float64)
    sign = np.zeros((n, 1), np.float64)
    for j in range(n // tf):
        k = (np.arange(half, dtype=np.int64) + j * half)[:, None]
        ang = 2.0 * np.pi * ((k * t) % n).astype(np.float64) / n
        fm[j * tf:j * tf + half] = np.cos(ang)
        fm[j * tf + half:(j + 1) * tf] = -np.sin(ang)
        scale[j * tf:(j + 1) * tf] = 2.0 / n
        sign[j * tf:j * tf + half] = 1.0
        sign[j * tf + half:(j + 1) * tf] = -1.0
    fm[half] = np.cos(np.pi * t[0])
    scale[0] = 1.0 / n
    scale[half] = 1.0 / n
    sign[half] = 1.0
    return (fm.astype(np.float32), np.ascontiguousarray(fm.T).astype(np.float32),
            scale.astype(np.float32), sign.astype(np.float32))


def _ktab_kernel(f_ref, h_ref, scale_ref, sign_ref, o_ref):
    hw = HY_WIDTH
    hh, hl = _split_bf16(h_ref[...])
    f = f_ref[...]
    kk = _dot(f, hh) + _dot(f, hl)
    o_ref[...] = scale_ref[...] * (kk[:, :hw] + sign_ref[...] * kk[:, hw:])


def _hyena_ktab(filt, L):
    fm, _, scale, sign = _dft_tables(L)
    n = 2 * L
    tf = _freq_tile(L)
    return pl.pallas_call(
        _ktab_kernel,
        grid=(n // tf,),
        in_specs=[pl.BlockSpec((tf, L), lambda j: (j, 0)),
                  pl.BlockSpec((L, 2 * HY_WIDTH), lambda j: (0, 0)),
                  pl.BlockSpec((tf, 1), lambda j: (j, 0)),
                  pl.BlockSpec((tf, 1), lambda j: (j, 0))],
        out_specs=pl.BlockSpec((tf, HY_WIDTH), lambda j: (j, 0)),
        out_shape=jax.ShapeDtypeStruct((n, HY_WIDTH), F32),
        compiler_params=_cparams("parallel"),
        name="hyena_ktab",
    )(jnp.asarray(fm, dtype=BF16), filt, jnp.asarray(scale), jnp.asarray(sign))


def _hyena_conv_kernel(z_ref, x0_ref, f_ref, ft_ref, k_ref, skip_ref, o_ref, zb_scr, acc_scr, *, tf):
    j = pl.program_id(1)
    half = tf // 2

    @pl.when(j == 0)
    def _():
        zb_scr[...] = z_ref[...].astype(BF16)
        acc_scr[...] = jnp.zeros_like(acc_scr)

    zf = _dot(f_ref[...], zb_scr[...])
    re, im = zf[:half], zf[half:]
    kre, kim = k_ref[:half, :], k_ref[half:, :]
    ridx = lax.broadcasted_iota(jnp.int32, re.shape, 0)
    mix = jnp.where(jnp.logical_and(j == 0, ridx == 0), 0.0, 1.0)
    yre = re * kre - mix * (im * kim)
    yim = mix * (re * kim) + im * jnp.where(mix == 0.0, kim, kre)
    y = jnp.concatenate([yre, yim], axis=0).astype(BF16)
    acc_scr[...] += _dot(ft_ref[...], y)

    @pl.when(j == pl.num_programs(1) - 1)
    def _():
        o_ref[...] = (x0_ref[...] * (acc_scr[...] + z_ref[...] * skip_ref[...])).astype(o_ref.dtype)


def _hyena_conv(z, x0c, ktab, skip, batch, L, blk0):
    fm, fmt, _, _ = _dft_tables(L)
    n = 2 * L
    tf = _freq_tile(L)
    hw = HY_WIDTH
    return pl.pallas_call(
        functools.partial(_hyena_conv_kernel, tf=tf),
        grid=(batch, n // tf),
        in_specs=[pl.BlockSpec((L, hw), lambda b, j: (blk0 + b, 0)),
                  pl.BlockSpec((L, hw), lambda b, j: (blk0 + b, 0)),
                  pl.BlockSpec((tf, L), lambda b, j: (j, 0)),
                  pl.BlockSpec((L, tf), lambda b, j: (0, j)),
                  pl.BlockSpec((tf, hw), lambda b, j: (j, 0)),
                  pl.BlockSpec((1, hw), lambda b, j: (0, 0))],
        out_specs=pl.BlockSpec((L, hw), lambda b, j: (b, 0)),
        out_shape=jax.ShapeDtypeStruct((batch * L, hw), BF16),
        scratch_shapes=[pltpu.VMEM((L, hw), BF16), pltpu.VMEM((L, hw), F32)],
        compiler_params=_cparams("parallel", "arbitrary"),
        name="hyena_conv",
    )(z, x0c, jnp.asarray(fm, dtype=BF16), jnp.asarray(fmt, dtype=BF16), ktab, skip.reshape(1, hw))


def _mix_out_kernel(ac_ref, al_ref, bc_ref, bl_ref, wa_ref, wb_ref, hc_ref, hl_ref, mod_ref, o_ref, *, nct):
    a = _stream_tile(ac_ref, al_ref, nct)
    b = _stream_tile(bc_ref, bl_ref, nct)
    y = _dot(a, wa_ref[...]) + _dot(b, wb_ref[...])
    o_ref[...] = _stream_tile(hc_ref, hl_ref, nct) + mod_ref[2] * y


def _mix_out(a_ctx, a_lat, b_ctx, b_lat, w_out, h, mods_l, rows):
    d = w_out.shape[1]
    ka, kb = a_ctx.shape[1], b_ctx.shape[1]
    return pl.pallas_call(
        functools.partial(_mix_out_kernel, nct=rows.nct),
        grid=(rows.n_tiles,),
        in_specs=[*_stream_specs((a_ctx, a_lat, 0), rows, ka), *_stream_specs((b_ctx, b_lat, 0), rows, kb),
                  pl.BlockSpec((ka, d), lambda i: (0, 0)),
                  pl.BlockSpec((kb, d), lambda i: (ka // kb, 0)),
                  *_stream_specs(h, rows, d),
                  pl.BlockSpec((None, 6, 1, d), lambda i: (rows.mod_row(i), 0, 0, 0))],
        out_specs=pl.BlockSpec((ROW_TILE, d), lambda i: (i, 0)),
        out_shape=jax.ShapeDtypeStruct((rows.n_tiles * ROW_TILE, d), F32),
        compiler_params=_cparams("parallel"),
        name="mix_out",
    )(a_ctx, a_lat, b_ctx, b_lat, w_out, w_out, h[0], h[1], mods_l)


def _shortconv_out_kernel(bg_ref, cg_ref, xi_ref, cgp_ref, cgn_ref, xip_ref, xin_ref, cw_ref, w_ref,
                          h_ref, mod_ref, o_ref, *, rows):
    first, last = rows.seq_edges(pl.program_id(0))
    keep_p = jnp.where(first, 0.0, 1.0)
    keep_n = jnp.where(last, 0.0, 1.0)
    f32 = lambda a: a.astype(F32)
    m = f32(cg_ref[...]) * f32(xi_ref[...])
    m_prev = f32(cgp_ref[HALO - 1:HALO, :]) * f32(xip_ref[HALO - 1:HALO, :]) * keep_p
    m_next = f32(cgn_ref[0:1, :]) * f32(xin_ref[0:1, :]) * keep_n
    y = f32(bg_ref[...]) * _conv3(m, m_prev, m_next, cw_ref)
    o_ref[...] = h_ref[...] + mod_ref[2] * _dot(y.astype(BF16), w_ref[...])


def _shortconv_out(p, conv_w, w_out, h, mods_l, rows):
    d = h.shape[1]
    t_rows = p.shape[0]
    in_specs = [pl.BlockSpec((ROW_TILE, d), lambda i, j=j: (i, j)) for j in range(3)]
    in_specs += [*_halo_specs(d, 1, t_rows), *_halo_specs(d, 2, t_rows),
                 pl.BlockSpec((3, d), lambda i: (0, 0)),
                 pl.BlockSpec((d, d), lambda i: (0, 0)),
                 pl.BlockSpec((ROW_TILE, d), lambda i: (i, 0)),
                 pl.BlockSpec((None, 6, 1, d), lambda i: (rows.mod_row(i), 0, 0, 0))]
    return pl.pallas_call(
        functools.partial(_shortconv_out_kernel, rows=rows),
        grid=(rows.n_tiles,),
        in_specs=in_specs,
        out_specs=pl.BlockSpec((ROW_TILE, d), lambda i: (i, 0)),
        out_shape=jax.ShapeDtypeStruct((rows.n_tiles * ROW_TILE, d), F32),
        compiler_params=_cparams("parallel"),
        name="shortconv_out",
    )(p, p, p, p, p, p, p, conv_w, w_out, h, mods_l)


META_E, META_RANK, META_GATE = 0, 2, 4


def _lane_min_index(mask, lane_f):
    return jnp.min(jnp.where(mask, lane_f, float(LANE)), axis=1, keepdims=True)


def _router_kernel(h_ref, g_ref, mod_ref, w_ref, b_ref, v_ref, meta_ref, cnt_ref, carry_scr):
    @pl.when(pl.program_id(0) == 0)
    def _():
        carry_scr[...] = jnp.zeros_like(carry_scr)

    v = _norm_mod(h_ref[...], g_ref[...], mod_ref[3], mod_ref[4])
    v_ref[...] = v
    lg = _dot3(v, w_ref[...]) + b_ref[...]
    tm = lg.shape[0]
    lane = lax.broadcasted_iota(jnp.int32, lg.shape, 1)
    lane_f = lane.astype(F32)
    neg = -jnp.inf

    is_grp = lane < N_GROUPS
    lgm = jnp.where(is_grp, lg, neg)
    m_g = jnp.max(lgm, axis=1, keepdims=True)
    s_g = jnp.sum(jnp.where(is_grp, jnp.exp(lg - m_g), 0.0), axis=1, keepdims=True)
    p_g = 1.0 / s_g
    grp = _lane_min_index(lgm == m_g, lane_f)

    ex_lane = lane - N_GROUPS
    in_grp = jnp.logical_and(jnp.logical_and(ex_lane >= 0, ex_lane < N_EXPERTS),
                             (ex_lane // EXP_PER_GROUP).astype(F32) == grp)
    m_e = jnp.max(jnp.where(in_grp, lg, neg), axis=1, keepdims=True)
    ee = jnp.where(in_grp, jnp.exp(lg - m_e), 0.0)
    pe = ee / jnp.sum(ee, axis=1, keepdims=True)
    pe1 = jnp.where(in_grp, pe, -1.0)
    p1 = jnp.max(pe1, axis=1, keepdims=True)
    i1 = _lane_min_index(pe1 == p1, lane_f)
    pe2 = jnp.where(lane_f == i1, -1.0, pe1)
    p2 = jnp.max(pe2, axis=1, keepdims=True)
    i2 = _lane_min_index(pe2 == p2, lane_f)
    denom = p1 + p2
    g1 = p_g * p1 / denom
    g2 = p_g * p2 / denom
    e1 = i1 - float(N_GROUPS)
    e2 = i2 - float(N_GROUPS)

    oh1 = lane_f == e1
    oh2 = lane_f == e2
    row = lax.broadcasted_iota(jnp.int32, (tm, tm), 0)
    col = lax.broadcasted_iota(jnp.int32, (tm, tm), 1)
    earlier = jnp.where(row > col, 1.0, 0.0).astype(BF16)
    c1 = _dot(earlier, jnp.where(oh1, 1.0, 0.0).astype(BF16))
    c2 = _dot(earlier, jnp.where(oh2, 1.0, 0.0).astype(BF16))
    tot1 = jnp.sum(jnp.where(oh1, 1.0, 0.0), axis=0, keepdims=True)
    tot2 = jnp.sum(jnp.where(oh2, 1.0, 0.0), axis=0, keepdims=True)
    carry = carry_scr[...]
    r1 = jnp.sum(jnp.where(oh1, carry + c1, 0.0), axis=1, keepdims=True)
    r2 = jnp.sum(jnp.where(oh2, carry + tot1 + c2, 0.0), axis=1, keepdims=True)
    carry = carry + tot1 + tot2
    carry_scr[...] = carry
    cnt_ref[...] = carry

    meta = jnp.zeros_like(lg)
    for idx, val in ((META_E, e1), (META_E + 1, e2), (META_RANK, r1), (META_RANK + 1, r2),
                     (META_GATE, g1), (META_GATE + 1, g2)):
        meta = jnp.where(lane == idx, val, meta)
    meta_ref[...] = meta


def _router(h, g, mods_l, w_rt, b_rt, rows):
    d = h.shape[1]
    tm = rows.tile
    nt = rows.n_tiles * tm
    return pl.pallas_call(
        _router_kernel,
        grid=(rows.n_tiles,),
        in_specs=[pl.BlockSpec((tm, d), lambda i: (i, 0)),
                  pl.BlockSpec((1, d), lambda i: (0, 0)),
                  pl.BlockSpec((None, 6, 1, d), lambda i: (rows.mod_row(i), 0, 0, 0)),
                  pl.BlockSpec((d, LANE), lambda i: (0, 0)),
                  pl.BlockSpec((1, LANE), lambda i: (0, 0))],
        out_specs=[pl.BlockSpec((tm, d), lambda i: (i, 0)),
                   pl.BlockSpec((tm, LANE), lambda i: (i, 0)),
                   pl.BlockSpec((1, LANE), lambda i: (0, 0))],
        out_shape=[jax.ShapeDtypeStruct((nt, d), F32), jax.ShapeDtypeStruct((nt, LANE), F32),
                   jax.ShapeDtypeStruct((1, LANE), F32)],
        scratch_shapes=[pltpu.VMEM((1, LANE), F32)],
        compiler_params=_cparams("arbitrary"),
        name="router",
    )(h, g.reshape(1, d), mods_l, w_rt, b_rt)


def _dispatch_plan(meta, counts):
    bm = MOE_ROWS
    t = meta.shape[0]
    n_blocks = -(-(t * TOP_K + N_EXPERTS * (bm - 1)) // bm)
    counts = counts[0, :N_EXPERTS].astype(jnp.int32)
    nblk = (counts + bm - 1) // bm
    blk_end = jnp.cumsum(nblk).astype(jnp.int32)
    slot0 = (blk_end - nblk) * bm
    expert = meta[:, META_E:META_E + TOP_K].astype(jnp.int32)
    rank = meta[:, META_RANK:META_RANK + TOP_K].astype(jnp.int32)
    onehot = expert[:, :, None] == jnp.arange(N_EXPERTS, dtype=jnp.int32)
    slot = rank + jnp.sum(jnp.where(onehot, slot0, 0), axis=-1)
    pad_lo = jnp.concatenate([slot0 + counts, blk_end[-1:] * bm])
    pad_hi = jnp.concatenate([blk_end * bm, jnp.full((1,), n_blocks * bm, jnp.int32)])
    src = _slot_tokens(jnp.concatenate([pad_lo, pad_hi]).astype(jnp.int32), _slot_tiles(slot), n_blocks)
    blk = jnp.arange(n_blocks, dtype=jnp.int32)
    blk_e = jnp.minimum(jnp.sum(blk[:, None] >= blk_end[None, :], axis=1), N_EXPERTS - 1).astype(jnp.int32)
    blk_used = (blk < blk_end[-1]).astype(jnp.int32)
    return slot, src.reshape(n_blocks, 1, bm), blk_e, blk_used


def _slot_tiles(slot):
    return slot.reshape(slot.shape[0] // ROW_TILE, ROW_TILE, TOP_K).transpose(0, 2, 1)


def _slot_tokens_kernel(pad_ref, slot_ref, src_ref):
    i = pl.program_id(0)
    tm = slot_ref.shape[1]
    n_ranges = pad_ref.shape[0] // 2

    @pl.when(i == 0)
    def _():
        def clear(j, c):
            src_ref[j] = 0
            return c

        for q in range(n_ranges):
            lax.fori_loop(pad_ref[q], pad_ref[n_ranges + q], clear, 0)

    base = i * tm
    for r in range(tm):
        for k in range(TOP_K):
            src_ref[slot_ref[k, r]] = base + r


def _slot_tokens(pad_ranges, slot_tiles, n_blocks):
    n = n_blocks * MOE_ROWS
    grid_spec = pltpu.PrefetchScalarGridSpec(
        num_scalar_prefetch=1,
        grid=(slot_tiles.shape[0],),
        in_specs=[pl.BlockSpec((None, TOP_K, ROW_TILE), lambda i, pad: (i, 0, 0), memory_space=pltpu.SMEM)],
        out_specs=pl.BlockSpec((n,), lambda i, pad: (0,), memory_space=pltpu.SMEM),
    )
    return pl.pallas_call(
        _slot_tokens_kernel,
        grid_spec=grid_spec,
        out_shape=jax.ShapeDtypeStruct((n,), jnp.int32),
        compiler_params=_cparams("arbitrary"),
        name="slot_tokens",
    )(pad_ranges, slot_tiles)


def _row_copies_start(src_hbm, idx_ref, k, dst, sem):
    for r in range(dst.shape[0]):
        pltpu.async_copy(src_hbm.at[pl.ds(idx_ref[k, r], 1), :], dst.at[pl.ds(r, 1), :], sem,
                         priority=r % DMA_THREADS)


def _row_copies_wait(src_hbm, dst, sem):
    pltpu.make_async_copy(src_hbm.at[pl.ds(0, dst.shape[0]), :], dst, sem).wait()


def _expert_kernel(blk_e_ref, blk_used_ref, src_ref, src_next_ref, v_hbm, w1_ref, w3_ref, w2_ref, y_ref,
                   xbuf, w1_scr, w3_scr, w2_scr, sem):
    j = pl.program_id(0)
    nb = pl.num_programs(0)
    used = blk_used_ref[j] > 0
    new_expert = jnp.logical_or(j == 0, blk_e_ref[j] != blk_e_ref[jnp.maximum(j - 1, 0)])

    @pl.when(j == 0)
    def _():
        _row_copies_start(v_hbm, src_ref, 0, xbuf.at[0], sem.at[0])

    @pl.when(jnp.logical_and(used, new_expert))
    def _():
        w1_scr[...] = w1_ref[...].astype(BF16)
        w3_scr[...] = w3_ref[...].astype(BF16)
        w2_scr[...] = w2_ref[...].astype(BF16)

    @pl.when(used)
    def _():
        b = j % 2
        _row_copies_wait(v_hbm, xbuf.at[b], sem.at[b])
        x = xbuf[b].astype(BF16)
        _row_copies_start(v_hbm, src_next_ref, 0, xbuf.at[1 - b], sem.at[1 - b])
        hid = _silu(_dot(x, w1_scr[...])) * _dot(x, w3_scr[...])
        y_ref[...] = _dot(hid.astype(BF16), w2_scr[...])

        @pl.when(jnp.logical_or(j == nb - 1, blk_used_ref[jnp.minimum(j + 1, nb - 1)] == 0))
        def _():
            _row_copies_wait(v_hbm, xbuf.at[1 - b], sem.at[1 - b])

    @pl.when(jnp.logical_not(used))
    def _():
        y_ref[...] = jnp.zeros_like(y_ref)


def _experts(v, src, blk_e, blk_used, w1, w3, w2, layer):
    t, d = v.shape
    n_blocks = src.shape[0]
    bm = MOE_ROWS
    de = w1.shape[3]

    def next_block(j, e, used):
        nxt = jnp.minimum(j + 1, n_blocks - 1)
        return jnp.where(used[nxt] > 0, nxt, j)

    grid_spec = pltpu.PrefetchScalarGridSpec(
        num_scalar_prefetch=2,
        grid=(n_blocks,),
        in_specs=[pl.BlockSpec((None, 1, bm), lambda j, e, u: (j, 0, 0), memory_space=pltpu.SMEM),
                  pl.BlockSpec((None, 1, bm), lambda j, e, u: (next_block(j, e, u), 0, 0), memory_space=pltpu.SMEM),
                  pl.BlockSpec(memory_space=pl.ANY),
                  pl.BlockSpec((None, None, d, de), lambda j, e, u: (layer, e[j], 0, 0)),
                  pl.BlockSpec((None, None, d, de), lambda j, e, u: (layer, e[j], 0, 0)),
                  pl.BlockSpec((None, None, de, d), lambda j, e, u: (layer, e[j], 0, 0))],
        out_specs=pl.BlockSpec((bm, d), lambda j, e, u: (j, 0)),
        scratch_shapes=[pltpu.VMEM((2, bm, d), F32), pltpu.VMEM((d, de), BF16), pltpu.VMEM((d, de), BF16),
                        pltpu.VMEM((de, d), BF16), pltpu.SemaphoreType.DMA((2,))],
    )
    return pl.pallas_call(
        _expert_kernel,
        grid_spec=grid_spec,
        out_shape=jax.ShapeDtypeStruct((n_blocks * bm, d), F32),
        compiler_params=_cparams("arbitrary"),
        name="experts",
    )(blk_e, blk_used, src, src, v, w1, w3, w2)


def _combined_tile(slot_ref, slot_next_ref, hc_ref, hl_ref, nct, meta_ref, modp_ref, y_hbm, buf, sem, overlap_with):
    i = pl.program_id(0)
    n = pl.num_programs(0)

    @pl.when(i == 0)
    def _():
        for k in range(TOP_K):
            _row_copies_start(y_hbm, slot_ref, k, buf.at[0, k], sem.at[0])

    b = i % 2
    for k in range(TOP_K):
        _row_copies_wait(y_hbm, buf.at[b, k], sem.at[b])
    meta = meta_ref[...]
    f = meta[:, META_GATE:META_GATE + 1] * buf[b, 0] + meta[:, META_GATE + 1:META_GATE + 2] * buf[b, 1]
    hn = _stream_tile(hc_ref, hl_ref, nct) + modp_ref[5] * f
    for k in range(TOP_K):
        _row_copies_start(y_hbm, slot_next_ref, k, buf.at[1 - b, k], sem.at[1 - b])
    overlap_with(hn)

    @pl.when(i == n - 1)
    def _():
        for k in range(TOP_K):
            _row_copies_wait(y_hbm, buf.at[1 - b, k], sem.at[1 - b])


def _combine_mm_kernel(slot_ref, slot_next_ref, hc_ref, hl_ref, meta_ref, modp_ref, g_ref, mod_ref, w_ref, y_hbm,
                       hn_ref, o_ref, buf, sem, *, nct):
    def project(hn):
        hn_ref[...] = hn
        u = _norm_mod(hn, g_ref[...], mod_ref[0], mod_ref[1])
        o_ref[...] = _dot(u.astype(BF16), w_ref[...]).astype(o_ref.dtype)

    _combined_tile(slot_ref, slot_next_ref, hc_ref, hl_ref, nct, meta_ref, modp_ref, y_hbm, buf, sem, project)


def _combine_norm_kernel(slot_ref, slot_next_ref, hc_ref, hl_ref, meta_ref, modp_ref, g_ref, y_hbm, o_ref, buf, sem,
                         *, nct):
    def finish(hn):
        res = hn * lax.rsqrt(jnp.mean(hn * hn, axis=-1, keepdims=True) + EPS) * g_ref[...]
        if len(o_ref.shape) == 2:
            o_ref[...] = res
        else:
            n_r = o_ref.shape[0]
            for w in range(o_ref.shape[1]):
                o_ref[:, w, :] = res[w * n_r:(w + 1) * n_r, :]

    _combined_tile(slot_ref, slot_next_ref, hc_ref, hl_ref, nct, meta_ref, modp_ref, y_hbm, buf, sem, finish)


def _combine_then(h, ys, slot_tiles, meta, mods_prev, rows, g, mods_l=None, w=None, name="combine_norm", tile0=0,
                  grid_seq=None):
    d = g.shape[0]
    n_tiles = rows.n_tiles
    slot_spec = lambda fn: pl.BlockSpec((None, TOP_K, ROW_TILE), lambda i: (tile0 + fn(i), 0, 0),
                                        memory_space=pltpu.SMEM)
    mod_spec = pl.BlockSpec((None, 6, 1, d), lambda i: (rows.mod_row(i), 0, 0, 0))
    tile_in = lambda width: pl.BlockSpec((ROW_TILE, width), lambda i: (tile0 + i, 0))
    tile = lambda width: pl.BlockSpec((ROW_TILE, width), lambda i: (i, 0))
    rows_shape = lambda width: jax.ShapeDtypeStruct((n_tiles * ROW_TILE, width), F32)
    h_specs = _stream_specs(h, rows, d)
    h_lat = h[1]
    if grid_seq is not None and w is not None:
        nct = rows.nct
        h_specs[1] = _grid_spec(grid_seq, d, h[2] * ROW_TILE // grid_seq)(lambda i: jnp.maximum(i - nct, 0))
        h_lat = _grid_view(h_lat, grid_seq)
    in_specs = [slot_spec(lambda i: i), slot_spec(lambda i: jnp.minimum(i + 1, n_tiles - 1)),
                *h_specs, tile_in(LANE), mod_spec, pl.BlockSpec((1, d), lambda i: (0, 0))]
    args = [slot_tiles, slot_tiles, h[0], h_lat, meta, mods_prev, g.reshape(1, d)]
    if w is None and grid_seq is not None:
        assert rows.nct == 0
        kern, out_specs = _combine_norm_kernel, _grid_spec(grid_seq, d, 0)(lambda i: i)
        out_shape = jax.eval_shape(lambda: _grid_view(jnp.zeros((n_tiles * ROW_TILE, d), F32), grid_seq))
    elif w is None:
        kern, out_specs, out_shape = _combine_norm_kernel, tile(d), rows_shape(d)
    else:
        nw = w.shape[1]
        in_specs += [mod_spec, pl.BlockSpec((d, nw), lambda i: (0, 0))]
        args += [mods_l, w]
        kern, out_specs = _combine_mm_kernel, [tile(d), tile(nw)]
        out_shape = [rows_shape(d), jax.ShapeDtypeStruct((n_tiles * ROW_TILE, nw), BF16)]
    return pl.pallas_call(
        functools.partial(kern, nct=rows.nct),
        grid=(n_tiles,),
        in_specs=in_specs + [pl.BlockSpec(memory_space=pl.ANY)],
        out_specs=out_specs,
        out_shape=out_shape,
        scratch_shapes=[pltpu.VMEM((2, TOP_K, ROW_TILE, d), F32), pltpu.SemaphoreType.DMA((2,))],
        compiler_params=_cparams("arbitrary"),
        name=name,
    )(*args, ys)


def _even_w_in(w):
    d = w.shape[0]
    n_main = 2 * GLA_QK + 2 * GLA_V
    ranks = w[:, n_main:n_main + 2 * GLA_RANK]
    hy = w[:, n_main + 2 * GLA_RANK:]
    pad = jnp.zeros((d, R_COLS - 2 * GLA_RANK), w.dtype)
    return jnp.concatenate([w[:, :n_main], hy, ranks, pad], axis=1).astype(BF16)


def _rank_proj(wa, first_row):
    return jnp.zeros((R_COLS, GLA_QK), F32).at[first_row:first_row + GLA_RANK].set(wa)


def _grid_transpose(h_lat, batch, a, b):
    d = h_lat.shape[1]
    return h_lat.reshape(batch, a, b, d).transpose(0, 2, 1, 3).reshape(-1, d)


def kernel(x, c, ctx, c_ctx, mod_w, mod_b, norm_mix, norm_ffn, norm_final, ev_w_in, ev_w_out, gla_wa_f, gla_ba_f, gla_wa_b, gla_ba_b, gla_norm, hy_conv_w, hy_conv_b, hy_w1, hy_b1, hy_f1, hy_w2, hy_b2, hy_f2, hy_w3, hy_skip, od_w_in, od_conv_w, od_w_out, rt_w_grp, rt_b_grp, rt_w_exp, rt_b_exp, ex_w1, ex_w3, ex_w2):
    batch, s, d = x.shape
    lc = ctx.shape[1]
    depth = mod_w.shape[0]
    tc, tl = batch * lc, batch * s
    assert lc % ROW_TILE == 0 and s % ROW_TILE == 0 and tc % s == 0 and s % GRID_W == 0
    assert lc % GLA_CHUNK == 0 and s % GLA_CHUNK == 0
    assert s % ROUTER_TILE == 0 and tc % ROUTER_TILE == 0
    assert depth % 2 == 0
    grid_rows = s // GRID_W
    assert ROW_TILE % grid_rows == 0 and GRID_W % (ROW_TILE // grid_rows) == 0 and (ROW_TILE // grid_rows) % 8 == 0

    n_cond = -(-(batch + 1) // 8) * 8
    cond = jnp.concatenate([c, c_ctx[None], jnp.zeros((n_cond - batch - 1, d), F32)], axis=0)
    mods = _mods(cond, mod_w, mod_b).reshape(depth, n_cond, 6, 1, d)

    hs = (ctx.reshape(tc, d), x.reshape(tl, d), 0)
    col_major_now = False
    moe = None

    def lat_part(a, fn):
        return fn(a) if a.shape[0] == tl else jnp.concatenate([a[:tc], fn(a[tc:])], axis=0)

    for l in range(depth):
        i = l // 2
        even = l % 2 == 0
        ctx_out = l < depth - 1
        col_major = i % 2 == 1
        routing = [] if moe is None else [moe[1], moe[2]]
        if col_major != col_major_now:
            perm = ((lambda a: _grid_transpose(a, batch, grid_rows, GRID_W)) if col_major
                    else (lambda a: _grid_transpose(a, batch, GRID_W, grid_rows)))
            grid_read = moe is not None and col_major
            if not grid_read:
                lat = hs[1][hs[2] * ROW_TILE:hs[2] * ROW_TILE + tl]
                hs = (hs[0], perm(lat), 0)
            routing = [lat_part(a, perm) for a in routing]
            col_major_now = col_major
        else:
            grid_read = False
        arr_rows = _Rows(batch, lc, s, not ctx_out)
        if not ctx_out:
            hs = (hs[1], hs[1], hs[2])
        mods_l = mods[l]
        w_in = _even_w_in(ev_w_in[i]) if even else od_w_in[i].astype(BF16)
        name = "even_in" if even else "odd_in"
        if moe is None:
            p = _normmod_mm(hs, norm_mix[l], mods_l, w_in, arr_rows, 0, 1, name)
        else:
            tile0 = routing[0].shape[0] // ROW_TILE - arr_rows.n_tiles
            h, p = _combine_then(hs, moe[0], _slot_tiles(routing[0]), routing[1], moe[3], arr_rows,
                                 norm_mix[l], mods_l, w_in, name, tile0, s if grid_read else None)
            hs = _as_stream(h, arr_rows)

        if even:
            waf = _rank_proj(gla_wa_f[i], 0)
            wab = _rank_proj(gla_wa_b[i], GLA_RANK)
            baf, bab = gla_ba_f[i].reshape(1, -1), gla_ba_b[i].reshape(1, -1)
            gain = gla_norm[i].reshape(1, -1)
            zeros = jnp.zeros((batch, 2, 2 * GLA_DV, 2 * GLA_DK), F32)
            filt_args = (hy_w1[i], hy_b1[i], hy_f1[i], hy_w2[i], hy_b2[i], hy_f2[i], hy_w3[i])
            x0c, z = _hyena_pre(p, hy_conv_w[i], hy_conv_b[i], arr_rows)
            gla_c, sc_f, sc_b = _gla(p, waf, wab, baf, bab, gain, zeros, zeros, batch, lc, 0)
            gla_l, _, _ = _gla(p, waf, wab, baf, bab, gain, sc_f, sc_b, batch, s, tc // s)
            kt_c = _hyena_ktab(_hyena_filters(lc, *filt_args), lc)
            kt_l = _hyena_ktab(_hyena_filters(s, *filt_args), s)
            hy_c = _hyena_conv(z, x0c, kt_c, hy_skip[i], batch, lc, 0)
            hy_l = _hyena_conv(z, x0c, kt_l, hy_skip[i], batch, s, tc // s)
            h = _mix_out(gla_c, gla_l, hy_c, hy_l, ev_w_out[i].astype(BF16), hs, mods_l, arr_rows)
        else:
            h = _shortconv_out(p, od_conv_w[i], od_w_out[i].astype(BF16), h, mods_l, arr_rows)

        w_rt = jnp.concatenate([rt_w_grp[l], rt_w_exp[l],
                                jnp.zeros((d, LANE - N_GROUPS - N_EXPERTS), F32)], axis=1)
        b_rt = jnp.concatenate([rt_b_grp[l], rt_b_exp[l],
                                jnp.zeros((LANE - N_GROUPS - N_EXPERTS,), F32)]).reshape(1, LANE)
        v, meta, counts = _router(h, norm_ffn[l], mods_l, w_rt, b_rt,
                                  _Rows(batch, lc, s, not ctx_out, ROUTER_TILE))
        slot, src, blk_e, blk_used = _dispatch_plan(meta, counts)
        ys = _experts(v, src, blk_e, blk_used, ex_w1, ex_w3, ex_w2, l)
        moe = (ys, slot, meta, mods_l)
        hs = _as_stream(h, arr_rows)

    out = _combine_then(hs, moe[0], _slot_tiles(moe[1]), moe[2], moe[3], arr_rows, norm_final,
                        grid_seq=s if col_major_now else None)
    return out.reshape(batch, s, d)
```

```python
import functools
import math

import numpy as np
import jax
import jax.numpy as jnp
from jax import lax
from jax.experimental import pallas as pl
from jax.experimental.pallas import tpu as pltpu

F32 = jnp.float32
BF16 = jnp.bfloat16

EPS = 1e-6
GRID_W = 64

GLA_HEADS = 4
GLA_DK = 64
GLA_DV = 128
GLA_RANK = 16
GLA_TAU = 16.0
GLA_CHUNK = 64
GLA_SLAB = 256
GLA_STEP_UNROLL = 4
GLA_QK = GLA_HEADS * GLA_DK
GLA_V = GLA_HEADS * GLA_DV

HY_WIDTH = 512
HY_EMB = 33
HY_BANDS = (HY_EMB - 1) // 2
HY_HIDDEN = 64
HY_FAST_DECAY = 0.3
HY_SLOW_DECAY = 1.5
HY_TARGET = 1e-2

N_GROUPS = 4
EXP_PER_GROUP = 8
N_EXPERTS = N_GROUPS * EXP_PER_GROUP
TOP_K = 2

LANE = 128
ROW_TILE = 256
MOE_ROWS = 256
ROUTER_TILE = 1024
R_COLS = LANE
VMEM_LIMIT = 56 * 1024 * 1024
DMA_THREADS = 2


def _cparams(*sem):
    return pltpu.CompilerParams(dimension_semantics=sem, vmem_limit_bytes=VMEM_LIMIT)


def _split_bf16(a):
    hi = a.astype(BF16)
    lo = (a - hi.astype(F32)).astype(BF16)
    return hi, lo


def _dot(a, b):
    return jnp.dot(a, b, preferred_element_type=F32)


def _dot_nt(a, b):
    return lax.dot_general(a, b, (((1,), (1,)), ((), ())), preferred_element_type=F32)


def _dot_tn(a, b):
    return lax.dot_general(a, b, (((0,), (0,)), ((), ())), preferred_element_type=F32)


def _dot3(a, b):
    ah, al = _split_bf16(a)
    bh, bl = _split_bf16(b)
    return _dot(ah, bh) + _dot(ah, bl) + _dot(al, bh)


def _silu(x):
    return x / (1.0 + jnp.exp(-x))


def _log_sigmoid(x):
    return jnp.minimum(x, 0.0) - jnp.log1p(jnp.exp(-jnp.abs(x)))


def _norm_mod(x, g, shift, scale):
    y = x * lax.rsqrt(jnp.mean(x * x, axis=-1, keepdims=True) + EPS)
    return (y * g) * (1.0 + scale) + shift


def _mods_kernel(s_ref, w_ref, b_ref, o_ref):
    s = s_ref[...]
    s = _silu(s)
    o_ref[...] = _dot3(s, w_ref[...]) + b_ref[...]


def _mods(cond, mod_w, mod_b):
    depth, d, n = mod_w.shape
    r = cond.shape[0]
    tn = 1024
    return pl.pallas_call(
        _mods_kernel,
        grid=(depth, n // tn),
        in_specs=[pl.BlockSpec((r, d), lambda l, j: (0, 0)),
                  pl.BlockSpec((None, d, tn), lambda l, j: (l, 0, j)),
                  pl.BlockSpec((None, 1, tn), lambda l, j: (l, 0, j))],
        out_specs=pl.BlockSpec((None, r, tn), lambda l, j: (l, 0, j)),
        out_shape=jax.ShapeDtypeStruct((depth, r, n), F32),
        compiler_params=_cparams("parallel", "parallel"),
        name="mods",
    )(cond, mod_w, mod_b.reshape(depth, 1, n))


class _Rows:
    def __init__(self, batch, lc, s, lat_only, tile=ROW_TILE):
        self.batch = batch
        self.tile = tile
        self.nct = 0 if lat_only else batch * lc // tile
        self.tps = s // tile
        self.tpc = lc // tile
        self.n_tiles = self.nct + batch * self.tps

    def mod_row(self, i):
        return jnp.where(i < self.nct, self.batch, (i - self.nct) // self.tps)

    def seq_edges(self, i):
        pos_c = i % self.tpc
        pos_l = (i - self.nct) % self.tps
        is_c = i < self.nct
        first = jnp.where(is_c, pos_c == 0, pos_l == 0)
        last = jnp.where(is_c, pos_c == self.tpc - 1, pos_l == self.tps - 1)
        return first, last


def _stream_specs(stream, rows, width):
    _, _, lat0 = stream
    nct = rows.nct
    ctx_spec = pl.BlockSpec((ROW_TILE, width), lambda i: (jnp.clip(i, 0, max(nct - 1, 0)), 0))
    lat_spec = pl.BlockSpec((ROW_TILE, width), lambda i: (jnp.maximum(i - nct, 0) + lat0, 0))
    return [ctx_spec, lat_spec]


def _grid_view(a, s):
    grid_rows = s // GRID_W
    cpt = ROW_TILE // grid_rows
    return a.reshape(a.shape[0] // s, grid_rows, GRID_W // cpt, cpt, a.shape[1])


def _grid_spec(s, d, first_seq):
    grid_rows = s // GRID_W
    cpt = ROW_TILE // grid_rows
    groups = GRID_W // cpt
    return lambda lat_tile: pl.BlockSpec((None, grid_rows, None, cpt, d),
                                         lambda i: (first_seq + lat_tile(i) // groups, 0, lat_tile(i) % groups, 0, 0))


def _lat_tile(l_ref):
    if len(l_ref.shape) == 2:
        return l_ref[...]
    return jnp.concatenate([l_ref[:, w, :] for w in range(l_ref.shape[1])], axis=0)


def _stream_tile(c_ref, l_ref, nct):
    if nct == 0:
        return _lat_tile(l_ref)
    return jnp.where(pl.program_id(0) < nct, c_ref[...], _lat_tile(l_ref))


def _as_stream(a, rows):
    return (a, a, rows.nct)


def _normmod_mm_kernel(hc_ref, hl_ref, g_ref, mod_ref, w_ref, o_ref, *, shift_i, scale_i, nct):
    u = _norm_mod(_stream_tile(hc_ref, hl_ref, nct), g_ref[...], mod_ref[shift_i], mod_ref[scale_i])
    o_ref[...] = _dot(u.astype(BF16), w_ref[...]).astype(o_ref.dtype)


def _normmod_mm(h, g, mods_l, w, rows, shift_i, scale_i, name):
    d = w.shape[0]
    n = w.shape[1]
    return pl.pallas_call(
        functools.partial(_normmod_mm_kernel, shift_i=shift_i, scale_i=scale_i, nct=rows.nct),
        grid=(rows.n_tiles,),
        in_specs=[*_stream_specs(h, rows, d),
                  pl.BlockSpec((1, d), lambda i: (0, 0)),
                  pl.BlockSpec((None, 6, 1, d), lambda i: (rows.mod_row(i), 0, 0, 0)),
                  pl.BlockSpec((d, n), lambda i: (0, 0))],
        out_specs=pl.BlockSpec((ROW_TILE, n), lambda i: (i, 0)),
        out_shape=jax.ShapeDtypeStruct((rows.n_tiles * ROW_TILE, n), BF16),
        compiler_params=_cparams("parallel"),
        name=name,
    )(h[0], h[1], g.reshape(1, d), mods_l, w)


def _gla_kernel(q_ref, k_ref, v_ref, g_ref, r_ref, waf_ref, wab_ref, baf_ref, bab_ref, gain_ref,
                s0f_ref, s0b_ref, o_ref, sf_ref, sb_ref, gl_scr, tot_scr, qd_scr, ds_scr, o_scr, st_scr,
                *, seq_len):
    C, SL = GLA_CHUNK, GLA_SLAB
    cps = SL // C
    n_chunks, n_slabs = seq_len // C, seq_len // SL
    head_of_lane = lax.broadcasted_iota(jnp.int32, (SL, 2 * GLA_DK), 1) // GLA_DK
    row = lax.broadcasted_iota(jnp.int32, (SL, SL), 0)
    col = lax.broadcasted_iota(jnp.int32, (SL, SL), 1)
    same_chunk = (row // C) == (col // C)
    srow = lax.broadcasted_iota(jnp.int32, (2 * GLA_DV, 2 * GLA_DK), 0) // GLA_DV
    scol = lax.broadcasted_iota(jnp.int32, (2 * GLA_DV, 2 * GLA_DK), 1) // GLA_DK
    same_head = srow == scol
    r = r_ref[...].astype(F32)

    def direction(wa_ref, ba_ref, s0_ref, s_out_ref, forward):
        mask = jnp.logical_and(same_chunk, (row >= col) if forward else (row <= col))
        tri = jnp.where(mask, 1.0, 0.0).astype(BF16)
        gl_scr[...] = _log_sigmoid(_dot3(r, wa_ref[...]) + ba_ref[...]) * (1.0 / GLA_TAU)

        def slab(s, carry):
            rows = pl.ds(pl.multiple_of(s * SL, SL), SL)
            gl = gl_scr[rows, :]
            g_hi, g_lo = _split_bf16(gl)
            b = _dot(tri, g_hi) + _dot(tri, g_lo)
            b3 = b.reshape(cps, C, 2 * GLA_DK)
            last = b3[:, C - 1:C, :] if forward else b3[:, 0:1, :]
            tot = jnp.broadcast_to(last, b3.shape).reshape(SL, 2 * GLA_DK)
            tot_scr[rows, :] = tot
            q = q_ref[rows, :].astype(F32) * (GLA_DK ** -0.5)
            k = k_ref[rows, :].astype(F32)
            v = v_ref[rows, :].astype(BF16)
            qd = q * jnp.exp(b)
            kd = (k * jnp.exp(-b)).astype(BF16)
            kr = (k * jnp.exp(tot - b)).astype(BF16)
            qd_scr[rows, :] = qd.astype(BF16)
            o_parts = []
            for h in range(2):
                qh = jnp.where(head_of_lane == h, qd, 0.0).astype(BF16)
                a = jnp.where(mask, _dot_nt(qh, kd), 0.0).astype(BF16)
                o_parts.append(_dot(a, v[:, h * GLA_DV:(h + 1) * GLA_DV]))
            o = jnp.concatenate(o_parts, axis=1)
            if forward:
                o_scr[rows, :] = o
            else:
                o_scr[rows, :] += o
            for c in range(cps):
                ds = _dot_tn(v[c * C:(c + 1) * C], kr[c * C:(c + 1) * C])
                ds_scr[s * cps + c] = jnp.where(same_head, ds, 0.0)
            return carry

        lax.fori_loop(0, n_slabs, slab, 0, unroll=2)
        st_scr[...] = s0_ref[...]

        def step(i, carry):
            c = i if forward else n_chunks - 1 - i
            first = pl.multiple_of(c * C, C)
            rows = pl.ds(first, C)
            st = st_scr[...]
            o_scr[rows, :] += _dot_nt(qd_scr[rows, :], st.astype(BF16))
            st_scr[...] = st * jnp.exp(tot_scr[pl.ds(first, 1), :]) + ds_scr[c]
            return carry

        lax.fori_loop(0, n_chunks, step, 0, unroll=GLA_STEP_UNROLL)
        s_out_ref[...] = st_scr[...]

    direction(waf_ref, baf_ref, s0f_ref, sf_ref, True)
    direction(wab_ref, bab_ref, s0b_ref, sb_ref, False)
    gain = gain_ref[...]

    def readout(s, carry):
        rows = pl.ds(pl.multiple_of(s * SL, SL), SL)
        o = o_scr[rows, :]
        outs = []
        for h in range(2):
            oh = o[:, h * GLA_DV:(h + 1) * GLA_DV]
            outs.append(oh * lax.rsqrt(jnp.mean(oh * oh, axis=-1, keepdims=True) + EPS) * gain)
        o_ref[rows, :] = (jnp.concatenate(outs, axis=1) * _silu(g_ref[rows, :].astype(F32))).astype(o_ref.dtype)
        return carry

    lax.fori_loop(0, n_slabs, readout, 0)


def _gla(p, waf, wab, baf, bab, gain, s0f, s0b, batch, seq_len, blk0):
    dk2, dv2 = 2 * GLA_DK, 2 * GLA_DV
    seq = lambda width, cb: pl.BlockSpec((seq_len, width), lambda b, hp: (blk0 + b, cb(hp)))
    state_spec = pl.BlockSpec((None, None, dv2, dk2), lambda b, hp: (b, hp, 0, 0))
    in_specs = [seq(dk2, lambda hp: hp),
                seq(dk2, lambda hp: GLA_QK // dk2 + hp),
                seq(dv2, lambda hp: 2 * GLA_QK // dv2 + hp),
                seq(dv2, lambda hp: (2 * GLA_QK + GLA_V) // dv2 + hp),
                seq(R_COLS, lambda hp: (2 * GLA_QK + 2 * GLA_V + 3 * HY_WIDTH) // R_COLS),
                pl.BlockSpec((R_COLS, dk2), lambda b, hp: (0, hp)),
                pl.BlockSpec((R_COLS, dk2), lambda b, hp: (0, hp)),
                pl.BlockSpec((1, dk2), lambda b, hp: (0, hp)),
                pl.BlockSpec((1, dk2), lambda b, hp: (0, hp)),
                pl.BlockSpec((1, GLA_DV), lambda b, hp: (0, 0)),
                state_spec, state_spec]
    state_shape = jax.ShapeDtypeStruct((batch, 2, dv2, dk2), F32)
    return pl.pallas_call(
        functools.partial(_gla_kernel, seq_len=seq_len),
        grid=(batch, 2),
        in_specs=in_specs,
        out_specs=[pl.BlockSpec((seq_len, dv2), lambda b, hp: (b, hp)), state_spec, state_spec],
        out_shape=[jax.ShapeDtypeStruct((batch * seq_len, GLA_V), BF16), state_shape, state_shape],
        scratch_shapes=[pltpu.VMEM((seq_len, dk2), F32),
                        pltpu.VMEM((seq_len, dk2), F32),
                        pltpu.VMEM((seq_len, dk2), BF16),
                        pltpu.VMEM((seq_len // GLA_CHUNK, dv2, dk2), F32),
                        pltpu.VMEM((seq_len, dv2), F32),
                        pltpu.VMEM((dv2, dk2), F32)],
        compiler_params=_cparams("parallel", "parallel"),
        name="gla",
    )(p, p, p, p, p, waf, wab, baf, bab, gain, s0f, s0b)


def _conv3(m, prev_row, next_row, w_ref):
    n = m.shape[0]
    ridx = lax.broadcasted_iota(jnp.int32, m.shape, 0)
    m_prev = jnp.where(ridx == 0, prev_row, pltpu.roll(m, 1, 0))
    m_next = jnp.where(ridx == n - 1, next_row, pltpu.roll(m, n - 1, 0))
    return w_ref[0:1, :] * m_prev + w_ref[1:2, :] * m + w_ref[2:3, :] * m_next


HALO = 16


def _halo_specs(width, col_block, t_rows):
    g = ROW_TILE // HALO
    last = t_rows // HALO - 1
    prev = pl.BlockSpec((HALO, width), lambda i: (jnp.maximum(i * g - 1, 0), col_block))
    nxt = pl.BlockSpec((HALO, width), lambda i: (jnp.minimum((i + 1) * g, last), col_block))
    return prev, nxt


def _hyena_pre_kernel(x0_ref, x1_ref, v_ref, x0p_ref, x0n_ref, x1p_ref, x1n_ref, vp_ref, vn_ref,
                      w_ref, b_ref, x0c_ref, z_ref, *, rows):
    first, last = rows.seq_edges(pl.program_id(0))
    keep_p = jnp.where(first, 0.0, 1.0)
    keep_n = jnp.where(last, 0.0, 1.0)
    hw = HY_WIDTH

    def conv(ref, p_ref, n_ref, j):
        w = w_ref.at[:, j * hw:(j + 1) * hw]
        y = _conv3(ref[...].astype(F32), p_ref[HALO - 1:HALO, :].astype(F32) * keep_p,
                   n_ref[0:1, :].astype(F32) * keep_n, w)
        return y + b_ref[:, j * hw:(j + 1) * hw]

    x0c_ref[...] = conv(x0_ref, x0p_ref, x0n_ref, 0)
    z_ref[...] = conv(v_ref, vp_ref, vn_ref, 2) * conv(x1_ref, x1p_ref, x1n_ref, 1)


def _hyena_pre(p, conv_w, conv_b, rows):
    t_rows = p.shape[0]
    hw = HY_WIDTH
    cb0 = (2 * GLA_QK + 2 * GLA_V) // hw
    in_specs = [pl.BlockSpec((ROW_TILE, hw), lambda i, j=j: (i, cb0 + j)) for j in range(3)]
    for j in range(3):
        in_specs.extend(_halo_specs(hw, cb0 + j, t_rows))
    in_specs += [pl.BlockSpec((3, 3 * hw), lambda i: (0, 0)), pl.BlockSpec((1, 3 * hw), lambda i: (0, 0))]
    out_spec = pl.BlockSpec((ROW_TILE, hw), lambda i: (i, 0))
    shape = jax.ShapeDtypeStruct((rows.n_tiles * ROW_TILE, hw), F32)
    return pl.pallas_call(
        functools.partial(_hyena_pre_kernel, rows=rows),
        grid=(rows.n_tiles,),
        in_specs=in_specs,
        out_specs=[out_spec, out_spec],
        out_shape=[shape, shape],
        compiler_params=_cparams("parallel"),
        name="hyena_pre",
    )(p, p, p, p, p, p, p, p, p, conv_w, conv_b.reshape(1, 3 * hw))


def _filter_kernel(z_ref, w1_ref, b1_ref, f1_ref, w2_ref, b2_ref, f2_ref, w3_ref, win_ref, o_ref):
    hh = jnp.sin(f1_ref[...] * (_dot3(z_ref[...], w1_ref[...]) + b1_ref[...]))
    hh = jnp.sin(f2_ref[...] * (_dot3(hh, w2_ref[...]) + b2_ref[...]))
    win = win_ref[...]
    o_ref[...] = _dot3(hh, w3_ref[...]) * jnp.concatenate([win, win], axis=1)


@functools.lru_cache(maxsize=None)
def _filter_features(L):
    t = np.linspace(0.0, 1.0, L, dtype=np.float32)[:, None]
    pos = np.arange(L, dtype=np.float32)[:, None]
    bands = np.linspace(1e-4, HY_BANDS - 1, HY_BANDS, dtype=np.float32)[None]
    ang = (np.float32(2.0 * math.pi / L) * pos * bands).astype(np.float32)
    z = np.concatenate([t, np.cos(ang), np.sin(ang)], axis=-1).astype(np.float32)
    z = np.pad(z, ((0, 0), (0, LANE - HY_EMB)))
    max_decay = math.log(HY_TARGET) / HY_FAST_DECAY
    min_decay = math.log(HY_TARGET) / HY_SLOW_DECAY
    deltas = np.linspace(min_decay, max_decay, HY_WIDTH, dtype=np.float32)
    window = np.exp(-t * np.abs(deltas)[None]).astype(np.float32)
    return z, window


def _hyena_filters(L, w1, b1, f1, w2, b2, f2, w3):
    z, window = _filter_features(L)
    w1p = jnp.pad(w1, ((0, LANE - HY_EMB), (0, 0)))
    tl = min(L, 512)
    full = lambda a: pl.BlockSpec(a.shape, lambda i: (0,) * a.ndim)
    row = lambda a: a.reshape(1, -1)
    ops = [w1p, row(b1), row(f1), w2, row(b2), row(f2), w3]
    return pl.pallas_call(
        _filter_kernel,
        grid=(L // tl,),
        in_specs=[pl.BlockSpec((tl, LANE), lambda i: (i, 0))] + [full(a) for a in ops]
                 + [pl.BlockSpec((tl, HY_WIDTH), lambda i: (i, 0))],
        out_specs=pl.BlockSpec((tl, 2 * HY_WIDTH), lambda i: (i, 0)),
        out_shape=jax.ShapeDtypeStruct((L, 2 * HY_WIDTH), F32),
        compiler_params=_cparams("parallel"),
        name="hyena_filters",
    )(jnp.asarray(z), *ops, jnp.asarray(window))


def _freq_tile(L):
    return min(2 * L, 1024)


@functools.lru_cache(maxsize=None)
def _dft_tables(L):
    n = 2 * L
    tf = _freq_tile(L)
    half = tf // 2
    t = np.arange(L, dtype=np.int64)[None, :]
    fm = np.zeros((n, L), np.float64)
    scale = np.zeros((n, 1), np.float64)
    sign = np.zeros((n, 1), np.float64)
    for j in range(n // tf):
        k = (np.arange(half, dtype=np.int64) + j * half)[:, None]
        ang = 2.0 * np.pi * ((k * t) % n).astype(np.float64) / n
        fm[j * tf:j * tf + half] = np.cos(ang)
        fm[j * tf + half:(j + 1) * tf] = -np.sin(ang)
        scale[j * tf:(j + 1) * tf] = 2.0 / n
        sign[j * tf:j * tf + half] = 1.0
        sign[j * tf + half:(j + 1) * tf] = -1.0
    fm[half] = np.cos(np.pi * t[0])
    scale[0] = 1.0 / n
    scale[half] = 1.0 / n
    sign[half] = 1.0
    return (fm.astype(np.float32), np.ascontiguousarray(fm.T).astype(np.float32),
            scale.astype(np.float32), sign.astype(np.float32))


def _ktab_kernel(f_ref, h_ref, scale_ref, sign_ref, o_ref):
    hw = HY_WIDTH
    hh, hl = _split_bf16(h_ref[...])
    f = f_ref[...]
    kk = _dot(f, hh) + _dot(f, hl)
    o_ref[...] = scale_ref[...] * (kk[:, :hw] + sign_ref[...] * kk[:, hw:])


def _hyena_ktab(filt, L):
    fm, _, scale, sign = _dft_tables(L)
    n = 2 * L
    tf = _freq_tile(L)
    return pl.pallas_call(
        _ktab_kernel,
        grid=(n // tf,),
        in_specs=[pl.BlockSpec((tf, L), lambda j: (j, 0)),
                  pl.BlockSpec((L, 2 * HY_WIDTH), lambda j: (0, 0)),
                  pl.BlockSpec((tf, 1), lambda j: (j, 0)),
                  pl.BlockSpec((tf, 1), lambda j: (j, 0))],
        out_specs=pl.BlockSpec((tf, HY_WIDTH), lambda j: (j, 0)),
        out_shape=jax.ShapeDtypeStruct((n, HY_WIDTH), F32),
        compiler_params=_cparams("parallel"),
        name="hyena_ktab",
    )(jnp.asarray(fm, dtype=BF16), filt, jnp.asarray(scale), jnp.asarray(sign))


def _hyena_conv_kernel(z_ref, x0_ref, f_ref, ft_ref, k_ref, skip_ref, o_ref, zb_scr, acc_scr, *, tf):
    j = pl.program_id(1)
    half = tf // 2

    @pl.when(j == 0)
    def _():
        zb_scr[...] = z_ref[...].astype(BF16)
        acc_scr[...] = jnp.zeros_like(acc_scr)

    zf = _dot(f_ref[...], zb_scr[...])
    re, im = zf[:half], zf[half:]
    kre, kim = k_ref[:half, :], k_ref[half:, :]
    ridx = lax.broadcasted_iota(jnp.int32, re.shape, 0)
    mix = jnp.where(jnp.logical_and(j == 0, ridx == 0), 0.0, 1.0)
    yre = re * kre - mix * (im * kim)
    yim = mix * (re * kim) + im * jnp.where(mix == 0.0, kim, kre)
    y = jnp.concatenate([yre, yim], axis=0).astype(BF16)
    acc_scr[...] += _dot(ft_ref[...], y)

    @pl.when(j == pl.num_programs(1) - 1)
    def _():
        o_ref[...] = (x0_ref[...] * (acc_scr[...] + z_ref[...] * skip_ref[...])).astype(o_ref.dtype)


def _hyena_conv(z, x0c, ktab, skip, batch, L, blk0):
    fm, fmt, _, _ = _dft_tables(L)
    n = 2 * L
    tf = _freq_tile(L)
    hw = HY_WIDTH
    return pl.pallas_call(
        functools.partial(_hyena_conv_kernel, tf=tf),
        grid=(batch, n // tf),
        in_specs=[pl.BlockSpec((L, hw), lambda b, j: (blk0 + b, 0)),
                  pl.BlockSpec((L, hw), lambda b, j: (blk0 + b, 0)),
                  pl.BlockSpec((tf, L), lambda b, j: (j, 0)),
                  pl.BlockSpec((L, tf), lambda b, j: (0, j)),
                  pl.BlockSpec((tf, hw), lambda b, j: (j, 0)),
                  pl.BlockSpec((1, hw), lambda b, j: (0, 0))],
        out_specs=pl.BlockSpec((L, hw), lambda b, j: (b, 0)),
        out_shape=jax.ShapeDtypeStruct((batch * L, hw), BF16),
        scratch_shapes=[pltpu.VMEM((L, hw), BF16), pltpu.VMEM((L, hw), F32)],
        compiler_params=_cparams("parallel", "arbitrary"),
        name="hyena_conv",
    )(z, x0c, jnp.asarray(fm, dtype=BF16), jnp.asarray(fmt, dtype=BF16), ktab, skip.reshape(1, hw))


def _mix_out_kernel(ac_ref, al_ref, bc_ref, bl_ref, wa_ref, wb_ref, hc_ref, hl_ref, mod_ref, o_ref, *, nct):
    a = _stream_tile(ac_ref, al_ref, nct)
    b = _stream_tile(bc_ref, bl_ref, nct)
    y = _dot(a, wa_ref[...]) + _dot(b, wb_ref[...])
    o_ref[...] = _stream_tile(hc_ref, hl_ref, nct) + mod_ref[2] * y


def _mix_out(a_ctx, a_lat, b_ctx, b_lat, w_out, h, mods_l, rows):
    d = w_out.shape[1]
    ka, kb = a_ctx.shape[1], b_ctx.shape[1]
    return pl.pallas_call(
        functools.partial(_mix_out_kernel, nct=rows.nct),
        grid=(rows.n_tiles,),
        in_specs=[*_stream_specs((a_ctx, a_lat, 0), rows, ka), *_stream_specs((b_ctx, b_lat, 0), rows, kb),
                  pl.BlockSpec((ka, d), lambda i: (0, 0)),
                  pl.BlockSpec((kb, d), lambda i: (ka // kb, 0)),
                  *_stream_specs(h, rows, d),
                  pl.BlockSpec((None, 6, 1, d), lambda i: (rows.mod_row(i), 0, 0, 0))],
        out_specs=pl.BlockSpec((ROW_TILE, d), lambda i: (i, 0)),
        out_shape=jax.ShapeDtypeStruct((rows.n_tiles * ROW_TILE, d), F32),
        compiler_params=_cparams("parallel"),
        name="mix_out",
    )(a_ctx, a_lat, b_ctx, b_lat, w_out, w_out, h[0], h[1], mods_l)


def _shortconv_out_kernel(bg_ref, cg_ref, xi_ref, cgp_ref, cgn_ref, xip_ref, xin_ref, cw_ref, w_ref,
                          h_ref, mod_ref, o_ref, *, rows):
    first, last = rows.seq_edges(pl.program_id(0))
    keep_p = jnp.where(first, 0.0, 1.0)
    keep_n = jnp.where(last, 0.0, 1.0)
    f32 = lambda a: a.astype(F32)
    m = f32(cg_ref[...]) * f32(xi_ref[...])
    m_prev = f32(cgp_ref[HALO - 1:HALO, :]) * f32(xip_ref[HALO - 1:HALO, :]) * keep_p
    m_next = f32(cgn_ref[0:1, :]) * f32(xin_ref[0:1, :]) * keep_n
    y = f32(bg_ref[...]) * _conv3(m, m_prev, m_next, cw_ref)
    o_ref[...] = h_ref[...] + mod_ref[2] * _dot(y.astype(BF16), w_ref[...])


def _shortconv_out(p, conv_w, w_out, h, mods_l, rows):
    d = h.shape[1]
    t_rows = p.shape[0]
    in_specs = [pl.BlockSpec((ROW_TILE, d), lambda i, j=j: (i, j)) for j in range(3)]
    in_specs += [*_halo_specs(d, 1, t_rows), *_halo_specs(d, 2, t_rows),
                 pl.BlockSpec((3, d), lambda i: (0, 0)),
                 pl.BlockSpec((d, d), lambda i: (0, 0)),
                 pl.BlockSpec((ROW_TILE, d), lambda i: (i, 0)),
                 pl.BlockSpec((None, 6, 1, d), lambda i: (rows.mod_row(i), 0, 0, 0))]
    return pl.pallas_call(
        functools.partial(_shortconv_out_kernel, rows=rows),
        grid=(rows.n_tiles,),
        in_specs=in_specs,
        out_specs=pl.BlockSpec((ROW_TILE, d), lambda i: (i, 0)),
        out_shape=jax.ShapeDtypeStruct((rows.n_tiles * ROW_TILE, d), F32),
        compiler_params=_cparams("parallel"),
        name="shortconv_out",
    )(p, p, p, p, p, p, p, conv_w, w_out, h, mods_l)


META_E, META_RANK, META_GATE = 0, 2, 4


def _lane_min_index(mask, lane_f):
    return jnp.min(jnp.where(mask, lane_f, float(LANE)), axis=1, keepdims=True)


def _router_kernel(h_ref, g_ref, mod_ref, w_ref, b_ref, v_ref, meta_ref, cnt_ref, carry_scr):
    @pl.when(pl.program_id(0) == 0)
    def _():
        carry_scr[...] = jnp.zeros_like(carry_scr)

    v = _norm_mod(h_ref[...], g_ref[...], mod_ref[3], mod_ref[4])
    v_ref[...] = v
    lg = _dot3(v, w_ref[...]) + b_ref[...]
    tm = lg.shape[0]
    lane = lax.broadcasted_iota(jnp.int32, lg.shape, 1)
    lane_f = lane.astype(F32)
    neg = -jnp.inf

    is_grp = lane < N_GROUPS
    lgm = jnp.where(is_grp, lg, neg)
    m_g = jnp.max(lgm, axis=1, keepdims=True)
    s_g = jnp.sum(jnp.where(is_grp, jnp.exp(lg - m_g), 0.0), axis=1, keepdims=True)
    p_g = 1.0 / s_g
    grp = _lane_min_index(lgm == m_g, lane_f)

    ex_lane = lane - N_GROUPS
    in_grp = jnp.logical_and(jnp.logical_and(ex_lane >= 0, ex_lane < N_EXPERTS),
                             (ex_lane // EXP_PER_GROUP).astype(F32) == grp)
    m_e = jnp.max(jnp.where(in_grp, lg, neg), axis=1, keepdims=True)
    ee = jnp.where(in_grp, jnp.exp(lg - m_e), 0.0)
    pe = ee / jnp.sum(ee, axis=1, keepdims=True)
    pe1 = jnp.where(in_grp, pe, -1.0)
    p1 = jnp.max(pe1, axis=1, keepdims=True)
    i1 = _lane_min_index(pe1 == p1, lane_f)
    pe2 = jnp.where(lane_f == i1, -1.0, pe1)
    p2 = jnp.max(pe2, axis=1, keepdims=True)
    i2 = _lane_min_index(pe2 == p2, lane_f)
    denom = p1 + p2
    g1 = p_g * p1 / denom
    g2 = p_g * p2 / denom
    e1 = i1 - float(N_GROUPS)
    e2 = i2 - float(N_GROUPS)

    oh1 = lane_f == e1
    oh2 = lane_f == e2
    row = lax.broadcasted_iota(jnp.int32, (tm, tm), 0)
    col = lax.broadcasted_iota(jnp.int32, (tm, tm), 1)
    earlier = jnp.where(row > col, 1.0, 0.0).astype(BF16)
    c1 = _dot(earlier, jnp.where(oh1, 1.0, 0.0).astype(BF16))
    c2 = _dot(earlier, jnp.where(oh2, 1.0, 0.0).astype(BF16))
    tot1 = jnp.sum(jnp.where(oh1, 1.0, 0.0), axis=0, keepdims=True)
    tot2 = jnp.sum(jnp.where(oh2, 1.0, 0.0), axis=0, keepdims=True)
    carry = carry_scr[...]
    r1 = jnp.sum(jnp.where(oh1, carry + c1, 0.0), axis=1, keepdims=True)
    r2 = jnp.sum(jnp.where(oh2, carry + tot1 + c2, 0.0), axis=1, keepdims=True)
    carry = carry + tot1 + tot2
    carry_scr[...] = carry
    cnt_ref[...] = carry

    meta = jnp.zeros_like(lg)
    for idx, val in ((META_E, e1), (META_E + 1, e2), (META_RANK, r1), (META_RANK + 1, r2),
                     (META_GATE, g1), (META_GATE + 1, g2)):
        meta = jnp.where(lane == idx, val, meta)
    meta_ref[...] = meta


def _router(h, g, mods_l, w_rt, b_rt, rows):
    d = h.shape[1]
    tm = rows.tile
    nt = rows.n_tiles * tm
    return pl.pallas_call(
        _router_kernel,
        grid=(rows.n_tiles,),
        in_specs=[pl.BlockSpec((tm, d), lambda i: (i, 0)),
                  pl.BlockSpec((1, d), lambda i: (0, 0)),
                  pl.BlockSpec((None, 6, 1, d), lambda i: (rows.mod_row(i), 0, 0, 0)),
                  pl.BlockSpec((d, LANE), lambda i: (0, 0)),
                  pl.BlockSpec((1, LANE), lambda i: (0, 0))],
        out_specs=[pl.BlockSpec((tm, d), lambda i: (i, 0)),
                   pl.BlockSpec((tm, LANE), lambda i: (i, 0)),
                   pl.BlockSpec((1, LANE), lambda i: (0, 0))],
        out_shape=[jax.ShapeDtypeStruct((nt, d), F32), jax.ShapeDtypeStruct((nt, LANE), F32),
                   jax.ShapeDtypeStruct((1, LANE), F32)],
        scratch_shapes=[pltpu.VMEM((1, LANE), F32)],
        compiler_params=_cparams("arbitrary"),
        name="router",
    )(h, g.reshape(1, d), mods_l, w_rt, b_rt)


def _dispatch_plan(meta, counts):
    bm = MOE_ROWS
    t = meta.shape[0]
    n_blocks = -(-(t * TOP_K + N_EXPERTS * (bm - 1)) // bm)
    counts = counts[0, :N_EXPERTS].astype(jnp.int32)
    nblk = (counts + bm - 1) // bm
    blk_end = jnp.cumsum(nblk).astype(jnp.int32)
    slot0 = (blk_end - nblk) * bm
    expert = meta[:, META_E:META_E + TOP_K].astype(jnp.int32)
    rank = meta[:, META_RANK:META_RANK + TOP_K].astype(jnp.int32)
    onehot = expert[:, :, None] == jnp.arange(N_EXPERTS, dtype=jnp.int32)
    slot = rank + jnp.sum(jnp.where(onehot, slot0, 0), axis=-1)
    pad_lo = jnp.concatenate([slot0 + counts, blk_end[-1:] * bm])
    pad_hi = jnp.concatenate([blk_end * bm, jnp.full((1,), n_blocks * bm, jnp.int32)])
    src = _slot_tokens(jnp.concatenate([pad_lo, pad_hi]).astype(jnp.int32), _slot_tiles(slot), n_blocks)
    blk = jnp.arange(n_blocks, dtype=jnp.int32)
    blk_e = jnp.minimum(jnp.sum(blk[:, None] >= blk_end[None, :], axis=1), N_EXPERTS - 1).astype(jnp.int32)
    blk_used = (blk < blk_end[-1]).astype(jnp.int32)
    return slot, src.reshape(n_blocks, 1, bm), blk_e, blk_used


def _slot_tiles(slot):
    return slot.reshape(slot.shape[0] // ROW_TILE, ROW_TILE, TOP_K).transpose(0, 2, 1)


def _slot_tokens_kernel(pad_ref, slot_ref, src_ref):
    i = pl.program_id(0)
    tm = slot_ref.shape[1]
    n_ranges = pad_ref.shape[0] // 2

    @pl.when(i == 0)
    def _():
        def clear(j, c):
            src_ref[j] = 0
            return c

        for q in range(n_ranges):
            lax.fori_loop(pad_ref[q], pad_ref[n_ranges + q], clear, 0)

    base = i * tm
    for r in range(tm):
        for k in range(TOP_K):
            src_ref[slot_ref[k, r]] = base + r


def _slot_tokens(pad_ranges, slot_tiles, n_blocks):
    n = n_blocks * MOE_ROWS
    grid_spec = pltpu.PrefetchScalarGridSpec(
        num_scalar_prefetch=1,
        grid=(slot_tiles.shape[0],),
        in_specs=[pl.BlockSpec((None, TOP_K, ROW_TILE), lambda i, pad: (i, 0, 0), memory_space=pltpu.SMEM)],
        out_specs=pl.BlockSpec((n,), lambda i, pad: (0,), memory_space=pltpu.SMEM),
    )
    return pl.pallas_call(
        _slot_tokens_kernel,
        grid_spec=grid_spec,
        out_shape=jax.ShapeDtypeStruct((n,), jnp.int32),
        compiler_params=_cparams("arbitrary"),
        name="slot_tokens",
    )(pad_ranges, slot_tiles)


def _row_copies_start(src_hbm, idx_ref, k, dst, sem):
    for r in range(dst.shape[0]):
        pltpu.async_copy(src_hbm.at[pl.ds(idx_ref[k, r], 1), :], dst.at[pl.ds(r, 1), :], sem,
                         priority=r % DMA_THREADS)


def _row_copies_wait(src_hbm, dst, sem):
    pltpu.make_async_copy(src_hbm.at[pl.ds(0, dst.shape[0]), :], dst, sem).wait()


def _expert_kernel(blk_e_ref, blk_used_ref, src_ref, src_next_ref, v_hbm, w1_ref, w3_ref, w2_ref, y_ref,
                   xbuf, w1_scr, w3_scr, w2_scr, sem):
    j = pl.program_id(0)
    nb = pl.num_programs(0)
    used = blk_used_ref[j] > 0
    new_expert = jnp.logical_or(j == 0, blk_e_ref[j] != blk_e_ref[jnp.maximum(j - 1, 0)])

    @pl.when(j == 0)
    def _():
        _row_copies_start(v_hbm, src_ref, 0, xbuf.at[0], sem.at[0])

    @pl.when(jnp.logical_and(used, new_expert))
    def _():
        w1_scr[...] = w1_ref[...].astype(BF16)
        w3_scr[...] = w3_ref[...].astype(BF16)
        w2_scr[...] = w2_ref[...].astype(BF16)

    @pl.when(used)
    def _():
        b = j % 2
        _row_copies_wait(v_hbm, xbuf.at[b], sem.at[b])
        x = xbuf[b].astype(BF16)
        _row_copies_start(v_hbm, src_next_ref, 0, xbuf.at[1 - b], sem.at[1 - b])
        hid = _silu(_dot(x, w1_scr[...])) * _dot(x, w3_scr[...])
        y_ref[...] = _dot(hid.astype(BF16), w2_scr[...])

        @pl.when(jnp.logical_or(j == nb - 1, blk_used_ref[jnp.minimum(j + 1, nb - 1)] == 0))
        def _():
            _row_copies_wait(v_hbm, xbuf.at[1 - b], sem.at[1 - b])

    @pl.when(jnp.logical_not(used))
    def _():
        y_ref[...] = jnp.zeros_like(y_ref)


def _experts(v, src, blk_e, blk_used, w1, w3, w2, layer):
    t, d = v.shape
    n_blocks = src.shape[0]
    bm = MOE_ROWS
    de = w1.shape[3]

    def next_block(j, e, used):
        nxt = jnp.minimum(j + 1, n_blocks - 1)
        return jnp.where(used[nxt] > 0, nxt, j)

    grid_spec = pltpu.PrefetchScalarGridSpec(
        num_scalar_prefetch=2,
        grid=(n_blocks,),
        in_specs=[pl.BlockSpec((None, 1, bm), lambda j, e, u: (j, 0, 0), memory_space=pltpu.SMEM),
                  pl.BlockSpec((None, 1, bm), lambda j, e, u: (next_block(j, e, u), 0, 0), memory_space=pltpu.SMEM),
                  pl.BlockSpec(memory_space=pl.ANY),
                  pl.BlockSpec((None, None, d, de), lambda j, e, u: (layer, e[j], 0, 0)),
                  pl.BlockSpec((None, None, d, de), lambda j, e, u: (layer, e[j], 0, 0)),
                  pl.BlockSpec((None, None, de, d), lambda j, e, u: (layer, e[j], 0, 0))],
        out_specs=pl.BlockSpec((bm, d), lambda j, e, u: (j, 0)),
        scratch_shapes=[pltpu.VMEM((2, bm, d), F32), pltpu.VMEM((d, de), BF16), pltpu.VMEM((d, de), BF16),
                        pltpu.VMEM((de, d), BF16), pltpu.SemaphoreType.DMA((2,))],
    )
    return pl.pallas_call(
        _expert_kernel,
        grid_spec=grid_spec,
        out_shape=jax.ShapeDtypeStruct((n_blocks * bm, d), F32),
        compiler_params=_cparams("arbitrary"),
        name="experts",
    )(blk_e, blk_used, src, src, v, w1, w3, w2)


def _combined_tile(slot_ref, slot_next_ref, hc_ref, hl_ref, nct, meta_ref, modp_ref, y_hbm, buf, sem, overlap_with):
    i = pl.program_id(0)
    n = pl.num_programs(0)

    @pl.when(i == 0)
    def _():
        for k in range(TOP_K):
            _row_copies_start(y_hbm, slot_ref, k, buf.at[0, k], sem.at[0])

    b = i % 2
    for k in range(TOP_K):
        _row_copies_wait(y_hbm, buf.at[b, k], sem.at[b])
    meta = meta_ref[...]
    f = meta[:, META_GATE:META_GATE + 1] * buf[b, 0] + meta[:, META_GATE + 1:META_GATE + 2] * buf[b, 1]
    hn = _stream_tile(hc_ref, hl_ref, nct) + modp_ref[5] * f
    for k in range(TOP_K):
        _row_copies_start(y_hbm, slot_next_ref, k, buf.at[1 - b, k], sem.at[1 - b])
    overlap_with(hn)

    @pl.when(i == n - 1)
    def _():
        for k in range(TOP_K):
            _row_copies_wait(y_hbm, buf.at[1 - b, k], sem.at[1 - b])


def _combine_mm_kernel(slot_ref, slot_next_ref, hc_ref, hl_ref, meta_ref, modp_ref, g_ref, mod_ref, w_ref, y_hbm,
                       hn_ref, o_ref, buf, sem, *, nct):
    def project(hn):
        hn_ref[...] = hn
        u = _norm_mod(hn, g_ref[...], mod_ref[0], mod_ref[1])
        o_ref[...] = _dot(u.astype(BF16), w_ref[...]).astype(o_ref.dtype)

    _combined_tile(slot_ref, slot_next_ref, hc_ref, hl_ref, nct, meta_ref, modp_ref, y_hbm, buf, sem, project)


def _combine_norm_kernel(slot_ref, slot_next_ref, hc_ref, hl_ref, meta_ref, modp_ref, g_ref, y_hbm, o_ref, buf, sem,
                         *, nct):
    def finish(hn):
        res = hn * lax.rsqrt(jnp.mean(hn * hn, axis=-1, keepdims=True) + EPS) * g_ref[...]
        if len(o_ref.shape) == 2:
            o_ref[...] = res
        else:
            n_r = o_ref.shape[0]
            for w in range(o_ref.shape[1]):
                o_ref[:, w, :] = res[w * n_r:(w + 1) * n_r, :]

    _combined_tile(slot_ref, slot_next_ref, hc_ref, hl_ref, nct, meta_ref, modp_ref, y_hbm, buf, sem, finish)


def _combine_then(h, ys, slot_tiles, meta, mods_prev, rows, g, mods_l=None, w=None, name="combine_norm", tile0=0,
                  grid_seq=None):
    d = g.shape[0]
    n_tiles = rows.n_tiles
    slot_spec = lambda fn: pl.BlockSpec((None, TOP_K, ROW_TILE), lambda i: (tile0 + fn(i), 0, 0),
                                        memory_space=pltpu.SMEM)
    mod_spec = pl.BlockSpec((None, 6, 1, d), lambda i: (rows.mod_row(i), 0, 0, 0))
    tile_in = lambda width: pl.BlockSpec((ROW_TILE, width), lambda i: (tile0 + i, 0))
    tile = lambda width: pl.BlockSpec((ROW_TILE, width), lambda i: (i, 0))
    rows_shape = lambda width: jax.ShapeDtypeStruct((n_tiles * ROW_TILE, width), F32)
    h_specs = _stream_specs(h, rows, d)
    h_lat = h[1]
    if grid_seq is not None and w is not None:
        nct = rows.nct
        h_specs[1] = _grid_spec(grid_seq, d, h[2] * ROW_TILE // grid_seq)(lambda i: jnp.maximum(i - nct, 0))
        h_lat = _grid_view(h_lat, grid_seq)
    in_specs = [slot_spec(lambda i: i), slot_spec(lambda i: jnp.minimum(i + 1, n_tiles - 1)),
                *h_specs, tile_in(LANE), mod_spec, pl.BlockSpec((1, d), lambda i: (0, 0))]
    args = [slot_tiles, slot_tiles, h[0], h_lat, meta, mods_prev, g.reshape(1, d)]
    if w is None and grid_seq is not None:
        assert rows.nct == 0
        kern, out_specs = _combine_norm_kernel, _grid_spec(grid_seq, d, 0)(lambda i: i)
        out_shape = jax.eval_shape(lambda: _grid_view(jnp.zeros((n_tiles * ROW_TILE, d), F32), grid_seq))
    elif w is None:
        kern, out_specs, out_shape = _combine_norm_kernel, tile(d), rows_shape(d)
    else:
        nw = w.shape[1]
        in_specs += [mod_spec, pl.BlockSpec((d, nw), lambda i: (0, 0))]
        args += [mods_l, w]
        kern, out_specs = _combine_mm_kernel, [tile(d), tile(nw)]
        out_shape = [rows_shape(d), jax.ShapeDtypeStruct((n_tiles * ROW_TILE, nw), BF16)]
    return pl.pallas_call(
        functools.partial(kern, nct=rows.nct),
        grid=(n_tiles,),
        in_specs=in_specs + [pl.BlockSpec(memory_space=pl.ANY)],
        out_specs=out_specs,
        out_shape=out_shape,
        scratch_shapes=[pltpu.VMEM((2, TOP_K, ROW_TILE, d), F32), pltpu.SemaphoreType.DMA((2,))],
        compiler_params=_cparams("arbitrary"),
        name=name,
    )(*args, ys)


def _even_w_in(w):
    d = w.shape[0]
    n_main = 2 * GLA_QK + 2 * GLA_V
    ranks = w[:, n_main:n_main + 2 * GLA_RANK]
    hy = w[:, n_main + 2 * GLA_RANK:]
    pad = jnp.zeros((d, R_COLS - 2 * GLA_RANK), w.dtype)
    return jnp.concatenate([w[:, :n_main], hy, ranks, pad], axis=1).astype(BF16)


def _rank_proj(wa, first_row):
    return jnp.zeros((R_COLS, GLA_QK), F32).at[first_row:first_row + GLA_RANK].set(wa)


def _grid_transpose(h_lat, batch, a, b):
    d = h_lat.shape[1]
    return h_lat.reshape(batch, a, b, d).transpose(0, 2, 1, 3).reshape(-1, d)


def kernel(x, c, ctx, c_ctx, mod_w, mod_b, norm_mix, norm_ffn, norm_final, ev_w_in, ev_w_out, gla_wa_f, gla_ba_f, gla_wa_b, gla_ba_b, gla_norm, hy_conv_w, hy_conv_b, hy_w1, hy_b1, hy_f1, hy_w2, hy_b2, hy_f2, hy_w3, hy_skip, od_w_in, od_conv_w, od_w_out, rt_w_grp, rt_b_grp, rt_w_exp, rt_b_exp, ex_w1, ex_w3, ex_w2):
    batch, s, d = x.shape
    lc = ctx.shape[1]
    depth = mod_w.shape[0]
    tc, tl = batch * lc, batch * s
    assert lc % ROW_TILE == 0 and s % ROW_TILE == 0 and tc % s == 0 and s % GRID_W == 0
    assert lc % GLA_CHUNK == 0 and s % GLA_CHUNK == 0
    assert s % ROUTER_TILE == 0 and tc % ROUTER_TILE == 0
    assert depth % 2 == 0
    grid_rows = s // GRID_W
    assert ROW_TILE % grid_rows == 0 and GRID_W % (ROW_TILE // grid_rows) == 0 and (ROW_TILE // grid_rows) % 8 == 0

    n_cond = -(-(batch + 1) // 8) * 8
    cond = jnp.concatenate([c, c_ctx[None], jnp.zeros((n_cond - batch - 1, d), F32)], axis=0)
    mods = _mods(cond, mod_w, mod_b).reshape(depth, n_cond, 6, 1, d)

    hs = (ctx.reshape(tc, d), x.reshape(tl, d), 0)
    col_major_now = False
    moe = None

    def lat_part(a, fn):
        return fn(a) if a.shape[0] == tl else jnp.concatenate([a[:tc], fn(a[tc:])], axis=0)

    for l in range(depth):
        i = l // 2
        even = l % 2 == 0
        ctx_out = l < depth - 1
        col_major = i % 2 == 1
        routing = [] if moe is None else [moe[1], moe[2]]
        if col_major != col_major_now:
            perm = ((lambda a: _grid_transpose(a, batch, grid_rows, GRID_W)) if col_major
                    else (lambda a: _grid_transpose(a, batch, GRID_W, grid_rows)))
            grid_read = moe is not None and col_major
            if not grid_read:
                lat = hs[1][hs[2] * ROW_TILE:hs[2] * ROW_TILE + tl]
                hs = (hs[0], perm(lat), 0)
            routing = [lat_part(a, perm) for a in routing]
            col_major_now = col_major
        else:
            grid_read = False
        arr_rows = _Rows(batch, lc, s, not ctx_out)
        if not ctx_out:
            hs = (hs[1], hs[1], hs[2])
        mods_l = mods[l]
        w_in = _even_w_in(ev_w_in[i]) if even else od_w_in[i].astype(BF16)
        name = "even_in" if even else "odd_in"
        if moe is None:
            p = _normmod_mm(hs, norm_mix[l], mods_l, w_in, arr_rows, 0, 1, name)
        else:
            tile0 = routing[0].shape[0] // ROW_TILE - arr_rows.n_tiles
            h, p = _combine_then(hs, moe[0], _slot_tiles(routing[0]), routing[1], moe[3], arr_rows,
                                 norm_mix[l], mods_l, w_in, name, tile0, s if grid_read else None)
            hs = _as_stream(h, arr_rows)

        if even:
            waf = _rank_proj(gla_wa_f[i], 0)
            wab = _rank_proj(gla_wa_b[i], GLA_RANK)
            baf, bab = gla_ba_f[i].reshape(1, -1), gla_ba_b[i].reshape(1, -1)
            gain = gla_norm[i].reshape(1, -1)
            zeros = jnp.zeros((batch, 2, 2 * GLA_DV, 2 * GLA_DK), F32)
            filt_args = (hy_w1[i], hy_b1[i], hy_f1[i], hy_w2[i], hy_b2[i], hy_f2[i], hy_w3[i])
            x0c, z = _hyena_pre(p, hy_conv_w[i], hy_conv_b[i], arr_rows)
            gla_c, sc_f, sc_b = _gla(p, waf, wab, baf, bab, gain, zeros, zeros, batch, lc, 0)
            gla_l, _, _ = _gla(p, waf, wab, baf, bab, gain, sc_f, sc_b, batch, s, tc // s)
            kt_c = _hyena_ktab(_hyena_filters(lc, *filt_args), lc)
            kt_l = _hyena_ktab(_hyena_filters(s, *filt_args), s)
            hy_c = _hyena_conv(z, x0c, kt_c, hy_skip[i], batch, lc, 0)
            hy_l = _hyena_conv(z, x0c, kt_l, hy_skip[i], batch, s, tc // s)
            h = _mix_out(gla_c, gla_l, hy_c, hy_l, ev_w_out[i].astype(BF16), hs, mods_l, arr_rows)
        else:
            h = _shortconv_out(p, od_conv_w[i], od_w_out[i].astype(BF16), h, mods_l, arr_rows)

        w_rt = jnp.concatenate([rt_w_grp[l], rt_w_exp[l],
                                jnp.zeros((d, LANE - N_GROUPS - N_EXPERTS), F32)], axis=1)
        b_rt = jnp.concatenate([rt_b_grp[l], rt_b_exp[l],
                                jnp.zeros((LANE - N_GROUPS - N_EXPERTS,), F32)]).reshape(1, LANE)
        v, meta, counts = _router(h, norm_ffn[l], mods_l, w_rt, b_rt,
                                  _Rows(batch, lc, s, not ctx_out, ROUTER_TILE))
        slot, src, blk_e, blk_used = _dispatch_plan(meta, counts)
        ys = _experts(v, src, blk_e, blk_used, ex_w1, ex_w3, ex_w2, l)
        moe = (ys, slot, meta, mods_l)
        hs = _as_stream(h, arr_rows)

    out = _combine_then(hs, moe[0], _slot_tiles(moe[1]), moe[2], moe[3], arr_rows, norm_final,
                        grid_seq=s if col_major_now else None)
    return out.reshape(batch, s, d)
```
